```python
import math
import jax, jax.numpy as jnp
from jax import lax
import numpy as np

D_MODEL = 2048
BATCH = 8
SEQ = 4096
DEPTH = 4

MIX_WIDTH = D_MODEL // 2
N_BRANCHES = 3
EPS = 1e-6
CHUNK = 128
GM_WIDTH = MIX_WIDTH
GM_GROUP_CH = 128
GM_GROUPS = GM_WIDTH // GM_GROUP_CH
S5_WIDTH = MIX_WIDTH
S5_GROUP_CH = 16
S5_GROUPS = S5_WIDTH // S5_GROUP_CH
S5_STATE = 64
DT_MIN = 1e-3
DT_MAX = 1e-1
SB_WIDTH = MIX_WIDTH
SB_HEAD_DIM = 128
SB_HEADS = SB_WIDTH // SB_HEAD_DIM
SB_BLOCK = 128
PROJ_COLS = 2 * GM_WIDTH + S5_WIDTH + 3 * SB_WIDTH + N_BRANCHES * D_MODEL
D_FF = 4 * D_MODEL

kernel_name = "hybrid_gated_gmlp_s5_stickbreak"


def rmsnorm(x, g):
    xf = x.astype(jnp.float32)
    y = xf * lax.rsqrt(jnp.mean(xf * xf, axis=-1, keepdims=True) + EPS)
    return (y * g.astype(jnp.float32)).astype(x.dtype)


def gmlp_mixer(uv, norm_g, w_s, b_s):
    bsz, seq, _ = uv.shape
    z = jax.nn.gelu(uv)
    u, v = jnp.split(z, 2, axis=-1)
    v = rmsnorm(v, norm_g)
    v = v.reshape(bsz, seq // CHUNK, CHUNK, GM_GROUPS, GM_GROUP_CH)
    w = jnp.tril(w_s)
    mixed = jnp.einsum('gts,bcsgh->bctgh', w, v) + jnp.transpose(b_s)[None, None, :, :, None]
    return u * mixed.reshape(bsz, seq, GM_WIDTH)


def s5_mixer(xin, lam_re, lam_im, log_dt, b_re, b_im, c_re, c_im, d, w_glu, b_glu):
    f32 = jnp.float32
    bsz, seq, _ = xin.shape
    u = xin.astype(f32).reshape(bsz, seq, S5_GROUPS, S5_GROUP_CH)
    lam_re = lam_re.astype(f32)
    lam_im = lam_im.astype(f32)
    b_re = b_re.astype(f32)
    b_im = b_im.astype(f32)
    dt = jnp.exp(log_dt.astype(f32))[:, None]
    mag = jnp.exp(lam_re * dt)
    ab_re = mag * jnp.cos(lam_im * dt)
    ab_im = mag * jnp.sin(lam_im * dt)
    den = lam_re * lam_re + lam_im * lam_im
    n_re = ab_re - 1.0
    n_im = ab_im
    k_re = (n_re * lam_re + n_im * lam_im) / den
    k_im = (n_im * lam_re - n_re * lam_im) / den
    bb_re = k_re[..., None] * b_re - k_im[..., None] * b_im
    bb_im = k_re[..., None] * b_im + k_im[..., None] * b_re
    bu_re = jnp.einsum('blgh,gph->blgp', u, bb_re)
    bu_im = jnp.einsum('blgh,gph->blgp', u, bb_im)
    a_re = jnp.broadcast_to(ab_re, bu_re.shape)
    a_im = jnp.broadcast_to(ab_im, bu_im.shape)

    def combine(e_i, e_j):
        ar_i, ai_i, br_i, bi_i = e_i
        ar_j, ai_j, br_j, bi_j = e_j
        return (ar_j * ar_i - ai_j * ai_i,
                ar_j * ai_i + ai_j * ar_i,
                ar_j * br_i - ai_j * bi_i + br_j,
                ar_j * bi_i + ai_j * br_i + bi_j)

    _, _, s_re, s_im = lax.associative_scan(combine, (a_re, a_im, bu_re, bu_im), axis=1)
    y = (jnp.einsum('blgp,ghp->blgh', s_re, c_re.astype(f32))
         - jnp.einsum('blgp,ghp->blgh', s_im, c_im.astype(f32)))
    y = y.reshape(bsz, seq, S5_WIDTH) + d.astype(f32) * xin.astype(f32)
    g = jax.nn.gelu(y)
    out = g * jax.nn.sigmoid(g @ w_glu.astype(f32) + b_glu.astype(f32))
    return out.astype(xin.dtype)


def stick_breaking_attention(q, k, v):
    f32 = jnp.float32
    seq = q.shape[2]
    scale = SB_HEAD_DIM ** -0.5
    outs = []
    for blk in range(seq // SB_BLOCK):
        t0 = blk * SB_BLOCK
        t1 = t0 + SB_BLOCK
        qb = q[:, :, t0:t1].astype(f32)
        kb = k[:, :, :t1].astype(f32)
        vb = v[:, :, :t1].astype(f32)
        z = jnp.einsum('bhtd,bhsd->bhts', qb, kb) * scale
        mask = jnp.arange(t1)[None, :] < jnp.arange(t0, t1)[:, None]
        log_1m = jnp.where(mask, jax.nn.log_sigmoid(-z), 0.0)
        after = lax.cumsum(log_1m, axis=3, reverse=True) - log_1m
        w = jnp.where(mask, jnp.exp(jax.nn.log_sigmoid(z) + after), 0.0)
        outs.append(jnp.einsum('bhts,bhsd->bhtd', w, vb))
    return jnp.concatenate(outs, axis=2).astype(q.dtype)


def _fwd_setup_inputs(seed: int = 0) -> dict:
    key = jax.random.key(seed)
    ks = jax.random.split(key, 32)

    def nrm(k, shape, scale):
        return jax.random.normal(k, shape, jnp.float32) * scale

    x = nrm(ks[0], (BATCH, SEQ, D_MODEL), 1.0)
    norm1_g = 1.0 + nrm(ks[1], (DEPTH, D_MODEL), 0.05)
    w_in = nrm(ks[2], (DEPTH, D_MODEL, PROJ_COLS), D_MODEL ** -0.5)
    b_gate = nrm(ks[3], (DEPTH, N_BRANCHES, D_MODEL), 0.1)
    gm_norm_g = 1.0 + nrm(ks[4], (DEPTH, GM_WIDTH), 0.05)
    gm_w_s = nrm(ks[5], (DEPTH, GM_GROUPS, CHUNK, CHUNK), CHUNK ** -0.5)
    gm_b_s = 1.0 + nrm(ks[6], (DEPTH, GM_GROUPS, CHUNK), 0.1)
    s5_lambda_re = -0.5 + nrm(ks[7], (DEPTH, S5_GROUPS, S5_STATE), 0.01)
    s5_lambda_im = (math.pi * jnp.arange(S5_STATE, dtype=jnp.float32))[None, None, :] \
        + nrm(ks[8], (DEPTH, S5_GROUPS, S5_STATE), 0.01)
    s5_log_dt = jax.random.uniform(ks[9], (DEPTH, S5_GROUPS), jnp.float32,
                                   math.log(DT_MIN), math.log(DT_MAX))
    s5_b_re = nrm(ks[10], (DEPTH, S5_GROUPS, S5_STATE, S5_GROUP_CH), (2 * S5_GROUP_CH) ** -0.5)
    s5_b_im = nrm(ks[11], (DEPTH, S5_GROUPS, S5_STATE, S5_GROUP_CH), (2 * S5_GROUP_CH) ** -0.5)
    s5_c_re = nrm(ks[12], (DEPTH, S5_GROUPS, S5_GROUP_CH, S5_STATE), (2 * S5_STATE) ** -0.5)
    s5_c_im = nrm(ks[13], (DEPTH, S5_GROUPS, S5_GROUP_CH, S5_STATE), (2 * S5_STATE) ** -0.5)
    s5_d = nrm(ks[14], (DEPTH, S5_WIDTH), 1.0)
    s5_w_glu = nrm(ks[15], (DEPTH, S5_WIDTH, S5_WIDTH), S5_WIDTH ** -0.5)
    s5_b_glu = nrm(ks[16], (DEPTH, S5_WIDTH), 0.02)
    w_branch = nrm(ks[17], (DEPTH, N_BRANCHES, MIX_WIDTH, D_MODEL), MIX_WIDTH ** -0.5)
    w_out = nrm(ks[18], (DEPTH, D_MODEL, D_MODEL), D_MODEL ** -0.5)
    norm2_g = 1.0 + nrm(ks[19], (DEPTH, D_MODEL), 0.05)
    w_mlp_in = nrm(ks[20], (DEPTH, D_MODEL, D_FF), D_MODEL ** -0.5)
    w_mlp_out = nrm(ks[21], (DEPTH, D_FF, D_MODEL), D_FF ** -0.5)
    final_g = 1.0 + nrm(ks[22], (D_MODEL,), 0.05)
    return {"x": x, "norm1_g": norm1_g, "w_in": w_in, "b_gate": b_gate,
            "gm_norm_g": gm_norm_g, "gm_w_s": gm_w_s, "gm_b_s": gm_b_s,
            "s5_lambda_re": s5_lambda_re, "s5_lambda_im": s5_lambda_im, "s5_log_dt": s5_log_dt,
            "s5_b_re": s5_b_re, "s5_b_im": s5_b_im, "s5_c_re": s5_c_re, "s5_c_im": s5_c_im,
            "s5_d": s5_d, "s5_w_glu": s5_w_glu, "s5_b_glu": s5_b_glu,
            "w_branch": w_branch, "w_out": w_out, "norm2_g": norm2_g,
            "w_mlp_in": w_mlp_in, "w_mlp_out": w_mlp_out, "final_g": final_g}


def _fwd_reference(x, norm1_g, w_in, b_gate, gm_norm_g, gm_w_s, gm_b_s,
              s5_lambda_re, s5_lambda_im, s5_log_dt, s5_b_re, s5_b_im, s5_c_re, s5_c_im,
              s5_d, s5_w_glu, s5_b_glu, w_branch, w_out, norm2_g,
              w_mlp_in, w_mlp_out, final_g):
    bsz, seq, _ = x.shape
    o_a = 2 * GM_WIDTH
    o_b = o_a + S5_WIDTH
    o_c = o_b + 3 * SB_WIDTH
    for l in range(DEPTH):
        h = rmsnorm(x, norm1_g[l])
        proj = h @ w_in[l]
        uv = proj[..., :o_a]
        s5_in = proj[..., o_a:o_b]
        qkv = proj[..., o_b:o_c].reshape(bsz, seq, 3, SB_HEADS, SB_HEAD_DIM)
        qkv = jnp.transpose(qkv, (2, 0, 3, 1, 4))
        gates = jax.nn.sigmoid(proj[..., o_c:].reshape(bsz, seq, N_BRANCHES, D_MODEL) + b_gate[l])

        y_a = gmlp_mixer(uv, gm_norm_g[l], gm_w_s[l], gm_b_s[l])
        y_b = s5_mixer(s5_in, s5_lambda_re[l], s5_lambda_im[l], s5_log_dt[l],
                       s5_b_re[l], s5_b_im[l], s5_c_re[l], s5_c_im[l],
                       s5_d[l], s5_w_glu[l], s5_b_glu[l])
        y_c = stick_breaking_attention(qkv[0], qkv[1], qkv[2])
        y_c = jnp.transpose(y_c, (0, 2, 1, 3)).reshape(bsz, seq, SB_WIDTH)

        ys = jnp.stack([y_a, y_b, y_c], axis=2)
        br = jnp.einsum('blnw,nwd->blnd', ys, w_branch[l])
        merged = jnp.sum(gates * br, axis=2)
        x = x + merged @ w_out[l]
        h2 = rmsnorm(x, norm2_g[l])
        x = x + jnp.square(jax.nn.relu(h2 @ w_mlp_in[l])) @ w_mlp_out[l]
    return rmsnorm(x, final_g)


import jax as _jax
import jax.numpy as _jnp

TWIN_FORMAT = 'train_step'
FWD_PARAMS = ['x', 'norm1_g', 'w_in', 'b_gate', 'gm_norm_g', 'gm_w_s', 'gm_b_s', 's5_lambda_re', 's5_lambda_im', 's5_log_dt', 's5_b_re', 's5_b_im', 's5_c_re', 's5_c_im', 's5_d', 's5_w_glu', 's5_b_glu', 'w_branch', 'w_out', 'norm2_g', 'w_mlp_in', 'w_mlp_out', 'final_g']
TWIN_WEIGHTS = ['norm1_g', 'w_in', 'b_gate', 'gm_norm_g', 'gm_w_s', 'gm_b_s', 's5_lambda_re', 's5_lambda_im', 's5_log_dt', 's5_b_re', 's5_b_im', 's5_c_re', 's5_c_im', 's5_d', 's5_w_glu', 's5_b_glu', 'w_branch', 'w_out', 'norm2_g', 'w_mlp_in', 'w_mlp_out', 'final_g']
TWIN_DIFF_INPUT = 'x'
TWIN_INPUTS = ['x', 'norm1_g', 'w_in', 'b_gate', 'gm_norm_g', 'gm_w_s', 'gm_b_s', 's5_lambda_re', 's5_lambda_im', 's5_log_dt', 's5_b_re', 's5_b_im', 's5_c_re', 's5_c_im', 's5_d', 's5_w_glu', 's5_b_glu', 'w_branch', 'w_out', 'norm2_g', 'w_mlp_in', 'w_mlp_out', 'final_g', 'loss_target', 'm_norm1_g', 'm_w_in', 'm_b_gate', 'm_gm_norm_g', 'm_gm_w_s', 'm_gm_b_s', 'm_s5_lambda_re', 'm_s5_lambda_im', 'm_s5_log_dt', 'm_s5_b_re', 'm_s5_b_im', 'm_s5_c_re', 'm_s5_c_im', 'm_s5_d', 'm_s5_w_glu', 'm_s5_b_glu', 'm_w_branch', 'm_w_out', 'm_norm2_g', 'm_w_mlp_in', 'm_w_mlp_out', 'm_final_g', 'v_norm1_g', 'v_w_in', 'v_b_gate', 'v_gm_norm_g', 'v_gm_w_s', 'v_gm_b_s', 'v_s5_lambda_re', 'v_s5_lambda_im', 'v_s5_log_dt', 'v_s5_b_re', 'v_s5_b_im', 'v_s5_c_re', 'v_s5_c_im', 'v_s5_d', 'v_s5_w_glu', 'v_s5_b_glu', 'v_w_branch', 'v_w_out', 'v_norm2_g', 'v_w_mlp_in', 'v_w_mlp_out', 'v_final_g']
TWIN_OUTPUTS = ['loss', 'grad_x', 'grad_norm1_g', 'grad_w_in', 'grad_b_gate', 'grad_gm_norm_g', 'grad_gm_w_s', 'grad_gm_b_s', 'grad_s5_lambda_re', 'grad_s5_lambda_im', 'grad_s5_log_dt', 'grad_s5_b_re', 'grad_s5_b_im', 'grad_s5_c_re', 'grad_s5_c_im', 'grad_s5_d', 'grad_s5_w_glu', 'grad_s5_b_glu', 'grad_w_branch', 'grad_w_out', 'grad_norm2_g', 'grad_w_mlp_in', 'grad_w_mlp_out', 'grad_final_g', 'delta_norm1_g', 'delta_w_in', 'delta_b_gate', 'delta_gm_norm_g', 'delta_gm_w_s', 'delta_gm_b_s', 'delta_s5_lambda_re', 'delta_s5_lambda_im', 'delta_s5_log_dt', 'delta_s5_b_re', 'delta_s5_b_im', 'delta_s5_c_re', 'delta_s5_c_im', 'delta_s5_d', 'delta_s5_w_glu', 'delta_s5_b_glu', 'delta_w_branch', 'delta_w_out', 'delta_norm2_g', 'delta_w_mlp_in', 'delta_w_mlp_out', 'delta_final_g', 'new_m_norm1_g', 'new_m_w_in', 'new_m_b_gate', 'new_m_gm_norm_g', 'new_m_gm_w_s', 'new_m_gm_b_s', 'new_m_s5_lambda_re', 'new_m_s5_lambda_im', 'new_m_s5_log_dt', 'new_m_s5_b_re', 'new_m_s5_b_im', 'new_m_s5_c_re', 'new_m_s5_c_im', 'new_m_s5_d', 'new_m_s5_w_glu', 'new_m_s5_b_glu', 'new_m_w_branch', 'new_m_w_out', 'new_m_norm2_g', 'new_m_w_mlp_in', 'new_m_w_mlp_out', 'new_m_final_g', 'new_v_norm1_g', 'new_v_w_in', 'new_v_b_gate', 'new_v_gm_norm_g', 'new_v_gm_w_s', 'new_v_gm_b_s', 'new_v_s5_lambda_re', 'new_v_s5_lambda_im', 'new_v_s5_log_dt', 'new_v_s5_b_re', 'new_v_s5_b_im', 'new_v_s5_c_re', 'new_v_s5_c_im', 'new_v_s5_d', 'new_v_s5_w_glu', 'new_v_s5_b_glu', 'new_v_w_branch', 'new_v_w_out', 'new_v_norm2_g', 'new_v_w_mlp_in', 'new_v_w_mlp_out', 'new_v_final_g']
TWIN_LEAF_KINDS = {'loss': 'loss', 'grad_x': 'grad_x', 'grad_norm1_g': 'grad_w', 'grad_w_in': 'grad_w', 'grad_b_gate': 'grad_w', 'grad_gm_norm_g': 'grad_w', 'grad_gm_w_s': 'grad_w', 'grad_gm_b_s': 'grad_w', 'grad_s5_lambda_re': 'grad_w', 'grad_s5_lambda_im': 'grad_w', 'grad_s5_log_dt': 'grad_w', 'grad_s5_b_re': 'grad_w', 'grad_s5_b_im': 'grad_w', 'grad_s5_c_re': 'grad_w', 'grad_s5_c_im': 'grad_w', 'grad_s5_d': 'grad_w', 'grad_s5_w_glu': 'grad_w', 'grad_s5_b_glu': 'grad_w', 'grad_w_branch': 'grad_w', 'grad_w_out': 'grad_w', 'grad_norm2_g': 'grad_w', 'grad_w_mlp_in': 'grad_w', 'grad_w_mlp_out': 'grad_w', 'grad_final_g': 'grad_w', 'delta_norm1_g': 'delta_w', 'delta_w_in': 'delta_w', 'delta_b_gate': 'delta_w', 'delta_gm_norm_g': 'delta_w', 'delta_gm_w_s': 'delta_w', 'delta_gm_b_s': 'delta_w', 'delta_s5_lambda_re': 'delta_w', 'delta_s5_lambda_im': 'delta_w', 'delta_s5_log_dt': 'delta_w', 'delta_s5_b_re': 'delta_w', 'delta_s5_b_im': 'delta_w', 'delta_s5_c_re': 'delta_w', 'delta_s5_c_im': 'delta_w', 'delta_s5_d': 'delta_w', 'delta_s5_w_glu': 'delta_w', 'delta_s5_b_glu': 'delta_w', 'delta_w_branch': 'delta_w', 'delta_w_out': 'delta_w', 'delta_norm2_g': 'delta_w', 'delta_w_mlp_in': 'delta_w', 'delta_w_mlp_out': 'delta_w', 'delta_final_g': 'delta_w', 'new_m_norm1_g': 'new_m', 'new_m_w_in': 'new_m', 'new_m_b_gate': 'new_m', 'new_m_gm_norm_g': 'new_m', 'new_m_gm_w_s': 'new_m', 'new_m_gm_b_s': 'new_m', 'new_m_s5_lambda_re': 'new_m', 'new_m_s5_lambda_im': 'new_m', 'new_m_s5_log_dt': 'new_m', 'new_m_s5_b_re': 'new_m', 'new_m_s5_b_im': 'new_m', 'new_m_s5_c_re': 'new_m', 'new_m_s5_c_im': 'new_m', 'new_m_s5_d': 'new_m', 'new_m_s5_w_glu': 'new_m', 'new_m_s5_b_glu': 'new_m', 'new_m_w_branch': 'new_m', 'new_m_w_out': 'new_m', 'new_m_norm2_g': 'new_m', 'new_m_w_mlp_in': 'new_m', 'new_m_w_mlp_out': 'new_m', 'new_m_final_g': 'new_m', 'new_v_norm1_g': 'new_v', 'new_v_w_in': 'new_v', 'new_v_b_gate': 'new_v', 'new_v_gm_norm_g': 'new_v', 'new_v_gm_w_s': 'new_v', 'new_v_gm_b_s': 'new_v', 'new_v_s5_lambda_re': 'new_v', 'new_v_s5_lambda_im': 'new_v', 'new_v_s5_log_dt': 'new_v', 'new_v_s5_b_re': 'new_v', 'new_v_s5_b_im': 'new_v', 'new_v_s5_c_re': 'new_v', 'new_v_s5_c_im': 'new_v', 'new_v_s5_d': 'new_v', 'new_v_s5_w_glu': 'new_v', 'new_v_s5_b_glu': 'new_v', 'new_v_w_branch': 'new_v', 'new_v_w_out': 'new_v', 'new_v_norm2_g': 'new_v', 'new_v_w_mlp_in': 'new_v', 'new_v_w_mlp_out': 'new_v', 'new_v_final_g': 'new_v'}


def _forward(args):
    return _fwd_reference(*[args[k] for k in FWD_PARAMS])


def _output_shape():
    def fwd():
        inp = _fwd_setup_inputs(0)
        return _fwd_reference(*[inp[k] for k in FWD_PARAMS])
    out = _jax.eval_shape(fwd)
    return out.shape, out.dtype

N_MICROBATCH = 1
ADAM_LR = 0.001
ADAM_B1 = 0.9
ADAM_B2 = 0.999
ADAM_EPS = 1e-08
ADAM_WD = 0.01
ADAM_STEP = 10
PER_EXAMPLE_BATCH_AXIS = {'x': 0, 'loss_target': 0}
SHARED_INPUTS = []
_WEIGHT_DTYPES = {'norm1_g': _jnp.float32, 'w_in': _jnp.float32, 'b_gate': _jnp.float32, 'gm_norm_g': _jnp.float32, 'gm_w_s': _jnp.float32, 'gm_b_s': _jnp.float32, 's5_lambda_re': _jnp.float32, 's5_lambda_im': _jnp.float32, 's5_log_dt': _jnp.float32, 's5_b_re': _jnp.float32, 's5_b_im': _jnp.float32, 's5_c_re': _jnp.float32, 's5_c_im': _jnp.float32, 's5_d': _jnp.float32, 's5_w_glu': _jnp.float32, 's5_b_glu': _jnp.float32, 'w_branch': _jnp.float32, 'w_out': _jnp.float32, 'norm2_g': _jnp.float32, 'w_mlp_in': _jnp.float32, 'w_mlp_out': _jnp.float32, 'final_g': _jnp.float32}
MOMENT_SCALE = {'norm1_g': 5.901958e-02, 'w_in': 2.413324e-02, 'b_gate': 1.535821e-02, 'gm_norm_g': 2.199625e-02, 'gm_w_s': 2.206915e-02, 'gm_b_s': 3.133242e-02, 's5_lambda_re': 1.983667e-03, 's5_lambda_im': 2.631801e-03, 's5_log_dt': 4.882759e-01, 's5_b_re': 1.203400e-03, 's5_b_im': 1.200909e-03, 's5_c_re': 2.497303e-03, 's5_c_im': 2.365142e-03, 's5_d': 3.612256e-02, 's5_w_glu': 6.519012e-03, 's5_b_glu': 1.438129e-02, 'w_branch': 3.773867e-02, 'w_out': 6.534412e-02, 'norm2_g': 8.428129e-02, 'w_mlp_in': 4.212676e-02, 'w_mlp_out': 1.384105e-01, 'final_g': 1.652612e+01}


def _to_microbatches(a, axis):
    t = _jnp.moveaxis(a, axis, 0)
    t = t.reshape((N_MICROBATCH, t.shape[0] // N_MICROBATCH) + t.shape[1:])
    return _jnp.moveaxis(t, 1, axis + 1)


def setup_inputs(seed: int = 0) -> dict:
    inp = _fwd_setup_inputs(seed)
    key = _jax.random.fold_in(_jax.random.key(seed), 7919)
    shape, _ = _output_shape()
    out = dict(inp)
    out["loss_target"] = _jax.random.normal(_jax.random.fold_in(key, 0), shape, _jnp.float32)
    for i, name in enumerate(TWIN_WEIGHTS):
        w = inp[name].astype(_jnp.float32)
        if MOMENT_SCALE is None:
            s = _jnp.sqrt(_jnp.mean(_jnp.square(w)) + 1e-30)
        else:
            s = MOMENT_SCALE[name]
        km, kv = _jax.random.split(_jax.random.fold_in(key, i + 1))
        out[name] = w
        out["m_" + name] = s * _jax.random.normal(km, w.shape, _jnp.float32)
        out["v_" + name] = (s * s) * _jax.random.uniform(kv, w.shape, _jnp.float32, 0.5, 1.5)
    if N_MICROBATCH > 1:
        for name, axis in PER_EXAMPLE_BATCH_AXIS.items():
            out[name] = _to_microbatches(out[name], axis)
    return {'x': out['x'], 'norm1_g': out['norm1_g'], 'w_in': out['w_in'], 'b_gate': out['b_gate'], 'gm_norm_g': out['gm_norm_g'], 'gm_w_s': out['gm_w_s'], 'gm_b_s': out['gm_b_s'], 's5_lambda_re': out['s5_lambda_re'], 's5_lambda_im': out['s5_lambda_im'], 's5_log_dt': out['s5_log_dt'], 's5_b_re': out['s5_b_re'], 's5_b_im': out['s5_b_im'], 's5_c_re': out['s5_c_re'], 's5_c_im': out['s5_c_im'], 's5_d': out['s5_d'], 's5_w_glu': out['s5_w_glu'], 's5_b_glu': out['s5_b_glu'], 'w_branch': out['w_branch'], 'w_out': out['w_out'], 'norm2_g': out['norm2_g'], 'w_mlp_in': out['w_mlp_in'], 'w_mlp_out': out['w_mlp_out'], 'final_g': out['final_g'], 'loss_target': out['loss_target'], 'm_norm1_g': out['m_norm1_g'], 'm_w_in': out['m_w_in'], 'm_b_gate': out['m_b_gate'], 'm_gm_norm_g': out['m_gm_norm_g'], 'm_gm_w_s': out['m_gm_w_s'], 'm_gm_b_s': out['m_gm_b_s'], 'm_s5_lambda_re': out['m_s5_lambda_re'], 'm_s5_lambda_im': out['m_s5_lambda_im'], 'm_s5_log_dt': out['m_s5_log_dt'], 'm_s5_b_re': out['m_s5_b_re'], 'm_s5_b_im': out['m_s5_b_im'], 'm_s5_c_re': out['m_s5_c_re'], 'm_s5_c_im': out['m_s5_c_im'], 'm_s5_d': out['m_s5_d'], 'm_s5_w_glu': out['m_s5_w_glu'], 'm_s5_b_glu': out['m_s5_b_glu'], 'm_w_branch': out['m_w_branch'], 'm_w_out': out['m_w_out'], 'm_norm2_g': out['m_norm2_g'], 'm_w_mlp_in': out['m_w_mlp_in'], 'm_w_mlp_out': out['m_w_mlp_out'], 'm_final_g': out['m_final_g'], 'v_norm1_g': out['v_norm1_g'], 'v_w_in': out['v_w_in'], 'v_b_gate': out['v_b_gate'], 'v_gm_norm_g': out['v_gm_norm_g'], 'v_gm_w_s': out['v_gm_w_s'], 'v_gm_b_s': out['v_gm_b_s'], 'v_s5_lambda_re': out['v_s5_lambda_re'], 'v_s5_lambda_im': out['v_s5_lambda_im'], 'v_s5_log_dt': out['v_s5_log_dt'], 'v_s5_b_re': out['v_s5_b_re'], 'v_s5_b_im': out['v_s5_b_im'], 'v_s5_c_re': out['v_s5_c_re'], 'v_s5_c_im': out['v_s5_c_im'], 'v_s5_d': out['v_s5_d'], 'v_s5_w_glu': out['v_s5_w_glu'], 'v_s5_b_glu': out['v_s5_b_glu'], 'v_w_branch': out['v_w_branch'], 'v_w_out': out['v_w_out'], 'v_norm2_g': out['v_norm2_g'], 'v_w_mlp_in': out['v_w_mlp_in'], 'v_w_mlp_out': out['v_w_mlp_out'], 'v_final_g': out['v_final_g']}


def _loss(weights, diff, rest, loss_target):
    with _jax.named_scope("forward"):
        args = {**rest, TWIN_DIFF_INPUT: diff, **{k: w.astype(_WEIGHT_DTYPES[k]) for k, w in weights.items()}}
        y = _forward(args)
    with _jax.named_scope("loss_head"):
        err = _jnp.square(y.astype(_jnp.float32) - loss_target)
        return 0.5 * _jnp.sum(_jnp.mean(err, axis=-1)) if err.ndim else 0.5 * err


def _adamw(w, g, m, v):
    m = ADAM_B1 * m + (1.0 - ADAM_B1) * g
    v = ADAM_B2 * v + (1.0 - ADAM_B2) * _jnp.square(g)
    m_hat = m / (1.0 - ADAM_B1 ** ADAM_STEP)
    v_hat = v / (1.0 - ADAM_B2 ** ADAM_STEP)
    delta = -ADAM_LR * (m_hat / (_jnp.sqrt(v_hat) + ADAM_EPS) + ADAM_WD * w)
    return delta, m, v


def reference(x, norm1_g, w_in, b_gate, gm_norm_g, gm_w_s, gm_b_s, s5_lambda_re, s5_lambda_im, s5_log_dt, s5_b_re, s5_b_im, s5_c_re, s5_c_im, s5_d, s5_w_glu, s5_b_glu, w_branch, w_out, norm2_g, w_mlp_in, w_mlp_out, final_g, loss_target, m_norm1_g, m_w_in, m_b_gate, m_gm_norm_g, m_gm_w_s, m_gm_b_s, m_s5_lambda_re, m_s5_lambda_im, m_s5_log_dt, m_s5_b_re, m_s5_b_im, m_s5_c_re, m_s5_c_im, m_s5_d, m_s5_w_glu, m_s5_b_glu, m_w_branch, m_w_out, m_norm2_g, m_w_mlp_in, m_w_mlp_out, m_final_g, v_norm1_g, v_w_in, v_b_gate, v_gm_norm_g, v_gm_w_s, v_gm_b_s, v_s5_lambda_re, v_s5_lambda_im, v_s5_log_dt, v_s5_b_re, v_s5_b_im, v_s5_c_re, v_s5_c_im, v_s5_d, v_s5_w_glu, v_s5_b_glu, v_w_branch, v_w_out, v_norm2_g, v_w_mlp_in, v_w_mlp_out, v_final_g):
    given = dict(x=x, norm1_g=norm1_g, w_in=w_in, b_gate=b_gate, gm_norm_g=gm_norm_g, gm_w_s=gm_w_s, gm_b_s=gm_b_s, s5_lambda_re=s5_lambda_re, s5_lambda_im=s5_lambda_im, s5_log_dt=s5_log_dt, s5_b_re=s5_b_re, s5_b_im=s5_b_im, s5_c_re=s5_c_re, s5_c_im=s5_c_im, s5_d=s5_d, s5_w_glu=s5_w_glu, s5_b_glu=s5_b_glu, w_branch=w_branch, w_out=w_out, norm2_g=norm2_g, w_mlp_in=w_mlp_in, w_mlp_out=w_mlp_out, final_g=final_g, loss_target=loss_target, m_norm1_g=m_norm1_g, m_w_in=m_w_in, m_b_gate=m_b_gate, m_gm_norm_g=m_gm_norm_g, m_gm_w_s=m_gm_w_s, m_gm_b_s=m_gm_b_s, m_s5_lambda_re=m_s5_lambda_re, m_s5_lambda_im=m_s5_lambda_im, m_s5_log_dt=m_s5_log_dt, m_s5_b_re=m_s5_b_re, m_s5_b_im=m_s5_b_im, m_s5_c_re=m_s5_c_re, m_s5_c_im=m_s5_c_im, m_s5_d=m_s5_d, m_s5_w_glu=m_s5_w_glu, m_s5_b_glu=m_s5_b_glu, m_w_branch=m_w_branch, m_w_out=m_w_out, m_norm2_g=m_norm2_g, m_w_mlp_in=m_w_mlp_in, m_w_mlp_out=m_w_mlp_out, m_final_g=m_final_g, v_norm1_g=v_norm1_g, v_w_in=v_w_in, v_b_gate=v_b_gate, v_gm_norm_g=v_gm_norm_g, v_gm_w_s=v_gm_w_s, v_gm_b_s=v_gm_b_s, v_s5_lambda_re=v_s5_lambda_re, v_s5_lambda_im=v_s5_lambda_im, v_s5_log_dt=v_s5_log_dt, v_s5_b_re=v_s5_b_re, v_s5_b_im=v_s5_b_im, v_s5_c_re=v_s5_c_re, v_s5_c_im=v_s5_c_im, v_s5_d=v_s5_d, v_s5_w_glu=v_s5_w_glu, v_s5_b_glu=v_s5_b_glu, v_w_branch=v_w_branch, v_w_out=v_w_out, v_norm2_g=v_norm2_g, v_w_mlp_in=v_w_mlp_in, v_w_mlp_out=v_w_mlp_out, v_final_g=v_final_g)
    weights = {n: given[n] for n in TWIN_WEIGHTS}
    shared = {n: given[n] for n in SHARED_INPUTS}
    per_example = {n: given[n] for n in ['x']}
    grad_fn = _jax.value_and_grad(_loss, argnums=(0, 1))

    def one_microbatch(ex, loss_target):
        ex = dict(ex)
        diff = ex.pop(TWIN_DIFF_INPUT)
        return grad_fn(weights, diff, {**shared, **ex}, loss_target)

    if N_MICROBATCH == 1:
        loss, (grad_w, grad_x) = one_microbatch(per_example, given["loss_target"])
    else:
        def body(carry, xs):
            loss_sum, grad_sum = carry
            l_k, (gw_k, gx_k) = one_microbatch(xs[0], xs[1])
            with _jax.named_scope("update"):
                return (loss_sum + l_k, _jax.tree.map(_jnp.add, grad_sum, gw_k)), gx_k

        init = (_jnp.zeros((), _jnp.float32), _jax.tree.map(_jnp.zeros_like, weights))
        (loss, grad_w), grad_x = _jax.lax.scan(body, init, (per_example, given["loss_target"]))
    with _jax.named_scope("update"):
        delta_w, new_m, new_v = {}, {}, {}
        for n in TWIN_WEIGHTS:
            delta_w[n], new_m[n], new_v[n] = _adamw(weights[n], grad_w[n], given["m_" + n], given["v_" + n])
    return (loss, grad_x, *[grad_w[n] for n in TWIN_WEIGHTS], *[delta_w[n] for n in TWIN_WEIGHTS],
            *[new_m[n] for n in TWIN_WEIGHTS], *[new_v[n] for n in TWIN_WEIGHTS])
```

```python
import functools
import math

import jax
import jax.numpy as jnp
from jax import lax
from jax.experimental import pallas as pl
from jax.experimental.pallas import tpu as pltpu

F32 = jnp.float32
BF16 = jnp.bfloat16
MESH = pl.DeviceIdType.MESH
NDEV = 8
NCHIP = 4

EPS = 1e-6
CHUNK = 128
S5_GROUP_CH = 16
S5_STATE = 64
HEAD_DIM = 128
DT_MIN = 1e-3
DT_MAX = 1e-1
ADAM_LR = 0.001
ADAM_B1 = 0.9
ADAM_B2 = 0.999
ADAM_EPS = 1e-08
ADAM_WD = 0.01
ADAM_STEP = 10

V7X_VMEM_BYTES = 64 * 2**20
VMEM_LIMIT_BYTES = V7X_VMEM_BYTES - 8 * 2**20
SUBLANES = 8
LANES = 128

NN = (((1,), (0,)), ((), ()))
NT = (((1,), (1,)), ((), ()))
TN = (((0,), (0,)), ((), ()))

HBM = pl.BlockSpec(memory_space=pltpu.HBM)
ANY = pl.BlockSpec(memory_space=pl.ANY)


def _cp(**kw):
    return pltpu.CompilerParams(vmem_limit_bytes=VMEM_LIMIT_BYTES, **kw)


def _tile(n, pref, align=SUBLANES):
    if n <= pref:
        return n
    t = (pref // align) * align
    while t >= align:
        if n % t == 0:
            return t
        t -= align
    return n


def _dot(a, b, dims=NN):
    return lax.dot_general(a, b, dims, preferred_element_type=F32)


def _sds(shape, dtype):
    return jax.ShapeDtypeStruct(tuple(shape), dtype)


def _mesh_pos():
    return lax.axis_index("x"), lax.axis_index("y"), lax.axis_index("c")


def _all_gather(arrays, name):
    n = len(arrays)

    def body(*refs):
        ins, outs = refs[:n], refs[n:2 * n]
        send_sems, recv_sems, local_sems = refs[2 * n:]
        x, y, c = _mesh_pos()
        me, sib = (x, y, c), (x, y, 1 - c)
        chips = [(1 - x, y), (x, 1 - y), (1 - x, 1 - y)]

        def slot(t, p):
            return outs[t].at[:, 4 * p[0] + 2 * p[1] + p[2]]

        def copy(t, k, block, to, src=None):
            dst = slot(t, block)
            return pltpu.make_async_remote_copy(
                src_ref=dst if src is None else src, dst_ref=dst,
                send_sem=send_sems.at[t, k], recv_sem=recv_sems.at[t, k],
                device_id=to, device_id_type=MESH)

        mine = [pltpu.make_async_copy(ins[t], slot(t, me), local_sems.at[t]) for t in range(n)]
        for cp in mine:
            cp.start()
        first = []
        for t in range(n):
            first.append(copy(t, 0, me, sib, src=ins[t]))
            first += [copy(t, 1 + j, me, (*chip, c), src=ins[t]) for j, chip in enumerate(chips)]
        for cp in first:
            cp.start()
        passed = []
        for j, chip in enumerate(chips):
            for t in range(n):
                copy(t, 1 + j, (*chip, c), me).wait_recv()
                fw = copy(t, 4 + j, (*chip, c), sib)
                fw.start()
                passed.append(fw)
        for t in range(n):
            copy(t, 0, sib, me).wait_recv()
            for j, chip in enumerate(chips):
                copy(t, 4 + j, (*chip, 1 - c), me).wait_recv()
        for cp in first + passed:
            cp.wait_send()
        for cp in mine:
            cp.wait()

    outs = pl.pallas_call(
        body, name=name,
        out_shape=[_sds((a.shape[0], NDEV) + a.shape[1:], a.dtype) for a in arrays],
        in_specs=[HBM] * n, out_specs=[HBM] * n,
        scratch_shapes=[pltpu.SemaphoreType.DMA((n, 7)), pltpu.SemaphoreType.DMA((n, 7)),
                        pltpu.SemaphoreType.DMA((n,))],
    )(*arrays)
    return list(outs)


def _rs_pair(gs, name):
    n = len(gs)

    def body(*refs):
        ins, outs = refs[:n], refs[n:2 * n]
        send_sems, recv_sems = refs[2 * n:]
        x, y, c = _mesh_pos()
        copies = []
        for t in range(n):
            for p in range(NCHIP):
                copies.append(pltpu.make_async_remote_copy(
                    src_ref=ins[t].at[:, 2 * p + (1 - c)], dst_ref=outs[t].at[:, p],
                    send_sem=send_sems.at[t, p], recv_sem=recv_sems.at[t, p],
                    device_id=(x, y, 1 - c), device_id_type=MESH))
        for cp in copies:
            cp.start()
        for cp in copies:
            cp.wait()

    outs = pl.pallas_call(
        body, name=name,
        out_shape=[_sds((g.shape[0], NCHIP) + g.shape[2:], g.dtype) for g in gs],
        in_specs=[HBM] * n, out_specs=[HBM] * n,
        scratch_shapes=[pltpu.SemaphoreType.DMA((n, NCHIP)), pltpu.SemaphoreType.DMA((n, NCHIP))],
    )(*gs)
    return list(outs)


def _rs_chips(ss, name):
    n = len(ss)

    def body(*refs):
        ins, outs = refs[:n], refs[n:2 * n]
        send_sems, recv_sems = refs[2 * n:]
        x, y, c = _mesh_pos()
        chips = [(1 - x, y), (x, 1 - y), (1 - x, 1 - y)]
        copies = []
        for t in range(n):
            for j, chip in enumerate(chips):
                copies.append(pltpu.make_async_remote_copy(
                    src_ref=ins[t].at[:, 2 * chip[0] + chip[1]], dst_ref=outs[t].at[:, j],
                    send_sem=send_sems.at[t, j], recv_sem=recv_sems.at[t, j],
                    device_id=(*chip, c), device_id_type=MESH))
        for cp in copies:
            cp.start()
        for cp in copies:
            cp.wait()

    outs = pl.pallas_call(
        body, name=name,
        out_shape=[_sds((s.shape[0], 3) + s.shape[2:], s.dtype) for s in ss],
        in_specs=[HBM] * n, out_specs=[HBM] * n,
        scratch_shapes=[pltpu.SemaphoreType.DMA((n, 3)), pltpu.SemaphoreType.DMA((n, 3))],
    )(*ss)
    return list(outs)


def _rows_tile(r, c, itemsize=4):
    return _tile(r, max(SUBLANES, (2**20 // itemsize) // c))


def _pair_add(g, r1, core, name):
    n0, _, r, c = g.shape
    tr = _rows_tile(r, c)
    g5 = g.reshape(n0, NCHIP, 2, r, c)

    def body(core_ref, g_ref, r_ref, o_ref):
        o_ref[...] = (g_ref[...].astype(F32) + r_ref[...].astype(F32)).astype(o_ref.dtype)

    return pl.pallas_call(
        body, name=name,
        grid_spec=pltpu.PrefetchScalarGridSpec(
            num_scalar_prefetch=1, grid=(n0, NCHIP, r // tr),
            in_specs=[pl.BlockSpec((None, None, None, tr, c), lambda l, p, i, cr: (l, p, cr[0], i, 0)),
                      pl.BlockSpec((None, None, tr, c), lambda l, p, i, cr: (l, p, i, 0))],
            out_specs=pl.BlockSpec((None, None, tr, c), lambda l, p, i, cr: (l, p, i, 0))),
        out_shape=_sds((n0, NCHIP, r, c), BF16),
        compiler_params=_cp(),
    )(core, g5, r1)


def _adamw_math(g, w, m, v):
    m = ADAM_B1 * m + (1.0 - ADAM_B1) * g
    v = ADAM_B2 * v + (1.0 - ADAM_B2) * (g * g)
    m_hat = m / (1.0 - ADAM_B1 ** ADAM_STEP)
    v_hat = v / (1.0 - ADAM_B2 ** ADAM_STEP)
    delta = -ADAM_LR * (m_hat / (jnp.sqrt(v_hat) + ADAM_EPS) + ADAM_WD * w)
    return delta, m, v


def _adamw_big(s, r2, chip, w, m, v, name):
    n0, r, c = w.shape
    tr = _rows_tile(r, c) // 2 or SUBLANES
    tr = _tile(r, tr)

    def body(chip_ref, s_ref, ra_ref, rb_ref, rc_ref, w_ref, m_ref, v_ref, g_out, d_out, m_out, v_out):
        g = ((s_ref[...].astype(F32) + ra_ref[...].astype(F32)) + rb_ref[...].astype(F32)) + rc_ref[...].astype(F32)
        d, mm, vv = _adamw_math(g, w_ref[...], m_ref[...], v_ref[...])
        g_out[...] = g
        d_out[...] = d
        m_out[...] = mm
        v_out[...] = vv

    wspec = pl.BlockSpec((None, tr, c), lambda l, i, cr: (l, i, 0))
    rspec = [pl.BlockSpec((None, None, tr, c), functools.partial(lambda l, i, cr, j: (l, j, i, 0), j=j))
             for j in range(3)]
    return pl.pallas_call(
        body, name=name,
        grid_spec=pltpu.PrefetchScalarGridSpec(
            num_scalar_prefetch=1, grid=(n0, r // tr),
            in_specs=[pl.BlockSpec((None, None, tr, c), lambda l, i, cr: (l, cr[0], i, 0)),
                      *rspec, wspec, wspec, wspec],
            out_specs=[wspec] * 4),
        out_shape=[_sds(w.shape, F32)] * 4,
        compiler_params=_cp(),
    )(chip, s, r2, r2, r2, w, m, v)


def _sum_devices(parts, name):
    _, r, c = parts.shape
    tr = _tile(r, 512)

    def body(p_ref, o_ref):
        acc = p_ref[0]
        for k in range(1, NDEV):
            acc = acc + p_ref[k]
        o_ref[...] = acc

    return pl.pallas_call(
        body, name=name, grid=(r // tr,),
        in_specs=[pl.BlockSpec((NDEV, tr, c), lambda i: (0, i, 0))],
        out_specs=pl.BlockSpec((tr, c), lambda i: (i, 0)),
        out_shape=_sds((r, c), F32), compiler_params=_cp(),
    )(parts)


def _adamw_flat(g, w, m, v, name):
    r, c = w.shape
    tr = _tile(r, 512)

    def body(g_ref, w_ref, m_ref, v_ref, d_out, m_out, v_out):
        d, mm, vv = _adamw_math(g_ref[...], w_ref[...], m_ref[...], v_ref[...])
        d_out[...] = d
        m_out[...] = mm
        v_out[...] = vv

    spec = pl.BlockSpec((tr, c), lambda i: (i, 0))
    return pl.pallas_call(
        body, name=name, grid=(r // tr,), in_specs=[spec] * 4, out_specs=[spec] * 3,
        out_shape=[_sds(w.shape, F32)] * 3, compiler_params=_cp(),
    )(g, w, m, v)


def _matmul(name, a, b, *, grid, a_spec, b_spec, o_spec, out_shape, dims, acc_shape,
            extra=(), extra_specs=(), a_pro=None, epi=None, into=None):
    nk = grid[2]
    n_extra = len(extra)

    def body(*refs):
        a_ref, b_ref = refs[0], refs[1]
        ex = refs[2:2 + n_extra]
        rest = refs[2 + n_extra + (into is not None):]
        o_ref = rest[0]

        def product():
            av = a_ref[...]
            if a_pro is not None:
                av = a_pro(av)
            return _dot(av, b_ref[...], dims)

        def finish(r):
            if epi is not None:
                r = epi(r, *[e[...] for e in ex])
            o_ref[...] = r.astype(o_ref.dtype)

        if nk == 1:
            finish(product())
        else:
            acc_ref = rest[1]
            k = pl.program_id(2)

            @pl.when(k == 0)
            def _():
                acc_ref[...] = jnp.zeros_like(acc_ref)

            acc_ref[...] += product()

            @pl.when(k == nk - 1)
            def _():
                finish(acc_ref[...])

    operands = [a, b, *extra]
    in_specs = [a_spec, b_spec, *extra_specs]
    aliases = {}
    if into is not None:
        operands.append(into)
        in_specs.append(ANY)
        aliases = {len(operands) - 1: 0}
        out_shape = _sds(into.shape, into.dtype)
    return pl.pallas_call(
        body, name=name, grid=grid, in_specs=in_specs, out_specs=o_spec, out_shape=out_shape,
        scratch_shapes=[] if nk == 1 else [pltpu.VMEM(acc_shape, F32)],
        input_output_aliases=aliases, compiler_params=_cp(),
    )(*operands)


def _rms_fwd(x, g, name):
    L, D = x.shape
    tm = _tile(L, 256)

    def body(x_ref, g_ref, o_ref):
        xf = x_ref[...]
        rstd = lax.rsqrt(jnp.mean(xf * xf, axis=-1, keepdims=True) + EPS)
        o_ref[...] = (xf * rstd * g_ref[...]).astype(o_ref.dtype)

    return pl.pallas_call(
        body, name=name, grid=(L // tm,),
        in_specs=[pl.BlockSpec((tm, D), lambda i: (i, 0)), pl.BlockSpec((1, D), lambda i: (0, 0))],
        out_specs=pl.BlockSpec((tm, D), lambda i: (i, 0)),
        out_shape=_sds((L, D), BF16), compiler_params=_cp(),
    )(x, g)


def _rms_bwd(dh, x, g, dres, name):
    L, D = x.shape
    tm = _tile(L, 256)

    def body(dh_ref, x_ref, g_ref, dr_ref, dx_ref, dxb_ref, dg_ref):
        @pl.when(pl.program_id(0) == 0)
        def _():
            dg_ref[...] = jnp.zeros_like(dg_ref)

        xf = x_ref[...]
        dhf = dh_ref[...].astype(F32)
        rstd = lax.rsqrt(jnp.mean(xf * xf, axis=-1, keepdims=True) + EPS)
        xhat = xf * rstd
        dg_ref[...] += jnp.sum(dhf * xhat, axis=0, keepdims=True)
        dxh = dhf * g_ref[...]
        dx = dr_ref[...] + rstd * (dxh - xhat * jnp.mean(dxh * xhat, axis=-1, keepdims=True))
        dx_ref[...] = dx
        dxb_ref[...] = dx.astype(BF16)

    row = pl.BlockSpec((tm, D), lambda i: (i, 0))
    vec = pl.BlockSpec((1, D), lambda i: (0, 0))
    return pl.pallas_call(
        body, name=name, grid=(L // tm,), in_specs=[row, row, vec, row], out_specs=[row, row, vec],
        out_shape=[_sds((L, D), F32), _sds((L, D), BF16), _sds((1, D), F32)], compiler_params=_cp(),
    )(dh, x, g, dres)


def _loss_head(x, g, target, name):
    L, D = x.shape
    tm = _tile(L, 256)

    def body(x_ref, g_ref, t_ref, loss_ref, dx_ref, dxb_ref, dg_ref):
        @pl.when(pl.program_id(0) == 0)
        def _():
            dg_ref[...] = jnp.zeros_like(dg_ref)
            loss_ref[...] = jnp.zeros_like(loss_ref)

        xf = x_ref[...]
        gv = g_ref[...]
        rstd = lax.rsqrt(jnp.mean(xf * xf, axis=-1, keepdims=True) + EPS)
        xhat = xf * rstd
        err = xhat * gv - t_ref[...]
        part = jnp.sum(jnp.sum(err * err, axis=-1, keepdims=True), axis=0, keepdims=True)
        loss_ref[...] += jnp.broadcast_to(part * (0.5 / D), loss_ref.shape)
        dy = err * (1.0 / D)
        dg_ref[...] += jnp.sum(dy * xhat, axis=0, keepdims=True)
        dxh = dy * gv
        dx = rstd * (dxh - xhat * jnp.mean(dxh * xhat, axis=-1, keepdims=True))
        dx_ref[...] = dx
        dxb_ref[...] = dx.astype(BF16)

    row = pl.BlockSpec((tm, D), lambda i: (i, 0))
    vec = pl.BlockSpec((1, D), lambda i: (0, 0))
    lspec = pl.BlockSpec((SUBLANES, LANES), lambda i: (0, 0))
    return pl.pallas_call(
        body, name=name, grid=(L // tm,), in_specs=[row, vec, row], out_specs=[lspec, row, row, vec],
        out_shape=[_sds((SUBLANES, LANES), F32), _sds((L, D), F32), _sds((L, D), BF16), _sds((1, D), F32)],
        compiler_params=_cp(),
    )(x, g, target)


_GELU_C = math.sqrt(2.0 / math.pi)
_GELU_K = 0.044715


def _gelu(x):
    return 0.5 * x * (1.0 + jnp.tanh(_GELU_C * (x + _GELU_K * (x * x * x))))


def _gelu_and_grad(x):
    t = jnp.tanh(_GELU_C * (x + _GELU_K * (x * x * x)))
    val = 0.5 * x * (1.0 + t)
    grad = 0.5 * (1.0 + t) + 0.5 * x * (1.0 - t * t) * (_GELU_C * (1.0 + 3.0 * _GELU_K * (x * x)))
    return val, grad


def _sigmoid(x):
    e = jnp.exp(-jnp.abs(x))
    return jnp.where(x >= 0, 1.0, e) / (1.0 + e)


def _tril_mask():
    r = lax.broadcasted_iota(jnp.int32, (CHUNK, CHUNK), 0)
    c = lax.broadcasted_iota(jnp.int32, (CHUNK, CHUNK), 1)
    return r >= c


def _gmlp_fwd(proj, norm_g, w_s, b_col, mix, name):
    L = proj.shape[0]
    groups = mix // CHUNK
    tt = _tile(L, 2 * CHUNK)

    def body(uv_ref, g_ref, w_ref, b_ref, o_ref):
        z = _gelu(uv_ref[...].astype(F32))
        u, v = z[:, :mix], z[:, mix:]
        vn = v * lax.rsqrt(jnp.mean(v * v, axis=-1, keepdims=True) + EPS) * g_ref[...]
        mask = _tril_mask()
        for gi in range(groups):
            wt = jnp.where(mask, w_ref[gi], 0.0).astype(BF16)
            cols = slice(gi * CHUNK, (gi + 1) * CHUNK)
            for cc in range(tt // CHUNK):
                rows = slice(cc * CHUNK, (cc + 1) * CHUNK)
                mixed = _dot(wt, vn[rows, cols].astype(BF16)) + b_ref[gi]
                o_ref[rows, cols] = (u[rows, cols] * mixed).astype(o_ref.dtype)

    return pl.pallas_call(
        body, name=name, grid=(L // tt,),
        in_specs=[pl.BlockSpec((tt, 2 * mix), lambda i: (i, 0)),
                  pl.BlockSpec((1, mix), lambda i: (0, 0)),
                  pl.BlockSpec((groups, CHUNK, CHUNK), lambda i: (0, 0, 0)),
                  pl.BlockSpec((groups, CHUNK, 1), lambda i: (0, 0, 0))],
        out_specs=pl.BlockSpec((tt, mix), lambda i: (i, 0)),
        out_shape=_sds((L, mix), BF16), compiler_params=_cp(),
    )(proj, norm_g, w_s, b_col)


def _gmlp_bwd(proj, norm_g, w_s, w_st, b_col, dy, mix, name):
    L = proj.shape[0]
    groups = mix // CHUNK
    tt = _tile(L, 2 * CHUNK)

    def body(uv_ref, g_ref, w_ref, wt_ref, b_ref, dy_ref, duv_ref, dg_ref, dw_ref, db_ref, du_s, dvn_s):
        @pl.when(pl.program_id(0) == 0)
        def _():
            dg_ref[...] = jnp.zeros_like(dg_ref)
            dw_ref[...] = jnp.zeros_like(dw_ref)
            db_ref[...] = jnp.zeros_like(db_ref)

        z, zgrad = _gelu_and_grad(uv_ref[...].astype(F32))
        u, v = z[:, :mix], z[:, mix:]
        rstd = lax.rsqrt(jnp.mean(v * v, axis=-1, keepdims=True) + EPS)
        vhat = v * rstd
        gv = g_ref[...]
        vn = vhat * gv
        dyf = dy_ref[...].astype(F32)
        mask = _tril_mask()
        r = lax.broadcasted_iota(jnp.int32, (CHUNK, CHUNK), 0)
        c = lax.broadcasted_iota(jnp.int32, (CHUNK, CHUNK), 1)
        for gi in range(groups):
            w_low = jnp.where(mask, w_ref[gi], 0.0).astype(BF16)
            w_up = jnp.where(r <= c, wt_ref[gi], 0.0).astype(BF16)
            cols = slice(gi * CHUNK, (gi + 1) * CHUNK)
            for cc in range(tt // CHUNK):
                rows = slice(cc * CHUNK, (cc + 1) * CHUNK)
                vnb = vn[rows, cols].astype(BF16)
                mixed = _dot(w_low, vnb) + b_ref[gi]
                dyb = dyf[rows, cols]
                dm = dyb * u[rows, cols]
                dmb = dm.astype(BF16)
                du_s[rows, cols] = dyb * mixed
                dvn_s[rows, cols] = _dot(w_up, dmb)
                dw_ref[gi] += jnp.where(mask, _dot(dmb, vnb, NT), 0.0)
                db_ref[gi] += jnp.sum(dm, axis=1, keepdims=True)
        dvn = dvn_s[...]
        dg_ref[...] += jnp.sum(dvn * vhat, axis=0, keepdims=True)
        dvh = dvn * gv
        dv = rstd * (dvh - vhat * jnp.mean(dvh * vhat, axis=-1, keepdims=True))
        duv_ref[:, :mix] = (du_s[...] * zgrad[:, :mix]).astype(duv_ref.dtype)
        duv_ref[:, mix:] = (dv * zgrad[:, mix:]).astype(duv_ref.dtype)

    wspec = pl.BlockSpec((groups, CHUNK, CHUNK), lambda i: (0, 0, 0))
    bspec = pl.BlockSpec((groups, CHUNK, 1), lambda i: (0, 0, 0))
    gspec = pl.BlockSpec((1, mix), lambda i: (0, 0))
    return pl.pallas_call(
        body, name=name, grid=(L // tt,),
        in_specs=[pl.BlockSpec((tt, 2 * mix), lambda i: (i, 0)), gspec, wspec, wspec, bspec,
                  pl.BlockSpec((tt, mix), lambda i: (i, 0))],
        out_specs=[pl.BlockSpec((tt, 2 * mix), lambda i: (i, 0)), gspec, wspec, bspec],
        out_shape=[_sds((L, 2 * mix), BF16), _sds((1, mix), F32),
                   _sds((groups, CHUNK, CHUNK), F32), _sds((groups, CHUNK, 1), F32)],
        scratch_shapes=[pltpu.VMEM((tt, mix), F32), pltpu.VMEM((tt, mix), F32)],
        compiler_params=_cp(),
    )(proj, norm_g, w_s, w_st, b_col, dy)


S5_CB = 512
S5_UB = 128
S5_LEVELS = (1, 2, 4)


def _s5_fwd_consts(ar, ai):
    rows = lax.broadcasted_iota(jnp.int32, ar.shape, 0)
    out = []
    for d in S5_LEVELS:
        m = rows >= d
        out.append((jnp.where(m, ar, 0.0), jnp.where(m, ai, 0.0)))
        ar, ai = ar * ar - ai * ai, 2.0 * ar * ai
    return out


def _s5_rev_consts(ar, ai):
    rows = lax.broadcasted_iota(jnp.int32, ar.shape, 0)
    out = []
    for d in S5_LEVELS:
        m = rows < SUBLANES - d
        out.append((jnp.where(m, ar, 0.0), jnp.where(m, ai, 0.0)))
        ar, ai = ar * ar - ai * ai, 2.0 * ar * ai
    return out


def _scan8(xr, xi, consts, reverse):
    for (cr, ci), d in zip(consts, S5_LEVELS):
        sh = SUBLANES - d if reverse else d
        pr = pltpu.roll(xr, sh, 0)
        pi = pltpu.roll(xi, sh, 0)
        xr, xi = xr + (cr * pr - ci * pi), xi + (cr * pi + ci * pr)
    return xr, xi


def _row_bcast(x, row):
    rows = lax.broadcasted_iota(jnp.int32, x.shape, 0)
    return jnp.broadcast_to(jnp.sum(jnp.where(rows == row, x, 0.0), axis=0, keepdims=True), x.shape)


def _s5_forward_block(u_ref, bre_ref, bim_ref, sr, si, ar, ai, pwr, pwi, carry, tt):
    consts = _s5_fwd_consts(ar, ai)
    u = u_ref[...]
    sr[...] = _dot(u, bre_ref[...])
    si[...] = _dot(u, bim_ref[...])

    def step(r, cs):
        cr, ci = cs
        o = pl.multiple_of(r * SUBLANES, SUBLANES)
        xr, xi = _scan8(sr[pl.ds(o, SUBLANES), :], si[pl.ds(o, SUBLANES), :], consts, False)
        xr, xi = xr + (pwr * cr - pwi * ci), xi + (pwr * ci + pwi * cr)
        sr[pl.ds(o, SUBLANES), :] = xr
        si[pl.ds(o, SUBLANES), :] = xi
        return _row_bcast(xr, SUBLANES - 1), _row_bcast(xi, SUBLANES - 1)

    return lax.fori_loop(0, tt // SUBLANES, step, carry)


def _s5_powers(ar, ai):
    rows = lax.broadcasted_iota(jnp.int32, ar.shape, 0)
    return _scan8(jnp.where(rows == 0, ar, 0.0), jnp.where(rows == 0, ai, 0.0), _s5_fwd_consts(ar, ai), False)


def _s5_scan_fwd(proj, a_re, a_im, b_re, b_im, c_re, c_im, mix, name):
    L = proj.shape[0]
    S = a_re.shape[1]
    nj = mix // S5_UB
    tt = _tile(L, 256)
    ni = L // tt
    ucol = 2 * mix // S5_UB

    def body(u_ref, ar_ref, ai_ref, bre_ref, bim_ref, cre_ref, cim_ref, y_ref, sbr_ref, sbi_ref,
             sr, si, car, cai):
        ar = jnp.broadcast_to(ar_ref[...], (SUBLANES, S5_CB))
        ai = jnp.broadcast_to(ai_ref[...], (SUBLANES, S5_CB))

        @pl.when(pl.program_id(1) == 0)
        def _():
            car[...] = jnp.zeros_like(car)
            cai[...] = jnp.zeros_like(cai)

        sbr_ref[...] = car[...]
        sbi_ref[...] = cai[...]
        pwr, pwi = _s5_powers(ar, ai)
        cr, ci = _s5_forward_block(u_ref, bre_ref, bim_ref, sr, si, ar, ai, pwr, pwi,
                                   (car[...], cai[...]), tt)
        car[...] = cr
        cai[...] = ci
        y_ref[...] = _dot(sr[...].astype(BF16), cre_ref[...]) - _dot(si[...].astype(BF16), cim_ref[...])

    avec = pl.BlockSpec((1, S5_CB), lambda j, i: (0, j))
    bspec = pl.BlockSpec((S5_UB, S5_CB), lambda j, i: (j, 0))
    cspec = pl.BlockSpec((S5_CB, S5_UB), lambda j, i: (j, 0))
    sb = pl.BlockSpec((SUBLANES, S5_CB), lambda j, i: (i, j))
    return pl.pallas_call(
        body, name=name, grid=(nj, ni),
        in_specs=[pl.BlockSpec((tt, S5_UB), lambda j, i: (i, ucol + j)), avec, avec, bspec, bspec, cspec, cspec],
        out_specs=[pl.BlockSpec((tt, S5_UB), lambda j, i: (i, j)), sb, sb],
        out_shape=[_sds((L, mix), F32), _sds((ni * SUBLANES, S), F32), _sds((ni * SUBLANES, S), F32)],
        scratch_shapes=[pltpu.VMEM((tt, S5_CB), F32), pltpu.VMEM((tt, S5_CB), F32),
                        pltpu.VMEM((SUBLANES, S5_CB), F32), pltpu.VMEM((SUBLANES, S5_CB), F32)],
        compiler_params=_cp(),
    )(proj, a_re, a_im, b_re, b_im, c_re, c_im)


def _s5_scan_bwd(proj, a_re, a_im, b_re, b_im, c_re, c_im, sb_re, sb_im, dy, dxin, mix, name):
    L = proj.shape[0]
    S = a_re.shape[1]
    nj = mix // S5_UB
    tt = _tile(L, 256)
    ni = L // tt
    ucol = 2 * mix // S5_UB
    nb = tt // SUBLANES

    def body(u_ref, ar_ref, ai_ref, bre_ref, bim_ref, cre_ref, cim_ref, sbr_ref, sbi_ref, dy_ref, dx_ref,
             du_ref, dbr_ref, dbi_ref, dcr_ref, dci_ref, dar_ref, dai_ref,
             sr, si, gr, gi, car, cai, acr, aci):
        first = pl.program_id(1) == 0
        ar = jnp.broadcast_to(ar_ref[...], (SUBLANES, S5_CB))
        ai = jnp.broadcast_to(ai_ref[...], (SUBLANES, S5_CB))

        @pl.when(first)
        def _():
            for ref in (car, cai, acr, aci, dbr_ref, dbi_ref, dcr_ref, dci_ref):
                ref[...] = jnp.zeros_like(ref)

        pwr, pwi = _s5_powers(ar, ai)
        _s5_forward_block(u_ref, bre_ref, bim_ref, sr, si, ar, ai, pwr, pwi,
                          (sbr_ref[...], sbi_ref[...]), tt)

        dyb = dy_ref[...].astype(BF16)
        gr[...] = _dot(dyb, cre_ref[...], NT)
        gi[...] = -_dot(dyb, cim_ref[...], NT)

        nai = -ai
        consts = _s5_rev_consts(ar, nai)
        rows = lax.broadcasted_iota(jnp.int32, ar.shape, 0)
        last = rows == SUBLANES - 1
        qr, qi = _scan8(jnp.where(last, ar, 0.0), jnp.where(last, nai, 0.0), consts, True)

        def step(k, cs):
            cr, ci, dr, di = cs
            o = pl.multiple_of((nb - 1 - k) * SUBLANES, SUBLANES)
            xr, xi = _scan8(gr[pl.ds(o, SUBLANES), :], gi[pl.ds(o, SUBLANES), :], consts, True)
            xr, xi = xr + (qr * cr - qi * ci), xi + (qr * ci + qi * cr)
            gr[pl.ds(o, SUBLANES), :] = xr
            gi[pl.ds(o, SUBLANES), :] = xi
            hr = jnp.where(last, cr, pltpu.roll(xr, SUBLANES - 1, 0))
            hi = jnp.where(last, ci, pltpu.roll(xi, SUBLANES - 1, 0))
            s_r = sr[pl.ds(o, SUBLANES), :]
            s_i = si[pl.ds(o, SUBLANES), :]
            dr = dr + (hr * s_r + hi * s_i)
            di = di + (hi * s_r - hr * s_i)
            return _row_bcast(xr, 0), _row_bcast(xi, 0), dr, di

        cr, ci, dr, di = lax.fori_loop(0, nb, step, (car[...], cai[...], acr[...], aci[...]))
        car[...] = cr
        cai[...] = ci
        acr[...] = dr
        aci[...] = di
        dar_ref[...] = jnp.sum(dr, axis=0, keepdims=True)
        dai_ref[...] = jnp.sum(di, axis=0, keepdims=True)

        u = u_ref[...]
        gbr = gr[...].astype(BF16)
        gbi = gi[...].astype(BF16)
        dbr_ref[...] += _dot(u, gbr, TN)
        dbi_ref[...] += _dot(u, gbi, TN)
        du = _dot(gbr, bre_ref[...], NT) + _dot(gbi, bim_ref[...], NT)
        du_ref[...] = (du + dx_ref[...].astype(F32)).astype(du_ref.dtype)
        dyf = dy_ref[...].astype(BF16)
        dcr_ref[...] += _dot(sr[...].astype(BF16), dyf, TN)
        dci_ref[...] -= _dot(si[...].astype(BF16), dyf, TN)

    rev = lambda i: ni - 1 - i
    avec = pl.BlockSpec((1, S5_CB), lambda j, i: (0, j))
    bspec = pl.BlockSpec((S5_UB, S5_CB), lambda j, i: (j, 0))
    cspec = pl.BlockSpec((S5_CB, S5_UB), lambda j, i: (j, 0))
    sb = pl.BlockSpec((SUBLANES, S5_CB), lambda j, i: (rev(i), j))
    tile = pl.BlockSpec((tt, S5_UB), lambda j, i: (rev(i), j))
    return pl.pallas_call(
        body, name=name, grid=(nj, ni),
        in_specs=[pl.BlockSpec((tt, S5_UB), lambda j, i: (rev(i), ucol + j)), avec, avec, bspec, bspec,
                  cspec, cspec, sb, sb, tile, tile],
        out_specs=[tile, bspec, bspec, cspec, cspec, avec, avec],
        out_shape=[_sds((L, mix), BF16), _sds((mix, S5_CB), F32), _sds((mix, S5_CB), F32),
                   _sds((S, S5_UB), F32), _sds((S, S5_UB), F32), _sds((1, S), F32), _sds((1, S), F32)],
        scratch_shapes=[pltpu.VMEM((tt, S5_CB), F32)] * 4 + [pltpu.VMEM((SUBLANES, S5_CB), F32)] * 4,
        compiler_params=_cp(),
    )(proj, a_re, a_im, b_re, b_im, c_re, c_im, sb_re, sb_im, dy, dxin)


def _s5_glu_fwd(ypre, proj, d, w_glu, b_glu, layer, mix, name):
    L = ypre.shape[0]
    tm = _tile(L, 512)

    def body(y_ref, x_ref, d_ref, w_ref, b_ref, o_ref):
        g = _gelu(y_ref[...] + d_ref[...] * x_ref[...].astype(F32))
        z = _dot(g.astype(BF16), w_ref[...]) + b_ref[...]
        o_ref[...] = (g * _sigmoid(z)).astype(o_ref.dtype)

    row = pl.BlockSpec((tm, mix), lambda i: (i, 0))
    vec = pl.BlockSpec((1, mix), lambda i: (0, 0))
    return pl.pallas_call(
        body, name=name, grid=(L // tm,),
        in_specs=[row, pl.BlockSpec((tm, mix), lambda i: (i, 2)), vec,
                  pl.BlockSpec((None, mix, mix), lambda i: (layer, 0, 0)), vec],
        out_specs=row, out_shape=_sds((L, mix), BF16), compiler_params=_cp(),
    )(ypre, proj, d, w_glu, b_glu)


def _s5_glu_bwd(ypre, proj, d, w_glu, b_glu, dout, layer, mix, name):
    L = ypre.shape[0]
    tm = _tile(L, 512)

    def body(y_ref, x_ref, d_ref, w_ref, b_ref, do_ref, dy_ref, dx_ref, dw_ref, db_ref, dd_ref):
        @pl.when(pl.program_id(0) == 0)
        def _():
            dw_ref[...] = jnp.zeros_like(dw_ref)
            db_ref[...] = jnp.zeros_like(db_ref)
            dd_ref[...] = jnp.zeros_like(dd_ref)

        xin = x_ref[...].astype(F32)
        dv = d_ref[...]
        g, ggrad = _gelu_and_grad(y_ref[...] + dv * xin)
        gb = g.astype(BF16)
        w = w_ref[...]
        sg = _sigmoid(_dot(gb, w) + b_ref[...])
        do = do_ref[...].astype(F32)
        dz = do * g * sg * (1.0 - sg)
        dzb = dz.astype(BF16)
        dg = do * sg + _dot(dzb, w, NT)
        dw_ref[...] += _dot(gb, dzb, TN)
        db_ref[...] += jnp.sum(dz, axis=0, keepdims=True)
        dyv = dg * ggrad
        dd_ref[...] += jnp.sum(dyv * xin, axis=0, keepdims=True)
        dy_ref[...] = dyv.astype(dy_ref.dtype)
        dx_ref[...] = (dyv * dv).astype(dx_ref.dtype)

    row = pl.BlockSpec((tm, mix), lambda i: (i, 0))
    vec = pl.BlockSpec((1, mix), lambda i: (0, 0))
    mat = pl.BlockSpec((mix, mix), lambda i: (0, 0))
    return pl.pallas_call(
        body, name=name, grid=(L // tm,),
        in_specs=[row, pl.BlockSpec((tm, mix), lambda i: (i, 2)), vec,
                  pl.BlockSpec((None, mix, mix), lambda i: (layer, 0, 0)), vec, row],
        out_specs=[row, row, mat, vec, vec],
        out_shape=[_sds((L, mix), BF16), _sds((L, mix), BF16), _sds((mix, mix), F32),
                   _sds((1, mix), F32), _sds((1, mix), F32)],
        compiler_params=_cp(),
    )(ypre, proj, d, w_glu, b_glu, dout)


def _after_matrix(t):
    j = lax.broadcasted_iota(jnp.int32, (t, t), 0)
    s = lax.broadcasted_iota(jnp.int32, (t, t), 1)
    return jnp.where(j > s, 1.0, 0.0).astype(BF16)


def _suffix_sum(x, m_after):
    hi = x.astype(BF16)
    lo = (x - hi.astype(F32)).astype(BF16)
    return _dot(hi, m_after) + _dot(lo, m_after)


def _sb_scores(q, k, row0, col0, scale):
    z = _dot(q, k, NT) * scale
    t, s = z.shape
    rows = lax.broadcasted_iota(jnp.int32, (t, s), 0) + row0
    cols = lax.broadcasted_iota(jnp.int32, (t, s), 1) + col0
    mask = cols < rows
    e = jnp.exp(-jnp.abs(z))
    sp = jnp.maximum(z, 0.0) + jnp.log(1.0 + e)
    return z, sp, mask, e


def _sb_fwd(proj, m_after, mix, name):
    L = proj.shape[0]
    heads = mix // HEAD_DIM
    T = m_after.shape[0]
    nt = L // T
    qc, kc, vc = (3 * mix // HEAD_DIM, 4 * mix // HEAD_DIM, 5 * mix // HEAD_DIM)
    scale = HEAD_DIM ** -0.5

    def body(q_ref, k_ref, v_ref, m_ref, o_ref, o32_ref, acc, ra):
        qi, jj = pl.program_id(1), pl.program_id(2)

        @pl.when(jj == 0)
        def _():
            acc[...] = jnp.zeros_like(acc)
            ra[...] = jnp.zeros_like(ra)

        @pl.when(jj <= qi)
        def _():
            z, sp, mask, _ = _sb_scores(q_ref[...], k_ref[...], qi * T, (qi - jj) * T, scale)
            lg = jnp.where(mask, -sp, 0.0)
            after = _suffix_sum(lg, m_ref[...]) + ra[...]
            w = jnp.where(mask, jnp.exp(z - sp + after), 0.0)
            acc[...] += _dot(w.astype(BF16), v_ref[...])
            ra[...] += jnp.sum(lg, axis=1, keepdims=True)

        @pl.when(jj == qi)
        def _():
            o_ref[...] = acc[...].astype(o_ref.dtype)
            o32_ref[...] = acc[...]

    kv = lambda col: pl.BlockSpec((T, HEAD_DIM), lambda h, i, j: (jnp.maximum(i - j, 0), col + h))
    return pl.pallas_call(
        body, name=name, grid=(heads, nt, nt),
        in_specs=[pl.BlockSpec((T, HEAD_DIM), lambda h, i, j: (i, qc + h)), kv(kc), kv(vc),
                  pl.BlockSpec((T, T), lambda h, i, j: (0, 0))],
        out_specs=[pl.BlockSpec((T, HEAD_DIM), lambda h, i, j: (i, h))] * 2,
        out_shape=[_sds((L, mix), BF16), _sds((L, mix), F32)],
        scratch_shapes=[pltpu.VMEM((T, HEAD_DIM), F32), pltpu.VMEM((T, 1), F32)],
        compiler_params=_cp(),
    )(proj, proj, proj, m_after)


def _sb_bwd(proj, m_after, out, dout, mix, name):
    L = proj.shape[0]
    heads = mix // HEAD_DIM
    T = m_after.shape[0]
    nt = L // T
    qc, kc, vc = (3 * mix // HEAD_DIM, 4 * mix // HEAD_DIM, 5 * mix // HEAD_DIM)
    scale = HEAD_DIM ** -0.5

    def body(q_ref, k_ref, v_ref, m_ref, o_ref, do_ref, dq_ref, dk_ref, dv_ref, dq_acc, ra, rp, delta):
        qi, jj = pl.program_id(1), pl.program_id(2)

        @pl.when((qi == 0) & (jj == 0))
        def _():
            dk_ref[...] = jnp.zeros_like(dk_ref)
            dv_ref[...] = jnp.zeros_like(dv_ref)

        @pl.when(jj == 0)
        def _():
            dq_acc[...] = jnp.zeros_like(dq_acc)
            ra[...] = jnp.zeros_like(ra)
            rp[...] = jnp.zeros_like(rp)
            delta[...] = jnp.sum(do_ref[...].astype(F32) * o_ref[...], axis=1, keepdims=True)

        @pl.when(jj <= qi)
        def _():
            q, k, v, do = q_ref[...], k_ref[...], v_ref[...], do_ref[...]
            z, sp, mask, e = _sb_scores(q, k, qi * T, (qi - jj) * T, scale)
            lg = jnp.where(mask, -sp, 0.0)
            after = _suffix_sum(lg, m_ref[...]) + ra[...]
            wb = jnp.where(mask, jnp.exp(z - sp + after), 0.0).astype(BF16)
            p = wb.astype(F32) * _dot(do, v, NT)
            before = delta[...] - (_suffix_sum(p, m_ref[...]) + rp[...]) - p
            beta = jnp.where(z >= 0, 1.0, e) / (1.0 + e)
            dz = jnp.where(mask, p * (1.0 - beta) - beta * before, 0.0) * scale
            dzb = dz.astype(BF16)
            dq_acc[...] += _dot(dzb, k)
            rows = pl.ds(pl.multiple_of((qi - jj) * T, T), T)
            dk_ref[rows, :] += _dot(dzb, q, TN)
            dv_ref[rows, :] += _dot(wb, do, TN)
            ra[...] += jnp.sum(lg, axis=1, keepdims=True)
            rp[...] += jnp.sum(p, axis=1, keepdims=True)

        @pl.when(jj == qi)
        def _():
            dq_ref[...] = dq_acc[...]

    kv = lambda col: pl.BlockSpec((T, HEAD_DIM), lambda h, i, j: (jnp.maximum(i - j, 0), col + h))
    qo = pl.BlockSpec((T, HEAD_DIM), lambda h, i, j: (i, h))
    whole = pl.BlockSpec((L, HEAD_DIM), lambda h, i, j: (0, h))
    return pl.pallas_call(
        body, name=name, grid=(heads, nt, nt),
        in_specs=[pl.BlockSpec((T, HEAD_DIM), lambda h, i, j: (i, qc + h)), kv(kc), kv(vc),
                  pl.BlockSpec((T, T), lambda h, i, j: (0, 0)), qo, qo],
        out_specs=[qo, whole, whole],
        out_shape=[_sds((L, mix), F32)] * 3,
        scratch_shapes=[pltpu.VMEM((T, HEAD_DIM), F32), pltpu.VMEM((T, 1), F32), pltpu.VMEM((T, 1), F32),
                        pltpu.VMEM((T, 1), F32)],
        compiler_params=_cp(),
    )(proj, proj, proj, m_after, out, dout)


def _merge_fwd(ys, w_branch, proj, b_gate, layer, name):
    L, mix = ys[0].shape
    bw = w_branch.shape[-1]
    D = bw * NDEV
    tm = _tile(L, 512)
    gc = 6 * mix // bw

    def body(ya, yb, yc, w_ref, pa, pb, pc, b_ref, o_ref):
        acc = None
        for n, (y_ref, p_ref) in enumerate(((ya, pa), (yb, pb), (yc, pc))):
            gate = _sigmoid(p_ref[...].astype(F32) + b_ref[n:n + 1, :])
            term = gate * _dot(y_ref[...], w_ref[n])
            acc = term if acc is None else acc + term
        o_ref[...] = acc.astype(o_ref.dtype)

    yspec = pl.BlockSpec((tm, mix), lambda i, j: (i, 0))
    pspec = [pl.BlockSpec((tm, bw), functools.partial(lambda i, j, n: (i, gc + n * NDEV + j), n=n)) for n in range(3)]
    return pl.pallas_call(
        body, name=name, grid=(L // tm, NDEV),
        in_specs=[yspec, yspec, yspec,
                  pl.BlockSpec((None, None, 3, mix, bw), lambda i, j: (layer, j, 0, 0, 0)),
                  *pspec, pl.BlockSpec((None, None, 3, bw), lambda i, j: (layer, j, 0, 0))],
        out_specs=pl.BlockSpec((tm, bw), lambda i, j: (i, j)),
        out_shape=_sds((L, D), BF16), compiler_params=_cp(),
    )(*ys, w_branch, proj, proj, proj, b_gate)


def _merge_bwd(ys, w_branch, proj, b_gate, dmerged, layer, name):
    L, mix = ys[0].shape
    bw = w_branch.shape[-1]
    D = bw * NDEV
    tm = _tile(L, 512)
    gc = 6 * mix // bw

    def body(ya, yb, yc, w_ref, pa, pb, pc, b_ref, dm_ref,
             dpa, dpb, dpc, dba, dbb, dbc, dya, dyb, dyc, dbg_ref, acc):
        i, j = pl.program_id(0), pl.program_id(1)

        @pl.when((i == 0) & (j == 0))
        def _():
            dbg_ref[...] = jnp.zeros_like(dbg_ref)

        @pl.when(j == 0)
        def _():
            acc[...] = jnp.zeros_like(acc)

        dm = dm_ref[...].astype(F32)
        for n, (y_ref, p_ref, dp_ref, db_ref) in enumerate(((ya, pa, dpa, dba), (yb, pb, dpb, dbb), (yc, pc, dpc, dbc))):
            gate = _sigmoid(p_ref[...].astype(F32) + b_ref[n:n + 1, :])
            br = _dot(y_ref[...], w_ref[n])
            dp = dm * br * gate * (1.0 - gate)
            dp_ref[...] = dp.astype(dp_ref.dtype)
            dbg_ref[j, n:n + 1, :] += jnp.sum(dp, axis=0, keepdims=True)
            dbr = (dm * gate).astype(BF16)
            db_ref[...] = dbr
            acc[n] += _dot(dbr, w_ref[n], NT)

        @pl.when(j == NDEV - 1)
        def _():
            for n, dy_ref in enumerate((dya, dyb, dyc)):
                dy_ref[...] = acc[n].astype(dy_ref.dtype)

    yspec = pl.BlockSpec((tm, mix), lambda i, j: (i, 0))
    ospec = pl.BlockSpec((tm, bw), lambda i, j: (i, j))
    pspec = [pl.BlockSpec((tm, bw), functools.partial(lambda i, j, n: (i, gc + n * NDEV + j), n=n)) for n in range(3)]
    return pl.pallas_call(
        body, name=name, grid=(L // tm, NDEV),
        in_specs=[yspec, yspec, yspec,
                  pl.BlockSpec((None, None, 3, mix, bw), lambda i, j: (layer, j, 0, 0, 0)),
                  *pspec, pl.BlockSpec((None, None, 3, bw), lambda i, j: (layer, j, 0, 0)), ospec],
        out_specs=[ospec] * 6 + [yspec] * 3 + [pl.BlockSpec((NDEV, 3, bw), lambda i, j: (0, 0, 0))],
        out_shape=[_sds((L, D), BF16)] * 6 + [_sds((L, mix), BF16)] * 3 + [_sds((NDEV, 3, bw), F32)],
        scratch_shapes=[pltpu.VMEM((3, tm, mix), F32)],
        compiler_params=_cp(),
    )(*ys, w_branch, proj, proj, proj, b_gate, dmerged)


def _ktile(n):
    for t in (1024, 768, 512, 384, 256, 128, 64, 32, 16, 8):
        if n % t == 0:
            return t
    return n


def _mm_cols(name, a, wg, layer, out_dtype, epi=None):
    M, K = a.shape
    nb = wg.shape[3]
    tm, tn = _tile(M, 512), _tile(nb, 512)
    r = nb // tn
    return _matmul(
        name, a, wg, grid=(M // tm, NDEV * r, 1),
        a_spec=pl.BlockSpec((tm, K), lambda i, j, k: (i, 0)),
        b_spec=pl.BlockSpec((None, None, K, tn), lambda i, j, k: (layer, j // r, 0, j % r)),
        o_spec=pl.BlockSpec((tm, tn), lambda i, j, k: (i, j)),
        out_shape=_sds((M, NDEV * nb), out_dtype), dims=NN, acc_shape=(tm, tn), epi=epi)


def _mm_cols_t(name, a, wg, layer, out_dtype):
    M = a.shape[0]
    K, nb = wg.shape[2], wg.shape[3]
    tm, tn, tk = _tile(M, 1024), _tile(K, 1024), _ktile(nb)
    r = nb // tk
    return _matmul(
        name, a, wg, grid=(M // tm, K // tn, NDEV * r),
        a_spec=pl.BlockSpec((tm, tk), lambda i, j, k: (i, k)),
        b_spec=pl.BlockSpec((None, None, tn, tk), lambda i, j, k: (layer, k // r, j, k % r)),
        o_spec=pl.BlockSpec((tm, tn), lambda i, j, k: (i, j)),
        out_shape=_sds((M, K), out_dtype), dims=NT, acc_shape=(tm, tn))


def _mm_rows(name, a, wn, layer, out_dtype, res, a_pro=None):
    M, K = a.shape
    N = wn.shape[2]
    tm, tn, tk = _tile(M, 512), _tile(N, 512), _tile(K, 2048)
    tile = pl.BlockSpec((tm, tn), lambda i, j, k: (i, j))
    return _matmul(
        name, a, wn, grid=(M // tm, N // tn, K // tk),
        a_spec=pl.BlockSpec((tm, tk), lambda i, j, k: (i, k)),
        b_spec=pl.BlockSpec((None, tk, tn), lambda i, j, k: (layer, k, j)),
        o_spec=tile, out_shape=_sds((M, N), out_dtype), dims=NN, acc_shape=(tm, tn),
        extra=(res,), extra_specs=(tile,), epi=lambda acc, rv: acc + rv, a_pro=a_pro)


def _mm_rows_t(name, a, wn, layer, out_dtype, extra=(), epi=None):
    M, N = a.shape
    K = wn.shape[1]
    tm, tn = _tile(M, 512), _tile(K, 512)
    tile = pl.BlockSpec((tm, tn), lambda i, j, k: (i, j))
    return _matmul(
        name, a, wn, grid=(M // tm, K // tn, 1),
        a_spec=pl.BlockSpec((tm, N), lambda i, j, k: (i, 0)),
        b_spec=pl.BlockSpec((None, tn, N), lambda i, j, k: (layer, j, 0)),
        o_spec=tile, out_shape=_sds((M, K), out_dtype), dims=NT, acc_shape=(tm, tn),
        extra=extra, extra_specs=(tile,) * len(extra), epi=epi)


def _mm_grad(name, a, dy, into, o_block, o_map, tn, a_pro=None):
    L, K = a.shape
    N = dy.shape[1]
    tm, tt = _tile(K, 512), _tile(L, 1024)
    return _matmul(
        name, a, dy, grid=(K // tm, N // tn, L // tt),
        a_spec=pl.BlockSpec((tt, tm), lambda i, j, k: (k, i)),
        b_spec=pl.BlockSpec((tt, tn), lambda i, j, k: (k, j)),
        o_spec=pl.BlockSpec(o_block(tm, tn), o_map), out_shape=None, dims=TN, acc_shape=(tm, tn),
        a_pro=a_pro, into=into)


def _grad_cols(name, a, dy, into, layer):
    nb = into.shape[3]
    tn = _tile(nb, 512)
    r = nb // tn
    return _mm_grad(name, a, dy, into, lambda tm, t: (None, None, tm, t),
                    lambda i, j, k: (layer, j // r, i, j % r), tn)


def _grad_rows(name, a, dy, into, layer, a_pro=None):
    tn = _tile(dy.shape[1], 512)
    return _mm_grad(name, a, dy, into, lambda tm, t: (None, tm, t), lambda i, j, k: (layer, i, j), tn, a_pro)


def _grad_branch(name, y, dbr, into, layer, n):
    bw = into.shape[4]
    return _mm_grad(name, y, dbr, into, lambda tm, t: (None, None, None, tm, t),
                    lambda i, j, k: (layer, j, n, i, 0), bw)


def _s5_discretize(lam_re, lam_im, log_dt, b_re, b_im):
    dt = jnp.exp(log_dt)[:, None]
    mag = jnp.exp(lam_re * dt)
    ab_re = mag * jnp.cos(lam_im * dt)
    ab_im = mag * jnp.sin(lam_im * dt)
    den = lam_re * lam_re + lam_im * lam_im
    n_re = ab_re - 1.0
    n_im = ab_im
    k_re = (n_re * lam_re + n_im * lam_im) / den
    k_im = (n_im * lam_re - n_re * lam_im) / den
    bb_re = k_re[..., None] * b_re - k_im[..., None] * b_im
    bb_im = k_re[..., None] * b_im + k_im[..., None] * b_re
    return ab_re, ab_im, bb_re, bb_im


def _s5_b_blocks(bb):
    g, p, h = bb.shape
    t = bb.reshape(g // 8, 8, p, h)
    return jnp.einsum('jiph,ik->jihkp', t, jnp.eye(8, dtype=bb.dtype)).reshape(g * h, 8 * p)


def _s5_b_unblock(m, g, p, h):
    t = m.reshape(g // 8, 8, h, 8, p)
    return jnp.einsum('jihkp,ik->jiph', t, jnp.eye(8, dtype=m.dtype)).reshape(g, p, h)


def _s5_c_blocks(c):
    g, h, p = c.shape
    t = c.reshape(g // 8, 8, h, p)
    return jnp.einsum('jihp,ik->jipkh', t, jnp.eye(8, dtype=c.dtype)).reshape(g * p, 8 * h)


def _s5_c_unblock(m, g, h, p):
    t = m.reshape(g // 8, 8, p, 8, h)
    return jnp.einsum('jipkh,ik->jihp', t, jnp.eye(8, dtype=m.dtype)).reshape(g, h, p)


BIG = ("w_in", "w_branch", "w_out", "w_mlp_in", "w_mlp_out", "s5_w_glu")
SMALL = ("norm1_g", "gm_norm_g", "gm_w_s", "gm_b_s", "s5_lambda_re", "s5_lambda_im", "s5_log_dt",
         "s5_b_re", "s5_b_im", "s5_c_re", "s5_c_im", "s5_d", "s5_b_glu", "norm2_g", "final_g", "b_gate")
WEIGHTS = ("norm1_g", "w_in", "b_gate", "gm_norm_g", "gm_w_s", "gm_b_s", "s5_lambda_re", "s5_lambda_im",
           "s5_log_dt", "s5_b_re", "s5_b_im", "s5_c_re", "s5_c_im", "s5_d", "s5_w_glu", "s5_b_glu",
           "w_branch", "w_out", "norm2_g", "w_mlp_in", "w_mlp_out", "final_g")
FLAT_ROWS = 512


def _pack(arrays):
    flat = jnp.concatenate([a.reshape(-1) for a in arrays])
    unit = FLAT_ROWS * LANES
    pad = (-flat.shape[0]) % unit
    return jnp.pad(flat, (0, pad)).reshape(-1, LANES)


def _unpack(flat2d, shapes):
    flat = flat2d.reshape(-1)
    out, off = [], 0
    for s in shapes:
        n = math.prod(s)
        out.append(flat[off:off + n].reshape(s))
        off += n
    return out


def kernel(x, norm1_g, w_in, b_gate, gm_norm_g, gm_w_s, gm_b_s, s5_lambda_re, s5_lambda_im, s5_log_dt, s5_b_re, s5_b_im, s5_c_re, s5_c_im, s5_d, s5_w_glu, s5_b_glu, w_branch, w_out, norm2_g, w_mlp_in, w_mlp_out, final_g, loss_target, m_norm1_g, m_w_in, m_b_gate, m_gm_norm_g, m_gm_w_s, m_gm_b_s, m_s5_lambda_re, m_s5_lambda_im, m_s5_log_dt, m_s5_b_re, m_s5_b_im, m_s5_c_re, m_s5_c_im, m_s5_d, m_s5_w_glu, m_s5_b_glu, m_w_branch, m_w_out, m_norm2_g, m_w_mlp_in, m_w_mlp_out, m_final_g, v_norm1_g, v_w_in, v_b_gate, v_gm_norm_g, v_gm_w_s, v_gm_b_s, v_s5_lambda_re, v_s5_lambda_im, v_s5_log_dt, v_s5_b_re, v_s5_b_im, v_s5_c_re, v_s5_c_im, v_s5_d, v_s5_w_glu, v_s5_b_glu, v_w_branch, v_w_out, v_norm2_g, v_w_mlp_in, v_w_mlp_out, v_final_g):
    P = dict(locals())
    W = {n: P[n] for n in WEIGHTS}
    M1 = {n: P["m_" + n] for n in WEIGHTS}
    V2 = {n: P["v_" + n] for n in WEIGHTS}

    L, D = x.shape[1], x.shape[2]
    depth = norm1_g.shape[0]
    mix = D // 2
    nb_in, ffb, bw = w_in.shape[2], w_mlp_in.shape[2], w_branch.shape[3]
    ff = ffb * NDEV
    s5_groups = mix // S5_GROUP_CH
    assert mix % S5_UB == 0 and L % CHUNK == 0 and w_in.shape[2] * NDEV == 6 * mix + 3 * D

    xi, yi, ci = _mesh_pos()
    core = ci.astype(jnp.int32).reshape(1)
    chip = (2 * xi + yi).astype(jnp.int32).reshape(1)
    dev = 4 * xi + 2 * yi + ci

    wg_in, wg_br, wg_out, wg_mi, wg_mo, wg_glu, bg = _all_gather(
        [w_in.astype(BF16), w_branch.astype(BF16), w_out.astype(BF16), w_mlp_in.astype(BF16),
         w_mlp_out.astype(BF16), s5_w_glu.astype(BF16), b_gate], "gather_weights")
    wn_out = wg_out.reshape(depth, D, D)
    wn_mo = wg_mo.reshape(depth, ff, D)
    wn_glu = wg_glu.reshape(depth, mix, mix)

    T = _tile(L, 512)
    m_after = _after_matrix(T)

    xcur = x[0]
    saved = []
    for l in range(depth):
        h = _rms_fwd(xcur, norm1_g[l][None], f"rms1_l{l}")
        proj = _mm_cols(f"proj_l{l}", h, wg_in, l, BF16)
        ya = _gmlp_fwd(proj, gm_norm_g[l][None], gm_w_s[l], gm_b_s[l][..., None], mix, f"gmlp_l{l}")
        ab_re, ab_im, bb_re, bb_im = _s5_discretize(s5_lambda_re[l], s5_lambda_im[l], s5_log_dt[l],
                                                    s5_b_re[l], s5_b_im[l])
        s5p = (ab_re.reshape(1, -1), ab_im.reshape(1, -1),
               _s5_b_blocks(bb_re).astype(BF16), _s5_b_blocks(bb_im).astype(BF16),
               _s5_c_blocks(s5_c_re[l]).astype(BF16), _s5_c_blocks(s5_c_im[l]).astype(BF16))
        ypre, sb_re, sb_im = _s5_scan_fwd(proj, *s5p, mix, f"s5scan_l{l}")
        yb = _s5_glu_fwd(ypre, proj, s5_d[l][None], wn_glu, s5_b_glu[l][None], l, mix, f"s5glu_l{l}")
        yc, yc32 = _sb_fwd(proj, m_after, mix, f"sb_l{l}")
        merged = _merge_fwd((ya, yb, yc), wg_br, proj, bg, l, f"merge_l{l}")
        xmid = _mm_rows(f"wout_l{l}", merged, wn_out, l, F32, xcur)
        h2 = _rms_fwd(xmid, norm2_g[l][None], f"rms2_l{l}")
        r = _mm_cols(f"mlpin_l{l}", h2, wg_mi, l, BF16, epi=lambda acc: jnp.maximum(acc, 0.0))
        xout = _mm_rows(f"mlpout_l{l}", r, wn_mo, l, F32, xmid, a_pro=lambda t: t * t)
        saved.append(dict(x=xcur, h=h, proj=proj, ys=(ya, yb, yc), s5p=s5p, ypre=ypre, sb=(sb_re, sb_im), yc32=yc32,
                          merged=merged, xmid=xmid, h2=h2, r=r))
        xcur = xout

    loss_tile, dx, dxb, d_final_g = _loss_head(xcur, final_g[None], loss_target[0], "loss_head")
    loss = lax.psum(loss_tile[0, 0], ("x", "y", "c"))

    g_in = lax.empty((depth, NDEV, D, nb_in), BF16)
    g_br = lax.empty((depth, NDEV, 3, mix, bw), BF16)
    g_out = lax.empty((depth, D, D), BF16)
    g_mi = lax.empty((depth, NDEV, D, ffb), BF16)
    g_mo = lax.empty((depth, ff, D), BF16)
    g_glu = [None] * depth
    small = {n: [None] * depth for n in SMALL if n != "final_g"}

    for l in reversed(range(depth)):
        sv = saved[l]
        d_a = _mm_rows_t(f"d_act_l{l}", dxb, wn_mo, l, BF16, extra=(sv["r"],),
                         epi=lambda acc, rv: acc * (2.0 * rv.astype(F32)))
        g_mo = _grad_rows(f"g_mlpout_l{l}", sv["r"], dxb, g_mo, l, a_pro=lambda t: t * t)
        g_mi = _grad_cols(f"g_mlpin_l{l}", sv["h2"], d_a, g_mi, l)
        dh2 = _mm_cols_t(f"d_h2_l{l}", d_a, wg_mi, l, F32)
        dxm, dxmb, small["norm2_g"][l] = _rms_bwd(dh2, sv["xmid"], norm2_g[l][None], dx, f"rms2_bwd_l{l}")
        d_merged = _mm_rows_t(f"d_merged_l{l}", dxmb, wn_out, l, BF16)
        g_out = _grad_rows(f"g_wout_l{l}", sv["merged"], dxmb, g_out, l)
        (dpa, dpb, dpc, dba, dbb, dbc, dya, dyb, dyc, dbg) = _merge_bwd(
            sv["ys"], wg_br, sv["proj"], bg, d_merged, l, f"merge_bwd_l{l}")
        small["b_gate"][l] = jnp.transpose(dbg, (1, 0, 2)).reshape(3, D)
        for n, dbr in enumerate((dba, dbb, dbc)):
            g_br = _grad_branch(f"g_branch{n}_l{l}", sv["ys"][n], dbr, g_br, l, n)
        d_uv, d_gn, d_ws, d_bs = _gmlp_bwd(sv["proj"], gm_norm_g[l][None], gm_w_s[l],
                                           jnp.swapaxes(gm_w_s[l], 1, 2), gm_b_s[l][..., None], dya, mix,
                                           f"gmlp_bwd_l{l}")
        small["gm_norm_g"][l], small["gm_w_s"][l], small["gm_b_s"][l] = d_gn, d_ws, d_bs[..., 0]
        d_ypre, d_xin, dw_glu, db_glu, dd = _s5_glu_bwd(sv["ypre"], sv["proj"], s5_d[l][None], wn_glu,
                                                        s5_b_glu[l][None], dyb, l, mix, f"s5glu_bwd_l{l}")
        g_glu[l] = dw_glu.astype(BF16)
        small["s5_b_glu"][l], small["s5_d"][l] = db_glu, dd
        d_s5in, dbm_re, dbm_im, dcm_re, dcm_im, da_re, da_im = _s5_scan_bwd(
            sv["proj"], *sv["s5p"], *sv["sb"], d_ypre, d_xin, mix, f"s5scan_bwd_l{l}")
        small["s5_c_re"][l] = _s5_c_unblock(dcm_re, s5_groups, S5_GROUP_CH, S5_STATE)
        small["s5_c_im"][l] = _s5_c_unblock(dcm_im, s5_groups, S5_GROUP_CH, S5_STATE)
        _, disc_vjp = jax.vjp(_s5_discretize, s5_lambda_re[l], s5_lambda_im[l], s5_log_dt[l],
                              s5_b_re[l], s5_b_im[l])
        (small["s5_lambda_re"][l], small["s5_lambda_im"][l], small["s5_log_dt"][l],
         small["s5_b_re"][l], small["s5_b_im"][l]) = disc_vjp(
            (da_re.reshape(s5_groups, S5_STATE), da_im.reshape(s5_groups, S5_STATE),
             _s5_b_unblock(dbm_re, s5_groups, S5_STATE, S5_GROUP_CH),
             _s5_b_unblock(dbm_im, s5_groups, S5_STATE, S5_GROUP_CH)))
        dq, dk, dv = _sb_bwd(sv["proj"], m_after, sv["yc32"], dyc, mix, f"sb_bwd_l{l}")
        dproj = jnp.concatenate([d_uv, d_s5in, dq.astype(BF16), dk.astype(BF16), dv.astype(BF16),
                                 dpa, dpb, dpc], axis=1)
        g_in = _grad_cols(f"g_win_l{l}", sv["h"], dproj, g_in, l)
        dh = _mm_cols_t(f"d_h_l{l}", dproj, wg_in, l, F32)
        dx, dxb, small["norm1_g"][l] = _rms_bwd(dh, sv["x"], norm1_g[l][None], dxm, f"rms1_bwd_l{l}")

    grad_x = dx[None]

    big_w = {"w_in": (D, nb_in), "w_branch": (3 * mix, bw), "w_out": (D // NDEV, D),
             "w_mlp_in": (D, ffb), "w_mlp_out": (ffb, D), "s5_w_glu": (mix // NDEV, mix)}
    parts = {"w_in": g_in, "w_branch": g_br, "w_out": g_out, "w_mlp_in": g_mi, "w_mlp_out": g_mo,
             "s5_w_glu": jnp.stack(g_glu)}
    gs = [parts[n].reshape(depth, NDEV, *big_w[n]) if n != "w_branch" else
          jnp.transpose(parts[n], (0, 1, 2, 3, 4)).reshape(depth, NDEV, 3 * mix, bw) for n in BIG]
    r1 = _rs_pair(gs, "reduce_pair")
    ss = [_pair_add(g, rr, core, f"pair_add_{n}") for g, rr, n in zip(gs, r1, BIG)]
    r2 = _rs_chips(ss, "reduce_chips")
    grads, deltas, new_m, new_v = {}, {}, {}, {}
    for n, s, rr in zip(BIG, ss, r2):
        shp = (depth, *big_w[n])
        g, d, mm, vv = _adamw_big(s, rr, chip, W[n].reshape(shp), M1[n].reshape(shp), V2[n].reshape(shp),
                                  f"adamw_{n}")
        grads[n], deltas[n], new_m[n], new_v[n] = (t.reshape(W[n].shape) for t in (g, d, mm, vv))

    small_full = {n: jnp.stack(v) for n, v in small.items()}
    small_full["final_g"] = d_final_g[0]
    full_shapes = [W[n].shape if n != "b_gate" else (depth, 3, D) for n in SMALL]
    packed = _pack([small_full[n].reshape(s) for n, s in zip(SMALL, full_shapes)])
    gathered = _all_gather([packed[None]], "gather_small")[0][0]
    summed = _unpack(_sum_devices(gathered, "sum_small"), full_shapes)
    for n, g in zip(SMALL, summed):
        grads[n] = g if n != "b_gate" else lax.dynamic_slice_in_dim(g, dev * bw, bw, axis=2)
    shapes = [W[n].shape for n in SMALL]
    d_s, m_s, v_s = _adamw_flat(_pack([grads[n] for n in SMALL]), _pack([W[n] for n in SMALL]),
                                _pack([M1[n] for n in SMALL]), _pack([V2[n] for n in SMALL]), "adamw_small")
    for n, d, mm, vv in zip(SMALL, _unpack(d_s, shapes), _unpack(m_s, shapes), _unpack(v_s, shapes)):
        deltas[n], new_m[n], new_v[n] = d, mm, vv

    return (loss, grad_x, *[grads[n] for n in WEIGHTS], *[deltas[n] for n in WEIGHTS],
            *[new_m[n] for n in WEIGHTS], *[new_v[n] for n in WEIGHTS])
```

```python
import functools
import math
from typing import Callable, NamedTuple

import jax
import jax.numpy as jnp
from jax import lax
from jax.experimental import pallas as pl
from jax.experimental.pallas import tpu as pltpu

F32 = jnp.float32
BF16 = jnp.bfloat16
MESH = pl.DeviceIdType.MESH
NDEV = 8
NCHIP = 4

EPS = 1e-6
CHUNK = 128
S5_GROUP_CH = 16
S5_STATE = 64
HEAD_DIM = 128
DT_MIN = 1e-3
DT_MAX = 1e-1
ADAM_LR = 0.001
ADAM_B1 = 0.9
ADAM_B2 = 0.999
ADAM_EPS = 1e-08
ADAM_WD = 0.01
ADAM_STEP = 10

V7X_VMEM_BYTES = 64 * 2**20
VMEM_LIMIT_BYTES = V7X_VMEM_BYTES - 8 * 2**20
SUBLANES = 8
LANES = 128

NN = (((1,), (0,)), ((), ()))
NT = (((1,), (1,)), ((), ()))
TN = (((0,), (0,)), ((), ()))

HBM = pl.BlockSpec(memory_space=pltpu.HBM)
ANY = pl.BlockSpec(memory_space=pl.ANY)


def _cp(**kw):
    return pltpu.CompilerParams(vmem_limit_bytes=VMEM_LIMIT_BYTES, **kw)


def _tile(n, pref, align=SUBLANES):
    if n <= pref:
        return n
    t = (pref // align) * align
    while t >= align:
        if n % t == 0:
            return t
        t -= align
    return n


def _dot(a, b, dims=NN):
    return lax.dot_general(a, b, dims, preferred_element_type=F32)


def _sds(shape, dtype):
    return jax.ShapeDtypeStruct(tuple(shape), dtype)


def _mesh_pos():
    return lax.axis_index("x"), lax.axis_index("y"), lax.axis_index("c")


class _Plan(NamedTuple):
    operands: list
    out_shape: list
    scratch: list
    start: Callable
    finish: Callable


def _gather_plan(sources):
    n = len(sources)

    def copies(ins, outs, sems):
        send_sems, recv_sems, local_sems = sems
        x, y, c = _mesh_pos()
        me, sib = (x, y, c), (x, y, 1 - c)
        chips = [(1 - x, y), (x, 1 - y), (1 - x, 1 - y)]

        def src(t):
            layer = sources[t][1]
            return ins[t] if layer is None else ins[t].at[layer]

        def slot(t, p):
            return outs[t].at[4 * p[0] + 2 * p[1] + p[2]]

        def copy(t, k, block, to, own=False):
            dst = slot(t, block)
            return pltpu.make_async_remote_copy(
                src_ref=src(t) if own else dst, dst_ref=dst,
                send_sem=send_sems.at[t, k], recv_sem=recv_sems.at[t, k],
                device_id=to, device_id_type=MESH)

        mine = [pltpu.make_async_copy(src(t), slot(t, me), local_sems.at[t]) for t in range(n)]
        first = []
        for t in range(n):
            first.append(copy(t, 0, me, sib, own=True))
            first += [copy(t, 1 + j, me, (*chip, c), own=True) for j, chip in enumerate(chips)]
        return me, sib, c, chips, copy, mine, first

    def start(ins, outs, sems):
        *_, mine, first = copies(ins, outs, sems)
        for cp in mine + first:
            cp.start()

    def finish(ins, outs, sems):
        me, sib, c, chips, copy, mine, first = copies(ins, outs, sems)
        passed = []
        for j, chip in enumerate(chips):
            for t in range(n):
                copy(t, 1 + j, (*chip, c), me).wait_recv()
                fw = copy(t, 4 + j, (*chip, c), sib)
                fw.start()
                passed.append(fw)
        for t in range(n):
            copy(t, 0, sib, me).wait_recv()
            for j, chip in enumerate(chips):
                copy(t, 4 + j, (*chip, 1 - c), me).wait_recv()
        for cp in first + passed:
            cp.wait_send()
        for cp in mine:
            cp.wait()

    return _Plan(
        operands=[a for a, _ in sources],
        out_shape=[_sds((NDEV,) + (a.shape if layer is None else a.shape[1:]), a.dtype) for a, layer in sources],
        scratch=[pltpu.SemaphoreType.DMA((n, 7)), pltpu.SemaphoreType.DMA((n, 7)), pltpu.SemaphoreType.DMA((n,))],
        start=start, finish=finish)


def _pair_plan(gs):
    n = len(gs)

    def copies(ins, outs, sems):
        send_sems, recv_sems = sems
        x, y, c = _mesh_pos()
        return [pltpu.make_async_remote_copy(
            src_ref=ins[t].at[2 * p + (1 - c)], dst_ref=outs[t].at[p],
            send_sem=send_sems.at[t, p], recv_sem=recv_sems.at[t, p],
            device_id=(x, y, 1 - c), device_id_type=MESH) for t in range(n) for p in range(NCHIP)]

    def start(ins, outs, sems):
        for cp in copies(ins, outs, sems):
            cp.start()

    def finish(ins, outs, sems):
        for cp in copies(ins, outs, sems):
            cp.wait()

    return _Plan(operands=list(gs), out_shape=[_sds((NCHIP,) + g.shape[1:], g.dtype) for g in gs],
                 scratch=[pltpu.SemaphoreType.DMA((n, NCHIP)), pltpu.SemaphoreType.DMA((n, NCHIP))],
                 start=start, finish=finish)


def _chips_plan(ss):
    n = len(ss)

    def copies(ins, outs, sems):
        send_sems, recv_sems = sems
        x, y, c = _mesh_pos()
        chips = [(1 - x, y), (x, 1 - y), (1 - x, 1 - y)]
        return [pltpu.make_async_remote_copy(
            src_ref=ins[t].at[2 * chip[0] + chip[1]], dst_ref=outs[t].at[j],
            send_sem=send_sems.at[t, j], recv_sem=recv_sems.at[t, j],
            device_id=(*chip, c), device_id_type=MESH) for t in range(n) for j, chip in enumerate(chips)]

    def start(ins, outs, sems):
        for cp in copies(ins, outs, sems):
            cp.start()

    def finish(ins, outs, sems):
        for cp in copies(ins, outs, sems):
            cp.wait()

    return _Plan(operands=list(ss), out_shape=[_sds((3,) + s.shape[1:], s.dtype) for s in ss],
                 scratch=[pltpu.SemaphoreType.DMA((n, 3)), pltpu.SemaphoreType.DMA((n, 3))],
                 start=start, finish=finish)


def _run_plan(plan, name):
    n_in, n_out = len(plan.operands), len(plan.out_shape)

    def body(*refs):
        ins, outs, sems = refs[:n_in], refs[n_in:n_in + n_out], refs[n_in + n_out:]
        plan.start(ins, outs, sems)
        plan.finish(ins, outs, sems)

    return list(pl.pallas_call(
        body, name=name, out_shape=plan.out_shape, in_specs=[HBM] * n_in, out_specs=[HBM] * n_out,
        scratch_shapes=plan.scratch)(*plan.operands))


def _rows_tile(r, c, itemsize=4):
    return _tile(r, max(SUBLANES, (2**20 // itemsize) // c))


def _pair_add(g, r1, core, name):
    _, r, c = g.shape
    tr = _rows_tile(r, c)
    g4 = g.reshape(NCHIP, 2, r, c)

    def body(core_ref, g_ref, r_ref, o_ref):
        o_ref[...] = (g_ref[...].astype(F32) + r_ref[...].astype(F32)).astype(o_ref.dtype)

    return pl.pallas_call(
        body, name=name,
        grid_spec=pltpu.PrefetchScalarGridSpec(
            num_scalar_prefetch=1, grid=(NCHIP, r // tr),
            in_specs=[pl.BlockSpec((None, None, tr, c), lambda p, i, cr: (p, cr[0], i, 0)),
                      pl.BlockSpec((None, tr, c), lambda p, i, cr: (p, i, 0))],
            out_specs=pl.BlockSpec((None, tr, c), lambda p, i, cr: (p, i, 0))),
        out_shape=_sds((NCHIP, r, c), BF16),
        compiler_params=_cp(),
    )(core, g4, r1)


def _adamw_math(g, w, m, v):
    m = ADAM_B1 * m + (1.0 - ADAM_B1) * g
    v = ADAM_B2 * v + (1.0 - ADAM_B2) * (g * g)
    m_hat = m / (1.0 - ADAM_B1 ** ADAM_STEP)
    v_hat = v / (1.0 - ADAM_B2 ** ADAM_STEP)
    delta = -ADAM_LR * (m_hat / (jnp.sqrt(v_hat) + ADAM_EPS) + ADAM_WD * w)
    return delta, m, v


def _adamw_big(s, r2, chip, w, m, v, layer, into, name):
    _, r, c = w.shape
    tr = _tile(r, max(SUBLANES, _rows_tile(r, c) // 2))

    def body(chip_ref, s_ref, ra_ref, rb_ref, rc_ref, w_ref, m_ref, v_ref, *rest):
        g_out, d_out, m_out, v_out = rest[4:]
        g = ((s_ref[...].astype(F32) + ra_ref[...].astype(F32)) + rb_ref[...].astype(F32)) + rc_ref[...].astype(F32)
        d, mm, vv = _adamw_math(g, w_ref[...], m_ref[...], v_ref[...])
        g_out[...] = g
        d_out[...] = d
        m_out[...] = mm
        v_out[...] = vv

    wspec = pl.BlockSpec((None, tr, c), lambda i, cr: (layer, i, 0))
    rspec = [pl.BlockSpec((None, tr, c), functools.partial(lambda i, cr, j: (j, i, 0), j=j)) for j in range(3)]
    return pl.pallas_call(
        body, name=name,
        grid_spec=pltpu.PrefetchScalarGridSpec(
            num_scalar_prefetch=1, grid=(r // tr,),
            in_specs=[pl.BlockSpec((None, tr, c), lambda i, cr: (cr[0], i, 0)),
                      *rspec, wspec, wspec, wspec, ANY, ANY, ANY, ANY],
            out_specs=[wspec] * 4),
        out_shape=[_sds(w.shape, F32)] * 4,
        input_output_aliases={8: 0, 9: 1, 10: 2, 11: 3},
        compiler_params=_cp(),
    )(chip, s, r2, r2, r2, w, m, v, *into)


def _sum_devices(parts, name):
    _, r, c = parts.shape
    tr = _tile(r, 512)

    def body(p_ref, o_ref):
        acc = p_ref[0]
        for k in range(1, NDEV):
            acc = acc + p_ref[k]
        o_ref[...] = acc

    return pl.pallas_call(
        body, name=name, grid=(r // tr,),
        in_specs=[pl.BlockSpec((NDEV, tr, c), lambda i: (0, i, 0))],
        out_specs=pl.BlockSpec((tr, c), lambda i: (i, 0)),
        out_shape=_sds((r, c), F32), compiler_params=_cp(),
    )(parts)


def _adamw_flat(g, w, m, v, name):
    r, c = w.shape
    tr = _tile(r, 512)

    def body(g_ref, w_ref, m_ref, v_ref, d_out, m_out, v_out):
        d, mm, vv = _adamw_math(g_ref[...], w_ref[...], m_ref[...], v_ref[...])
        d_out[...] = d
        m_out[...] = mm
        v_out[...] = vv

    spec = pl.BlockSpec((tr, c), lambda i: (i, 0))
    return pl.pallas_call(
        body, name=name, grid=(r // tr,), in_specs=[spec] * 4, out_specs=[spec] * 3,
        out_shape=[_sds(w.shape, F32)] * 3, compiler_params=_cp(),
    )(g, w, m, v)


def _matmul(name, a, b, *, grid, a_spec, b_spec, o_spec, out_shape, dims, acc_shape,
            extra=(), extra_specs=(), a_pro=None, epi=None, into=None, t_spec=None, t_shape=None):
    nk = grid[2]
    n_extra = len(extra)
    n_out = 1 + (t_spec is not None)

    def body(*refs):
        a_ref, b_ref = refs[0], refs[1]
        ex = refs[2:2 + n_extra]
        rest = refs[2 + n_extra + (into is not None):]
        o_ref = rest[0]

        def product():
            av = a_ref[...]
            if a_pro is not None:
                av = a_pro(av)
            return _dot(av, b_ref[...], dims)

        def finish(r):
            if epi is not None:
                r = epi(r, *[e[...] for e in ex])
            o_ref[...] = r.astype(o_ref.dtype)
            if t_spec is not None:
                rest[1][...] = r.T.astype(rest[1].dtype)

        if nk == 1:
            finish(product())
        else:
            acc_ref = rest[n_out]
            k = pl.program_id(2)

            @pl.when(k == 0)
            def _():
                acc_ref[...] = jnp.zeros_like(acc_ref)

            acc_ref[...] += product()

            @pl.when(k == nk - 1)
            def _():
                finish(acc_ref[...])

    operands = [a, b, *extra]
    in_specs = [a_spec, b_spec, *extra_specs]
    aliases = {}
    if into is not None:
        operands.append(into)
        in_specs.append(ANY)
        aliases = {len(operands) - 1: 0}
        out_shape = _sds(into.shape, into.dtype)
    if t_spec is not None:
        o_spec, out_shape = [o_spec, t_spec], [out_shape, t_shape]
    return pl.pallas_call(
        body, name=name, grid=grid, in_specs=in_specs, out_specs=o_spec, out_shape=out_shape,
        scratch_shapes=[] if nk == 1 else [pltpu.VMEM(acc_shape, F32)],
        input_output_aliases=aliases, compiler_params=_cp(),
    )(*operands)


def _rms_fwd(x, g, name):
    L, D = x.shape
    tm = _tile(L, 256)

    def body(x_ref, g_ref, o_ref, ot_ref):
        xf = x_ref[...]
        rstd = lax.rsqrt(jnp.mean(xf * xf, axis=-1, keepdims=True) + EPS)
        y = xf * rstd * g_ref[...]
        o_ref[...] = y.astype(o_ref.dtype)
        ot_ref[...] = y.T.astype(ot_ref.dtype)

    return pl.pallas_call(
        body, name=name, grid=(L // tm,),
        in_specs=[pl.BlockSpec((tm, D), lambda i: (i, 0)), pl.BlockSpec((1, D), lambda i: (0, 0))],
        out_specs=[pl.BlockSpec((tm, D), lambda i: (i, 0)), pl.BlockSpec((D, tm), lambda i: (0, i))],
        out_shape=[_sds((L, D), BF16), _sds((D, L), BF16)], compiler_params=_cp(),
    )(x, g)


def _rms_bwd(dh, x, g, dres, name):
    L, D = x.shape
    tm = _tile(L, 256)

    def body(dh_ref, x_ref, g_ref, dr_ref, dx_ref, dxb_ref, dg_ref):
        @pl.when(pl.program_id(0) == 0)
        def _():
            dg_ref[...] = jnp.zeros_like(dg_ref)

        xf = x_ref[...]
        dhf = dh_ref[...].astype(F32)
        rstd = lax.rsqrt(jnp.mean(xf * xf, axis=-1, keepdims=True) + EPS)
        xhat = xf * rstd
        dg_ref[...] += jnp.sum(dhf * xhat, axis=0, keepdims=True)
        dxh = dhf * g_ref[...]
        dx = dr_ref[...] + rstd * (dxh - xhat * jnp.mean(dxh * xhat, axis=-1, keepdims=True))
        dx_ref[...] = dx
        dxb_ref[...] = dx.astype(BF16)

    row = pl.BlockSpec((tm, D), lambda i: (i, 0))
    vec = pl.BlockSpec((1, D), lambda i: (0, 0))
    return pl.pallas_call(
        body, name=name, grid=(L // tm,), in_specs=[row, row, vec, row], out_specs=[row, row, vec],
        out_shape=[_sds((L, D), F32), _sds((L, D), BF16), _sds((1, D), F32)], compiler_params=_cp(),
    )(dh, x, g, dres)


def _loss_head(x, g, target, name):
    L, D = x.shape
    tm = _tile(L, 256)

    def body(x_ref, g_ref, t_ref, loss_ref, dx_ref, dxb_ref, dg_ref):
        @pl.when(pl.program_id(0) == 0)
        def _():
            dg_ref[...] = jnp.zeros_like(dg_ref)
            loss_ref[...] = jnp.zeros_like(loss_ref)

        xf = x_ref[...]
        gv = g_ref[...]
        rstd = lax.rsqrt(jnp.mean(xf * xf, axis=-1, keepdims=True) + EPS)
        xhat = xf * rstd
        err = xhat * gv - t_ref[...]
        part = jnp.sum(jnp.sum(err * err, axis=-1, keepdims=True), axis=0, keepdims=True)
        loss_ref[...] += jnp.broadcast_to(part * (0.5 / D), loss_ref.shape)
        dy = err * (1.0 / D)
        dg_ref[...] += jnp.sum(dy * xhat, axis=0, keepdims=True)
        dxh = dy * gv
        dx = rstd * (dxh - xhat * jnp.mean(dxh * xhat, axis=-1, keepdims=True))
        dx_ref[...] = dx
        dxb_ref[...] = dx.astype(BF16)

    row = pl.BlockSpec((tm, D), lambda i: (i, 0))
    vec = pl.BlockSpec((1, D), lambda i: (0, 0))
    lspec = pl.BlockSpec((SUBLANES, LANES), lambda i: (0, 0))
    return pl.pallas_call(
        body, name=name, grid=(L // tm,), in_specs=[row, vec, row], out_specs=[lspec, row, row, vec],
        out_shape=[_sds((SUBLANES, LANES), F32), _sds((L, D), F32), _sds((L, D), BF16), _sds((1, D), F32)],
        compiler_params=_cp(),
    )(x, g, target)


_GELU_C = math.sqrt(2.0 / math.pi)
_GELU_K = 0.044715


def _gelu(x):
    return 0.5 * x * (1.0 + jnp.tanh(_GELU_C * (x + _GELU_K * (x * x * x))))


def _gelu_and_grad(x):
    t = jnp.tanh(_GELU_C * (x + _GELU_K * (x * x * x)))
    val = 0.5 * x * (1.0 + t)
    grad = 0.5 * (1.0 + t) + 0.5 * x * (1.0 - t * t) * (_GELU_C * (1.0 + 3.0 * _GELU_K * (x * x)))
    return val, grad


def _sigmoid(x):
    e = jnp.exp(-jnp.abs(x))
    return jnp.where(x >= 0, 1.0, e) / (1.0 + e)


def _tril_mask():
    r = lax.broadcasted_iota(jnp.int32, (CHUNK, CHUNK), 0)
    c = lax.broadcasted_iota(jnp.int32, (CHUNK, CHUNK), 1)
    return r >= c


def _gmlp_fwd(proj, norm_g, w_s, b_col, mix, name):
    L = proj.shape[0]
    groups = mix // CHUNK
    tt = _tile(L, 2 * CHUNK)

    def body(uv_ref, g_ref, w_ref, b_ref, o_ref):
        z = _gelu(uv_ref[...].astype(F32))
        u, v = z[:, :mix], z[:, mix:]
        vn = v * lax.rsqrt(jnp.mean(v * v, axis=-1, keepdims=True) + EPS) * g_ref[...]
        mask = _tril_mask()
        for gi in range(groups):
            wt = jnp.where(mask, w_ref[gi], 0.0).astype(BF16)
            cols = slice(gi * CHUNK, (gi + 1) * CHUNK)
            for cc in range(tt // CHUNK):
                rows = slice(cc * CHUNK, (cc + 1) * CHUNK)
                mixed = _dot(wt, vn[rows, cols].astype(BF16)) + b_ref[gi]
                o_ref[rows, cols] = (u[rows, cols] * mixed).astype(o_ref.dtype)

    return pl.pallas_call(
        body, name=name, grid=(L // tt,),
        in_specs=[pl.BlockSpec((tt, 2 * mix), lambda i: (i, 0)),
                  pl.BlockSpec((1, mix), lambda i: (0, 0)),
                  pl.BlockSpec((groups, CHUNK, CHUNK), lambda i: (0, 0, 0)),
                  pl.BlockSpec((groups, CHUNK, 1), lambda i: (0, 0, 0))],
        out_specs=pl.BlockSpec((tt, mix), lambda i: (i, 0)),
        out_shape=_sds((L, mix), BF16), compiler_params=_cp(),
    )(proj, norm_g, w_s, b_col)


def _gmlp_bwd(proj, norm_g, w_s, w_st, b_col, dy, mix, name):
    L = proj.shape[0]
    groups = mix // CHUNK
    tt = _tile(L, 2 * CHUNK)

    def body(uv_ref, g_ref, w_ref, wt_ref, b_ref, dy_ref, duv_ref, dg_ref, dw_ref, db_ref, du_s, dvn_s):
        @pl.when(pl.program_id(0) == 0)
        def _():
            dg_ref[...] = jnp.zeros_like(dg_ref)
            dw_ref[...] = jnp.zeros_like(dw_ref)
            db_ref[...] = jnp.zeros_like(db_ref)

        z, zgrad = _gelu_and_grad(uv_ref[...].astype(F32))
        u, v = z[:, :mix], z[:, mix:]
        rstd = lax.rsqrt(jnp.mean(v * v, axis=-1, keepdims=True) + EPS)
        vhat = v * rstd
        gv = g_ref[...]
        vn = vhat * gv
        dyf = dy_ref[...].astype(F32)
        mask = _tril_mask()
        r = lax.broadcasted_iota(jnp.int32, (CHUNK, CHUNK), 0)
        c = lax.broadcasted_iota(jnp.int32, (CHUNK, CHUNK), 1)
        for gi in range(groups):
            w_low = jnp.where(mask, w_ref[gi], 0.0).astype(BF16)
            w_up = jnp.where(r <= c, wt_ref[gi], 0.0).astype(BF16)
            cols = slice(gi * CHUNK, (gi + 1) * CHUNK)
            for cc in range(tt // CHUNK):
                rows = slice(cc * CHUNK, (cc + 1) * CHUNK)
                vnb = vn[rows, cols].astype(BF16)
                mixed = _dot(w_low, vnb) + b_ref[gi]
                dyb = dyf[rows, cols]
                dm = dyb * u[rows, cols]
                dmb = dm.astype(BF16)
                du_s[rows, cols] = dyb * mixed
                dvn_s[rows, cols] = _dot(w_up, dmb)
                dw_ref[gi] += jnp.where(mask, _dot(dmb, vnb, NT), 0.0)
                db_ref[gi] += jnp.sum(dm, axis=1, keepdims=True)
        dvn = dvn_s[...]
        dg_ref[...] += jnp.sum(dvn * vhat, axis=0, keepdims=True)
        dvh = dvn * gv
        dv = rstd * (dvh - vhat * jnp.mean(dvh * vhat, axis=-1, keepdims=True))
        duv_ref[:, :mix] = (du_s[...] * zgrad[:, :mix]).astype(duv_ref.dtype)
        duv_ref[:, mix:] = (dv * zgrad[:, mix:]).astype(duv_ref.dtype)

    wspec = pl.BlockSpec((groups, CHUNK, CHUNK), lambda i: (0, 0, 0))
    bspec = pl.BlockSpec((groups, CHUNK, 1), lambda i: (0, 0, 0))
    gspec = pl.BlockSpec((1, mix), lambda i: (0, 0))
    return pl.pallas_call(
        body, name=name, grid=(L // tt,),
        in_specs=[pl.BlockSpec((tt, 2 * mix), lambda i: (i, 0)), gspec, wspec, wspec, bspec,
                  pl.BlockSpec((tt, mix), lambda i: (i, 0))],
        out_specs=[pl.BlockSpec((tt, 2 * mix), lambda i: (i, 0)), gspec, wspec, bspec],
        out_shape=[_sds((L, 2 * mix), BF16), _sds((1, mix), F32),
                   _sds((groups, CHUNK, CHUNK), F32), _sds((groups, CHUNK, 1), F32)],
        scratch_shapes=[pltpu.VMEM((tt, mix), F32), pltpu.VMEM((tt, mix), F32)],
        compiler_params=_cp(),
    )(proj, norm_g, w_s, w_st, b_col, dy)


S5_CB = 512
S5_UB = 128
S5_LEVELS = (1, 2, 4)


def _s5_fwd_consts(ar, ai):
    rows = lax.broadcasted_iota(jnp.int32, ar.shape, 0)
    out = []
    for d in S5_LEVELS:
        m = rows >= d
        out.append((jnp.where(m, ar, 0.0), jnp.where(m, ai, 0.0)))
        ar, ai = ar * ar - ai * ai, 2.0 * ar * ai
    return out


def _s5_rev_consts(ar, ai):
    rows = lax.broadcasted_iota(jnp.int32, ar.shape, 0)
    out = []
    for d in S5_LEVELS:
        m = rows < SUBLANES - d
        out.append((jnp.where(m, ar, 0.0), jnp.where(m, ai, 0.0)))
        ar, ai = ar * ar - ai * ai, 2.0 * ar * ai
    return out


def _scan8(xr, xi, consts, reverse):
    for (cr, ci), d in zip(consts, S5_LEVELS):
        sh = SUBLANES - d if reverse else d
        pr = pltpu.roll(xr, sh, 0)
        pi = pltpu.roll(xi, sh, 0)
        xr, xi = xr + (cr * pr - ci * pi), xi + (cr * pi + ci * pr)
    return xr, xi


def _row_bcast(x, row):
    rows = lax.broadcasted_iota(jnp.int32, x.shape, 0)
    return jnp.broadcast_to(jnp.sum(jnp.where(rows == row, x, 0.0), axis=0, keepdims=True), x.shape)


def _s5_forward_block(u_ref, bre_ref, bim_ref, sr, si, ar, ai, pwr, pwi, carry, tt):
    consts = _s5_fwd_consts(ar, ai)
    u = u_ref[...]
    sr[...] = _dot(u, bre_ref[...])
    si[...] = _dot(u, bim_ref[...])

    def step(r, cs):
        cr, ci = cs
        o = pl.multiple_of(r * SUBLANES, SUBLANES)
        xr, xi = _scan8(sr[pl.ds(o, SUBLANES), :], si[pl.ds(o, SUBLANES), :], consts, False)
        xr, xi = xr + (pwr * cr - pwi * ci), xi + (pwr * ci + pwi * cr)
        sr[pl.ds(o, SUBLANES), :] = xr
        si[pl.ds(o, SUBLANES), :] = xi
        return _row_bcast(xr, SUBLANES - 1), _row_bcast(xi, SUBLANES - 1)

    return lax.fori_loop(0, tt // SUBLANES, step, carry)


def _s5_powers(ar, ai):
    rows = lax.broadcasted_iota(jnp.int32, ar.shape, 0)
    return _scan8(jnp.where(rows == 0, ar, 0.0), jnp.where(rows == 0, ai, 0.0), _s5_fwd_consts(ar, ai), False)


def _s5_scan_fwd(proj, a_re, a_im, b_re, b_im, c_re, c_im, mix, name):
    L = proj.shape[0]
    S = a_re.shape[1]
    nj = mix // S5_UB
    tt = _tile(L, 256)
    ni = L // tt
    ucol = 2 * mix // S5_UB

    def body(u_ref, ar_ref, ai_ref, bre_ref, bim_ref, cre_ref, cim_ref, y_ref, sbr_ref, sbi_ref,
             sr, si, car, cai):
        ar = jnp.broadcast_to(ar_ref[...], (SUBLANES, S5_CB))
        ai = jnp.broadcast_to(ai_ref[...], (SUBLANES, S5_CB))

        @pl.when(pl.program_id(1) == 0)
        def _():
            car[...] = jnp.zeros_like(car)
            cai[...] = jnp.zeros_like(cai)

        sbr_ref[...] = car[...]
        sbi_ref[...] = cai[...]
        pwr, pwi = _s5_powers(ar, ai)
        cr, ci = _s5_forward_block(u_ref, bre_ref, bim_ref, sr, si, ar, ai, pwr, pwi,
                                   (car[...], cai[...]), tt)
        car[...] = cr
        cai[...] = ci
        y_ref[...] = _dot(sr[...].astype(BF16), cre_ref[...]) - _dot(si[...].astype(BF16), cim_ref[...])

    avec = pl.BlockSpec((1, S5_CB), lambda j, i: (0, j))
    bspec = pl.BlockSpec((S5_UB, S5_CB), lambda j, i: (j, 0))
    cspec = pl.BlockSpec((S5_CB, S5_UB), lambda j, i: (j, 0))
    sb = pl.BlockSpec((SUBLANES, S5_CB), lambda j, i: (i, j))
    return pl.pallas_call(
        body, name=name, grid=(nj, ni),
        in_specs=[pl.BlockSpec((tt, S5_UB), lambda j, i: (i, ucol + j)), avec, avec, bspec, bspec, cspec, cspec],
        out_specs=[pl.BlockSpec((tt, S5_UB), lambda j, i: (i, j)), sb, sb],
        out_shape=[_sds((L, mix), F32), _sds((ni * SUBLANES, S), F32), _sds((ni * SUBLANES, S), F32)],
        scratch_shapes=[pltpu.VMEM((tt, S5_CB), F32), pltpu.VMEM((tt, S5_CB), F32),
                        pltpu.VMEM((SUBLANES, S5_CB), F32), pltpu.VMEM((SUBLANES, S5_CB), F32)],
        compiler_params=_cp(),
    )(proj, a_re, a_im, b_re, b_im, c_re, c_im)


def _s5_scan_bwd(proj, a_re, a_im, b_re, b_im, c_re, c_im, sb_re, sb_im, dy, dxin, mix, name):
    L = proj.shape[0]
    S = a_re.shape[1]
    nj = mix // S5_UB
    tt = _tile(L, 256)
    ni = L // tt
    ucol = 2 * mix // S5_UB
    nb = tt // SUBLANES

    def body(u_ref, ar_ref, ai_ref, bre_ref, bim_ref, cre_ref, cim_ref, sbr_ref, sbi_ref, dy_ref, dx_ref,
             du_ref, dbr_ref, dbi_ref, dcr_ref, dci_ref, dar_ref, dai_ref,
             sr, si, gr, gi, car, cai, acr, aci):
        first = pl.program_id(1) == 0
        ar = jnp.broadcast_to(ar_ref[...], (SUBLANES, S5_CB))
        ai = jnp.broadcast_to(ai_ref[...], (SUBLANES, S5_CB))

        @pl.when(first)
        def _():
            for ref in (car, cai, acr, aci, dbr_ref, dbi_ref, dcr_ref, dci_ref):
                ref[...] = jnp.zeros_like(ref)

        pwr, pwi = _s5_powers(ar, ai)
        _s5_forward_block(u_ref, bre_ref, bim_ref, sr, si, ar, ai, pwr, pwi,
                          (sbr_ref[...], sbi_ref[...]), tt)

        dyb = dy_ref[...].astype(BF16)
        gr[...] = _dot(dyb, cre_ref[...], NT)
        gi[...] = -_dot(dyb, cim_ref[...], NT)

        nai = -ai
        consts = _s5_rev_consts(ar, nai)
        rows = lax.broadcasted_iota(jnp.int32, ar.shape, 0)
        last = rows == SUBLANES - 1
        qr, qi = _scan8(jnp.where(last, ar, 0.0), jnp.where(last, nai, 0.0), consts, True)

        def step(k, cs):
            cr, ci, dr, di = cs
            o = pl.multiple_of((nb - 1 - k) * SUBLANES, SUBLANES)
            xr, xi = _scan8(gr[pl.ds(o, SUBLANES), :], gi[pl.ds(o, SUBLANES), :], consts, True)
            xr, xi = xr + (qr * cr - qi * ci), xi + (qr * ci + qi * cr)
            gr[pl.ds(o, SUBLANES), :] = xr
            gi[pl.ds(o, SUBLANES), :] = xi
            hr = jnp.where(last, cr, pltpu.roll(xr, SUBLANES - 1, 0))
            hi = jnp.where(last, ci, pltpu.roll(xi, SUBLANES - 1, 0))
            s_r = sr[pl.ds(o, SUBLANES), :]
            s_i = si[pl.ds(o, SUBLANES), :]
            dr = dr + (hr * s_r + hi * s_i)
            di = di + (hi * s_r - hr * s_i)
            return _row_bcast(xr, 0), _row_bcast(xi, 0), dr, di

        cr, ci, dr, di = lax.fori_loop(0, nb, step, (car[...], cai[...], acr[...], aci[...]))
        car[...] = cr
        cai[...] = ci
        acr[...] = dr
        aci[...] = di
        dar_ref[...] = jnp.sum(dr, axis=0, keepdims=True)
        dai_ref[...] = jnp.sum(di, axis=0, keepdims=True)

        u = u_ref[...]
        gbr = gr[...].astype(BF16)
        gbi = gi[...].astype(BF16)
        dbr_ref[...] += _dot(u, gbr, TN)
        dbi_ref[...] += _dot(u, gbi, TN)
        du = _dot(gbr, bre_ref[...], NT) + _dot(gbi, bim_ref[...], NT)
        du_ref[...] = (du + dx_ref[...].astype(F32)).astype(du_ref.dtype)
        dyf = dy_ref[...].astype(BF16)
        dcr_ref[...] += _dot(sr[...].astype(BF16), dyf, TN)
        dci_ref[...] -= _dot(si[...].astype(BF16), dyf, TN)

    rev = lambda i: ni - 1 - i
    avec = pl.BlockSpec((1, S5_CB), lambda j, i: (0, j))
    bspec = pl.BlockSpec((S5_UB, S5_CB), lambda j, i: (j, 0))
    cspec = pl.BlockSpec((S5_CB, S5_UB), lambda j, i: (j, 0))
    sb = pl.BlockSpec((SUBLANES, S5_CB), lambda j, i: (rev(i), j))
    tile = pl.BlockSpec((tt, S5_UB), lambda j, i: (rev(i), j))
    return pl.pallas_call(
        body, name=name, grid=(nj, ni),
        in_specs=[pl.BlockSpec((tt, S5_UB), lambda j, i: (rev(i), ucol + j)), avec, avec, bspec, bspec,
                  cspec, cspec, sb, sb, tile, tile],
        out_specs=[tile, bspec, bspec, cspec, cspec, avec, avec],
        out_shape=[_sds((L, mix), BF16), _sds((mix, S5_CB), F32), _sds((mix, S5_CB), F32),
                   _sds((S, S5_UB), F32), _sds((S, S5_UB), F32), _sds((1, S), F32), _sds((1, S), F32)],
        scratch_shapes=[pltpu.VMEM((tt, S5_CB), F32)] * 4 + [pltpu.VMEM((SUBLANES, S5_CB), F32)] * 4,
        compiler_params=_cp(),
    )(proj, a_re, a_im, b_re, b_im, c_re, c_im, sb_re, sb_im, dy, dxin)


def _s5_glu_fwd(ypre, proj, d, w_glu, b_glu, layer, mix, name):
    L = ypre.shape[0]
    tm = _tile(L, 512)

    def body(y_ref, x_ref, d_ref, w_ref, b_ref, o_ref):
        g = _gelu(y_ref[...] + d_ref[...] * x_ref[...].astype(F32))
        z = _dot(g.astype(BF16), w_ref[...]) + b_ref[...]
        o_ref[...] = (g * _sigmoid(z)).astype(o_ref.dtype)

    row = pl.BlockSpec((tm, mix), lambda i: (i, 0))
    vec = pl.BlockSpec((1, mix), lambda i: (0, 0))
    return pl.pallas_call(
        body, name=name, grid=(L // tm,),
        in_specs=[row, pl.BlockSpec((tm, mix), lambda i: (i, 2)), vec,
                  pl.BlockSpec((None, mix, mix), lambda i: (layer, 0, 0)), vec],
        out_specs=row, out_shape=_sds((L, mix), BF16), compiler_params=_cp(),
    )(ypre, proj, d, w_glu, b_glu)


def _s5_glu_bwd(ypre, proj, d, w_glu, b_glu, dout, layer, mix, name):
    L = ypre.shape[0]
    tm = _tile(L, 512)

    def body(y_ref, x_ref, d_ref, w_ref, b_ref, do_ref, dy_ref, dx_ref, dw_ref, db_ref, dd_ref):
        @pl.when(pl.program_id(0) == 0)
        def _():
            dw_ref[...] = jnp.zeros_like(dw_ref)
            db_ref[...] = jnp.zeros_like(db_ref)
            dd_ref[...] = jnp.zeros_like(dd_ref)

        xin = x_ref[...].astype(F32)
        dv = d_ref[...]
        g, ggrad = _gelu_and_grad(y_ref[...] + dv * xin)
        gb = g.astype(BF16)
        w = w_ref[...]
        sg = _sigmoid(_dot(gb, w) + b_ref[...])
        do = do_ref[...].astype(F32)
        dz = do * g * sg * (1.0 - sg)
        dzb = dz.astype(BF16)
        dg = do * sg + _dot(dzb, w, NT)
        dw_ref[...] += _dot(gb, dzb, TN)
        db_ref[...] += jnp.sum(dz, axis=0, keepdims=True)
        dyv = dg * ggrad
        dd_ref[...] += jnp.sum(dyv * xin, axis=0, keepdims=True)
        dy_ref[...] = dyv.astype(dy_ref.dtype)
        dx_ref[...] = (dyv * dv).astype(dx_ref.dtype)

    row = pl.BlockSpec((tm, mix), lambda i: (i, 0))
    vec = pl.BlockSpec((1, mix), lambda i: (0, 0))
    mat = pl.BlockSpec((mix, mix), lambda i: (0, 0))
    return pl.pallas_call(
        body, name=name, grid=(L // tm,),
        in_specs=[row, pl.BlockSpec((tm, mix), lambda i: (i, 2)), vec,
                  pl.BlockSpec((None, mix, mix), lambda i: (layer, 0, 0)), vec, row],
        out_specs=[row, row, mat, vec, vec],
        out_shape=[_sds((L, mix), BF16), _sds((L, mix), BF16), _sds((mix, mix), F32),
                   _sds((1, mix), F32), _sds((1, mix), F32)],
        compiler_params=_cp(),
    )(ypre, proj, d, w_glu, b_glu, dout)


def _after_matrix(t):
    j = lax.broadcasted_iota(jnp.int32, (t, t), 0)
    s = lax.broadcasted_iota(jnp.int32, (t, t), 1)
    return jnp.where(j > s, 1.0, 0.0).astype(BF16)


def _suffix_sum(x, m_after):
    hi = x.astype(BF16)
    lo = (x - hi.astype(F32)).astype(BF16)
    return _dot(hi, m_after) + _dot(lo, m_after)


LOG2E = 1.4426950408889634


def _sb_block(q, k, diagonal):
    z2 = _dot(q, k, NT) * (HEAD_DIM ** -0.5 * LOG2E)
    e = jnp.exp2(-jnp.abs(z2))
    sp2 = jnp.maximum(z2, 0.0) + jnp.log(1.0 + e) * LOG2E
    if not diagonal:
        return z2, sp2, e, -sp2, None
    mask = lax.broadcasted_iota(jnp.int32, z2.shape, 1) < lax.broadcasted_iota(jnp.int32, z2.shape, 0)
    return z2, sp2, e, jnp.where(mask, -sp2, 0.0), mask


def _pair_tables(nt):
    qs = [i for i in range(nt) for _ in range(i + 1)]
    ks = [i - j for i in range(nt) for j in range(i + 1)]
    return jnp.asarray(qs, jnp.int32), jnp.asarray(ks, jnp.int32)


def _split_plan_refs(rest, n_pi, n_out, n_po, n_scratch):
    a = n_pi + n_out
    b = a + n_po
    return rest[:n_pi], rest[n_pi:a], rest[a:b], rest[b:b + n_scratch], rest[b + n_scratch:]


def _sb_fwd(proj, m_after, mix, name, plan=None):
    L = proj.shape[0]
    heads = mix // HEAD_DIM
    T = m_after.shape[0]
    nt = L // T
    npairs = nt * (nt + 1) // 2
    qc, kc, vc = (3 * mix // HEAD_DIM, 4 * mix // HEAD_DIM, 5 * mix // HEAD_DIM)
    n_pi, n_po = (len(plan.operands), len(plan.out_shape)) if plan is not None else (0, 0)

    def body(qt, kt, q_ref, k_ref, v_ref, m_ref, *rest):
        p_ins, (o_ref, o32_ref), p_outs, (acc, ra), p_sems = _split_plan_refs(rest, n_pi, 2, n_po, 2)
        h, p = pl.program_id(0), pl.program_id(1)
        qi, ki = qt[p], kt[p]

        if plan is not None:
            @pl.when((h == 0) & (p == 0))
            def _():
                plan.start(p_ins, p_outs, p_sems)

        def block(diagonal):
            z2, sp2, _, lg2, mask = _sb_block(q_ref[...], k_ref[...], diagonal)
            after = _suffix_sum(lg2, m_ref[...])
            if not diagonal:
                after = after + ra[...]
            w = jnp.exp2(z2 - sp2 + after)
            if diagonal:
                w = jnp.where(mask, w, 0.0)
            pv = _dot(w.astype(BF16), v_ref[...])
            rs = jnp.sum(lg2, axis=1, keepdims=True)
            if diagonal:
                acc[...] = pv
                ra[...] = rs
            else:
                acc[...] += pv
                ra[...] += rs

        pl.when(ki == qi)(functools.partial(block, True))
        pl.when(ki != qi)(functools.partial(block, False))

        @pl.when(ki == 0)
        def _():
            o_ref[...] = acc[...].astype(o_ref.dtype)
            o32_ref[...] = acc[...]

        if plan is not None:
            @pl.when((h == heads - 1) & (p == npairs - 1))
            def _():
                plan.finish(p_ins, p_outs, p_sems)

    qtab, ktab = _pair_tables(nt)
    kv = lambda col: pl.BlockSpec((T, HEAD_DIM), lambda h, p, qt, kt: (kt[p], col + h))
    qo = pl.BlockSpec((T, HEAD_DIM), lambda h, p, qt, kt: (qt[p], h))
    outs = pl.pallas_call(
        body, name=name,
        grid_spec=pltpu.PrefetchScalarGridSpec(
            num_scalar_prefetch=2, grid=(heads, npairs),
            in_specs=[pl.BlockSpec((T, HEAD_DIM), lambda h, p, qt, kt: (qt[p], qc + h)), kv(kc), kv(vc),
                      pl.BlockSpec((T, T), lambda h, p, qt, kt: (0, 0))] + [HBM] * n_pi,
            out_specs=[qo, qo] + [HBM] * n_po,
            scratch_shapes=[pltpu.VMEM((T, HEAD_DIM), F32), pltpu.VMEM((T, 1), F32)]
            + (plan.scratch if plan is not None else [])),
        out_shape=[_sds((L, mix), BF16), _sds((L, mix), F32)] + (plan.out_shape if plan is not None else []),
        compiler_params=_cp(),
    )(qtab, ktab, proj, proj, proj, m_after, *(plan.operands if plan is not None else []))
    return outs[0], outs[1], list(outs[2:])


def _sb_bwd(proj, m_after, out, dout, mix, name, plan=None):
    L = proj.shape[0]
    heads = mix // HEAD_DIM
    T = m_after.shape[0]
    nt = L // T
    npairs = nt * (nt + 1) // 2
    qc, kc, vc = (3 * mix // HEAD_DIM, 4 * mix // HEAD_DIM, 5 * mix // HEAD_DIM)
    scale = HEAD_DIM ** -0.5
    n_pi, n_po = (len(plan.operands), len(plan.out_shape)) if plan is not None else (0, 0)

    def body(qt, kt, q_ref, k_ref, v_ref, m_ref, o_ref, do_ref, *rest):
        p_ins, (dq_ref, dk_ref, dv_ref), p_outs, (dq_acc, ra, rp, delta), p_sems = _split_plan_refs(
            rest, n_pi, 3, n_po, 4)
        h, p = pl.program_id(0), pl.program_id(1)
        qi, ki = qt[p], kt[p]

        if plan is not None:
            @pl.when((h == 0) & (p == 0))
            def _():
                plan.start(p_ins, p_outs, p_sems)

        @pl.when(p == 0)
        def _():
            dk_ref[...] = jnp.zeros_like(dk_ref)
            dv_ref[...] = jnp.zeros_like(dv_ref)

        def block(diagonal):
            q, k, v, do = q_ref[...], k_ref[...], v_ref[...], do_ref[...]
            z2, sp2, e, lg2, mask = _sb_block(q, k, diagonal)
            after = _suffix_sum(lg2, m_ref[...])
            if diagonal:
                dl = jnp.sum(do.astype(F32) * o_ref[...], axis=1, keepdims=True)
                delta[...] = dl
            else:
                dl = delta[...]
                after = after + ra[...]
            w = jnp.exp2(z2 - sp2 + after)
            if diagonal:
                w = jnp.where(mask, w, 0.0)
            wb = w.astype(BF16)
            pm = wb.astype(F32) * _dot(do, v, NT)
            suffix = _suffix_sum(pm, m_ref[...])
            if not diagonal:
                suffix = suffix + rp[...]
            before = dl - suffix - pm
            beta = jnp.where(z2 >= 0, 1.0, e) / (1.0 + e)
            dz = (pm * (1.0 - beta) - beta * before) * scale
            if diagonal:
                dz = jnp.where(mask, dz, 0.0)
            dzb = dz.astype(BF16)
            rows = pl.ds(pl.multiple_of(ki * T, T), T)
            dk_ref[rows, :] += _dot(dzb, q, TN)
            dv_ref[rows, :] += _dot(wb, do, TN)
            dq = _dot(dzb, k)
            rs_a = jnp.sum(lg2, axis=1, keepdims=True)
            rs_p = jnp.sum(pm, axis=1, keepdims=True)
            if diagonal:
                dq_acc[...] = dq
                ra[...] = rs_a
                rp[...] = rs_p
            else:
                dq_acc[...] += dq
                ra[...] += rs_a
                rp[...] += rs_p

        pl.when(ki == qi)(functools.partial(block, True))
        pl.when(ki != qi)(functools.partial(block, False))

        @pl.when(ki == 0)
        def _():
            dq_ref[...] = dq_acc[...]

        if plan is not None:
            @pl.when((h == heads - 1) & (p == npairs - 1))
            def _():
                plan.finish(p_ins, p_outs, p_sems)

    qtab, ktab = _pair_tables(nt)
    kv = lambda col: pl.BlockSpec((T, HEAD_DIM), lambda h, p, qt, kt: (kt[p], col + h))
    qo = pl.BlockSpec((T, HEAD_DIM), lambda h, p, qt, kt: (qt[p], h))
    whole = pl.BlockSpec((L, HEAD_DIM), lambda h, p, qt, kt: (0, h))
    outs = pl.pallas_call(
        body, name=name,
        grid_spec=pltpu.PrefetchScalarGridSpec(
            num_scalar_prefetch=2, grid=(heads, npairs),
            in_specs=[pl.BlockSpec((T, HEAD_DIM), lambda h, p, qt, kt: (qt[p], qc + h)), kv(kc), kv(vc),
                      pl.BlockSpec((T, T), lambda h, p, qt, kt: (0, 0)), qo, qo] + [HBM] * n_pi,
            out_specs=[qo, whole, whole] + [HBM] * n_po,
            scratch_shapes=[pltpu.VMEM((T, HEAD_DIM), F32), pltpu.VMEM((T, 1), F32), pltpu.VMEM((T, 1), F32),
                            pltpu.VMEM((T, 1), F32)] + (plan.scratch if plan is not None else [])),
        out_shape=[_sds((L, mix), F32)] * 3 + (plan.out_shape if plan is not None else []),
        compiler_params=_cp(),
    )(qtab, ktab, proj, proj, proj, m_after, out, dout, *(plan.operands if plan is not None else []))
    return outs[0], outs[1], outs[2], list(outs[3:])


def _merge_fwd(ys, w_branch, proj, b_gate, layer, b_layer, name):
    L, mix = ys[0].shape
    bw = w_branch.shape[-1]
    D = bw * NDEV
    tm = _tile(L, 512)
    gc = 6 * mix // bw

    def body(ya, yb, yc, w_ref, pa, pb, pc, b_ref, o_ref, ot_ref):
        acc = None
        for n, (y_ref, p_ref) in enumerate(((ya, pa), (yb, pb), (yc, pc))):
            gate = _sigmoid(p_ref[...].astype(F32) + b_ref[n:n + 1, :])
            term = gate * _dot(y_ref[...], w_ref[n])
            acc = term if acc is None else acc + term
        o_ref[...] = acc.astype(o_ref.dtype)
        ot_ref[...] = acc.T.astype(ot_ref.dtype)

    yspec = pl.BlockSpec((tm, mix), lambda i, j: (i, 0))
    pspec = [pl.BlockSpec((tm, bw), functools.partial(lambda i, j, n: (i, gc + n * NDEV + j), n=n)) for n in range(3)]
    return pl.pallas_call(
        body, name=name, grid=(L // tm, NDEV),
        in_specs=[yspec, yspec, yspec,
                  pl.BlockSpec((None, None, 3, mix, bw), lambda i, j: (layer, j, 0, 0, 0)),
                  *pspec, pl.BlockSpec((None, None, 3, bw), lambda i, j: (b_layer, j, 0, 0))],
        out_specs=[pl.BlockSpec((tm, bw), lambda i, j: (i, j)), pl.BlockSpec((bw, tm), lambda i, j: (j, i))],
        out_shape=[_sds((L, D), BF16), _sds((D, L), BF16)], compiler_params=_cp(),
    )(*ys, w_branch, proj, proj, proj, b_gate)


def _merge_bwd(ys, w_branch, proj, b_gate, dmerged, layer, b_layer, name):
    L, mix = ys[0].shape
    bw = w_branch.shape[-1]
    D = bw * NDEV
    tm = _tile(L, 512)
    gc = 6 * mix // bw

    def body(ya, yb, yc, w_ref, pa, pb, pc, b_ref, dm_ref,
             dpa, dpb, dpc, dba, dbb, dbc, dya, dyb, dyc, dbg_ref, acc):
        i, j = pl.program_id(0), pl.program_id(1)

        @pl.when((i == 0) & (j == 0))
        def _():
            dbg_ref[...] = jnp.zeros_like(dbg_ref)

        @pl.when(j == 0)
        def _():
            acc[...] = jnp.zeros_like(acc)

        dm = dm_ref[...].astype(F32)
        for n, (y_ref, p_ref, dp_ref, db_ref) in enumerate(((ya, pa, dpa, dba), (yb, pb, dpb, dbb), (yc, pc, dpc, dbc))):
            gate = _sigmoid(p_ref[...].astype(F32) + b_ref[n:n + 1, :])
            br = _dot(y_ref[...], w_ref[n])
            dp = dm * br * gate * (1.0 - gate)
            dp_ref[...] = dp.astype(dp_ref.dtype)
            dbg_ref[j, n:n + 1, :] += jnp.sum(dp, axis=0, keepdims=True)
            dbr = (dm * gate).astype(BF16)
            db_ref[...] = dbr
            acc[n] += _dot(dbr, w_ref[n], NT)

        @pl.when(j == NDEV - 1)
        def _():
            for n, dy_ref in enumerate((dya, dyb, dyc)):
                dy_ref[...] = acc[n].astype(dy_ref.dtype)

    yspec = pl.BlockSpec((tm, mix), lambda i, j: (i, 0))
    ospec = pl.BlockSpec((tm, bw), lambda i, j: (i, j))
    pspec = [pl.BlockSpec((tm, bw), functools.partial(lambda i, j, n: (i, gc + n * NDEV + j), n=n)) for n in range(3)]
    return pl.pallas_call(
        body, name=name, grid=(L // tm, NDEV),
        in_specs=[yspec, yspec, yspec,
                  pl.BlockSpec((None, None, 3, mix, bw), lambda i, j: (layer, j, 0, 0, 0)),
                  *pspec, pl.BlockSpec((None, None, 3, bw), lambda i, j: (b_layer, j, 0, 0)), ospec],
        out_specs=[ospec] * 6 + [yspec] * 3 + [pl.BlockSpec((NDEV, 3, bw), lambda i, j: (0, 0, 0))],
        out_shape=[_sds((L, D), BF16)] * 6 + [_sds((L, mix), BF16)] * 3 + [_sds((NDEV, 3, bw), F32)],
        scratch_shapes=[pltpu.VMEM((3, tm, mix), F32)],
        compiler_params=_cp(),
    )(*ys, w_branch, proj, proj, proj, b_gate, dmerged)


def _ktile(n):
    for t in (1024, 768, 512, 384, 256, 128, 64, 32, 16, 8):
        if n % t == 0:
            return t
    return n


def _mm_cols(name, a, wg, layer, out_dtype, epi=None, with_transpose=False):
    M, K = a.shape
    nb = wg.shape[3]
    tm, tn = _tile(M, 512), _tile(nb, 512)
    r = nb // tn
    t_out = dict(t_spec=pl.BlockSpec((tn, tm), lambda i, j, k: (j, i)),
                 t_shape=_sds((NDEV * nb, M), out_dtype)) if with_transpose else {}
    return _matmul(
        name, a, wg, grid=(M // tm, NDEV * r, 1),
        a_spec=pl.BlockSpec((tm, K), lambda i, j, k: (i, 0)),
        b_spec=pl.BlockSpec((None, None, K, tn), lambda i, j, k: (layer, j // r, 0, j % r)),
        o_spec=pl.BlockSpec((tm, tn), lambda i, j, k: (i, j)),
        out_shape=_sds((M, NDEV * nb), out_dtype), dims=NN, acc_shape=(tm, tn), epi=epi, **t_out)


def _mm_cols_t(name, a, wg, layer, out_dtype):
    M = a.shape[0]
    K, nb = wg.shape[2], wg.shape[3]
    tm, tn, tk = _tile(M, 1024), _tile(K, 1024), _ktile(nb)
    r = nb // tk
    return _matmul(
        name, a, wg, grid=(M // tm, K // tn, NDEV * r),
        a_spec=pl.BlockSpec((tm, tk), lambda i, j, k: (i, k)),
        b_spec=pl.BlockSpec((None, None, tn, tk), lambda i, j, k: (layer, k // r, j, k % r)),
        o_spec=pl.BlockSpec((tm, tn), lambda i, j, k: (i, j)),
        out_shape=_sds((M, K), out_dtype), dims=NT, acc_shape=(tm, tn))


def _mm_rows(name, a, wn, layer, out_dtype, res, a_pro=None):
    M, K = a.shape
    N = wn.shape[2]
    tm, tn, tk = _tile(M, 512), _tile(N, 512), _tile(K, 2048)
    tile = pl.BlockSpec((tm, tn), lambda i, j, k: (i, j))
    return _matmul(
        name, a, wn, grid=(M // tm, N // tn, K // tk),
        a_spec=pl.BlockSpec((tm, tk), lambda i, j, k: (i, k)),
        b_spec=pl.BlockSpec((None, tk, tn), lambda i, j, k: (layer, k, j)),
        o_spec=tile, out_shape=_sds((M, N), out_dtype), dims=NN, acc_shape=(tm, tn),
        extra=(res,), extra_specs=(tile,), epi=lambda acc, rv: acc + rv, a_pro=a_pro)


def _mm_rows_t(name, a, wn, layer, out_dtype, extra=(), epi=None):
    M, N = a.shape
    K = wn.shape[1]
    tm, tn = _tile(M, 512), _tile(K, 512)
    tile = pl.BlockSpec((tm, tn), lambda i, j, k: (i, j))
    return _matmul(
        name, a, wn, grid=(M // tm, K // tn, 1),
        a_spec=pl.BlockSpec((tm, N), lambda i, j, k: (i, 0)),
        b_spec=pl.BlockSpec((None, tn, N), lambda i, j, k: (layer, j, 0)),
        o_spec=tile, out_shape=_sds((M, K), out_dtype), dims=NT, acc_shape=(tm, tn),
        extra=extra, extra_specs=(tile,) * len(extra), epi=epi)


def _mm_grad(name, a, dy, *, tokens_last, o_block, o_map, tn, out_shape=None, into=None, a_pro=None):
    K, L = a.shape if tokens_last else a.shape[::-1]
    N = dy.shape[1]
    tm, tt = _tile(K, 512), _tile(L, 1024)
    a_spec = (pl.BlockSpec((tm, tt), lambda i, j, k: (i, k)) if tokens_last
              else pl.BlockSpec((tt, tm), lambda i, j, k: (k, i)))
    return _matmul(
        name, a, dy, grid=(K // tm, N // tn, L // tt), a_spec=a_spec,
        b_spec=pl.BlockSpec((tt, tn), lambda i, j, k: (k, j)),
        o_spec=pl.BlockSpec(o_block(tm, tn), o_map), out_shape=out_shape,
        dims=NN if tokens_last else TN, acc_shape=(tm, tn), a_pro=a_pro, into=into)


def _grad_cols(name, a_t, dy):
    nb = dy.shape[1] // NDEV
    tn = _tile(nb, 512)
    r = nb // tn
    return _mm_grad(name, a_t, dy, tokens_last=True, o_block=lambda tm, t: (None, tm, t),
                    o_map=lambda i, j, k: (j // r, i, j % r), tn=tn,
                    out_shape=_sds((NDEV, a_t.shape[0], nb), BF16))


def _grad_rows(name, a_t, dy, a_pro=None):
    tn = _tile(dy.shape[1], 512)
    return _mm_grad(name, a_t, dy, tokens_last=True, o_block=lambda tm, t: (tm, t),
                    o_map=lambda i, j, k: (i, j), tn=tn,
                    out_shape=_sds((a_t.shape[0], dy.shape[1]), BF16), a_pro=a_pro)


def _grad_branch(name, y, dbr, into, n):
    bw = into.shape[3]
    return _mm_grad(name, y, dbr, tokens_last=False, o_block=lambda tm, t: (None, None, tm, t),
                    o_map=lambda i, j, k: (j, n, i, 0), tn=bw, into=into)


def _s5_discretize(lam_re, lam_im, log_dt, b_re, b_im):
    dt = jnp.exp(log_dt)[:, None]
    mag = jnp.exp(lam_re * dt)
    ab_re = mag * jnp.cos(lam_im * dt)
    ab_im = mag * jnp.sin(lam_im * dt)
    den = lam_re * lam_re + lam_im * lam_im
    n_re = ab_re - 1.0
    n_im = ab_im
    k_re = (n_re * lam_re + n_im * lam_im) / den
    k_im = (n_im * lam_re - n_re * lam_im) / den
    bb_re = k_re[..., None] * b_re - k_im[..., None] * b_im
    bb_im = k_re[..., None] * b_im + k_im[..., None] * b_re
    return ab_re, ab_im, bb_re, bb_im


def _s5_b_blocks(bb):
    g, p, h = bb.shape
    t = bb.reshape(g // 8, 8, p, h)
    return jnp.einsum('jiph,ik->jihkp', t, jnp.eye(8, dtype=bb.dtype)).reshape(g * h, 8 * p)


def _s5_b_unblock(m, g, p, h):
    t = m.reshape(g // 8, 8, h, 8, p)
    return jnp.einsum('jihkp,ik->jiph', t, jnp.eye(8, dtype=m.dtype)).reshape(g, p, h)


def _s5_c_blocks(c):
    g, h, p = c.shape
    t = c.reshape(g // 8, 8, h, p)
    return jnp.einsum('jihp,ik->jipkh', t, jnp.eye(8, dtype=c.dtype)).reshape(g * p, 8 * h)


def _s5_c_unblock(m, g, h, p):
    t = m.reshape(g // 8, 8, p, 8, h)
    return jnp.einsum('jipkh,ik->jihp', t, jnp.eye(8, dtype=m.dtype)).reshape(g, h, p)


BIG = ("w_in", "w_branch", "w_out", "w_mlp_in", "w_mlp_out", "s5_w_glu")
SMALL = ("norm1_g", "gm_norm_g", "gm_w_s", "gm_b_s", "s5_lambda_re", "s5_lambda_im", "s5_log_dt",
         "s5_b_re", "s5_b_im", "s5_c_re", "s5_c_im", "s5_d", "s5_b_glu", "norm2_g", "final_g", "b_gate")
WEIGHTS = ("norm1_g", "w_in", "b_gate", "gm_norm_g", "gm_w_s", "gm_b_s", "s5_lambda_re", "s5_lambda_im",
           "s5_log_dt", "s5_b_re", "s5_b_im", "s5_c_re", "s5_c_im", "s5_d", "s5_w_glu", "s5_b_glu",
           "w_branch", "w_out", "norm2_g", "w_mlp_in", "w_mlp_out", "final_g")
FLAT_ROWS = 512


def _pack(arrays):
    flat = jnp.concatenate([a.reshape(-1) for a in arrays])
    unit = FLAT_ROWS * LANES
    pad = (-flat.shape[0]) % unit
    return jnp.pad(flat, (0, pad)).reshape(-1, LANES)


def _unpack(flat2d, shapes):
    flat = flat2d.reshape(-1)
    out, off = [], 0
    for s in shapes:
        n = math.prod(s)
        out.append(flat[off:off + n].reshape(s))
        off += n
    return out


def kernel(x, norm1_g, w_in, b_gate, gm_norm_g, gm_w_s, gm_b_s, s5_lambda_re, s5_lambda_im, s5_log_dt, s5_b_re, s5_b_im, s5_c_re, s5_c_im, s5_d, s5_w_glu, s5_b_glu, w_branch, w_out, norm2_g, w_mlp_in, w_mlp_out, final_g, loss_target, m_norm1_g, m_w_in, m_b_gate, m_gm_norm_g, m_gm_w_s, m_gm_b_s, m_s5_lambda_re, m_s5_lambda_im, m_s5_log_dt, m_s5_b_re, m_s5_b_im, m_s5_c_re, m_s5_c_im, m_s5_d, m_s5_w_glu, m_s5_b_glu, m_w_branch, m_w_out, m_norm2_g, m_w_mlp_in, m_w_mlp_out, m_final_g, v_norm1_g, v_w_in, v_b_gate, v_gm_norm_g, v_gm_w_s, v_gm_b_s, v_s5_lambda_re, v_s5_lambda_im, v_s5_log_dt, v_s5_b_re, v_s5_b_im, v_s5_c_re, v_s5_c_im, v_s5_d, v_s5_w_glu, v_s5_b_glu, v_w_branch, v_w_out, v_norm2_g, v_w_mlp_in, v_w_mlp_out, v_final_g):
    P = dict(locals())
    W = {n: P[n] for n in WEIGHTS}
    M1 = {n: P["m_" + n] for n in WEIGHTS}
    V2 = {n: P["v_" + n] for n in WEIGHTS}

    L, D = x.shape[1], x.shape[2]
    depth = norm1_g.shape[0]
    mix = D // 2
    nb_in, ffb, bw = w_in.shape[2], w_mlp_in.shape[2], w_branch.shape[3]
    ff = ffb * NDEV
    s5_groups = mix // S5_GROUP_CH
    assert mix % S5_UB == 0 and L % CHUNK == 0 and w_in.shape[2] * NDEV == 6 * mix + 3 * D

    xi, yi, ci = _mesh_pos()
    core = ci.astype(jnp.int32).reshape(1)
    chip = (2 * xi + yi).astype(jnp.int32).reshape(1)
    dev = 4 * xi + 2 * yi + ci

    local16 = [P[n].astype(BF16) for n in BIG]

    def weights_of(gathered):
        g_in_, g_br_, g_out_, g_mi_, g_mo_, g_glu_ = gathered
        return dict(w_in=g_in_[None], w_branch=g_br_[None], w_out=g_out_.reshape(1, D, D), w_mlp_in=g_mi_[None],
                    w_mlp_out=g_mo_.reshape(1, ff, D), s5_w_glu=g_glu_.reshape(1, mix, mix))

    first = _run_plan(_gather_plan([(a, 0) for a in local16] + [(b_gate, None)]), "gather_l0")
    wl = weights_of(first[:6])
    bg = jnp.swapaxes(first[6], 0, 1)

    T = _tile(L, 512)
    m_after = _after_matrix(T)

    xcur = x[0]
    saved = []
    for l in range(depth):
        h, h_t = _rms_fwd(xcur, norm1_g[l][None], f"rms1_l{l}")
        proj = _mm_cols(f"proj_l{l}", h, wl["w_in"], 0, BF16)
        ya = _gmlp_fwd(proj, gm_norm_g[l][None], gm_w_s[l], gm_b_s[l][..., None], mix, f"gmlp_l{l}")
        ab_re, ab_im, bb_re, bb_im = _s5_discretize(s5_lambda_re[l], s5_lambda_im[l], s5_log_dt[l],
                                                    s5_b_re[l], s5_b_im[l])
        s5p = (ab_re.reshape(1, -1), ab_im.reshape(1, -1),
               _s5_b_blocks(bb_re).astype(BF16), _s5_b_blocks(bb_im).astype(BF16),
               _s5_c_blocks(s5_c_re[l]).astype(BF16), _s5_c_blocks(s5_c_im[l]).astype(BF16))
        ypre, sb_re, sb_im = _s5_scan_fwd(proj, *s5p, mix, f"s5scan_l{l}")
        yb = _s5_glu_fwd(ypre, proj, s5_d[l][None], wl["s5_w_glu"], s5_b_glu[l][None], 0, mix, f"s5glu_l{l}")
        nxt = _gather_plan([(a, l + 1) for a in local16]) if l + 1 < depth else None
        yc, yc32, nxt_w = _sb_fwd(proj, m_after, mix, f"sb_l{l}", plan=nxt)
        merged, merged_t = _merge_fwd((ya, yb, yc), wl["w_branch"], proj, bg, 0, l, f"merge_l{l}")
        xmid = _mm_rows(f"wout_l{l}", merged, wl["w_out"], 0, F32, xcur)
        h2, h2_t = _rms_fwd(xmid, norm2_g[l][None], f"rms2_l{l}")
        r, r_t = _mm_cols(f"mlpin_l{l}", h2, wl["w_mlp_in"], 0, BF16, epi=lambda acc: jnp.maximum(acc, 0.0),
                          with_transpose=True)
        xout = _mm_rows(f"mlpout_l{l}", r, wl["w_mlp_out"], 0, F32, xmid, a_pro=lambda t: t * t)
        saved.append(dict(x=xcur, h_t=h_t, proj=proj, ys=(ya, yb, yc), s5p=s5p, ypre=ypre, sb=(sb_re, sb_im),
                          yc32=yc32, merged_t=merged_t, xmid=xmid, h2_t=h2_t, r=r, r_t=r_t, w=wl))
        xcur = xout
        if nxt is not None:
            wl = weights_of(nxt_w)

    loss_tile, dx, dxb, d_final_g = _loss_head(xcur, final_g[None], loss_target[0], "loss_head")
    loss = lax.psum(loss_tile[0, 0], ("x", "y", "c"))

    big_w = {"w_in": (D, nb_in), "w_branch": (3 * mix, bw), "w_out": (D // NDEV, D),
             "w_mlp_in": (D, ffb), "w_mlp_out": (ffb, D), "s5_w_glu": (mix // NDEV, mix)}
    small = {n: [None] * depth for n in SMALL if n != "final_g"}
    pair_sums = [None] * depth
    chip_parts = [None] * depth

    for l in reversed(range(depth)):
        sv = saved[l]
        wl = sv["w"]
        d_a = _mm_rows_t(f"d_act_l{l}", dxb, wl["w_mlp_out"], 0, BF16, extra=(sv["r"],),
                         epi=lambda acc, rv: acc * (2.0 * rv.astype(F32)))
        g_mo = _grad_rows(f"g_mlpout_l{l}", sv["r_t"], dxb, a_pro=lambda t: t * t)
        g_mi = _grad_cols(f"g_mlpin_l{l}", sv["h2_t"], d_a)
        dh2 = _mm_cols_t(f"d_h2_l{l}", d_a, wl["w_mlp_in"], 0, F32)
        dxm, dxmb, small["norm2_g"][l] = _rms_bwd(dh2, sv["xmid"], norm2_g[l][None], dx, f"rms2_bwd_l{l}")
        d_merged = _mm_rows_t(f"d_merged_l{l}", dxmb, wl["w_out"], 0, BF16)
        g_out = _grad_rows(f"g_wout_l{l}", sv["merged_t"], dxmb)
        (dpa, dpb, dpc, dba, dbb, dbc, dya, dyb, dyc, dbg) = _merge_bwd(
            sv["ys"], wl["w_branch"], sv["proj"], bg, d_merged, 0, l, f"merge_bwd_l{l}")
        small["b_gate"][l] = jnp.transpose(dbg, (1, 0, 2)).reshape(3, D)
        g_br = lax.empty((NDEV, 3, mix, bw), BF16)
        for n, dbr in enumerate((dba, dbb, dbc)):
            g_br = _grad_branch(f"g_branch{n}_l{l}", sv["ys"][n], dbr, g_br, n)
        d_uv, d_gn, d_ws, d_bs = _gmlp_bwd(sv["proj"], gm_norm_g[l][None], gm_w_s[l],
                                           jnp.swapaxes(gm_w_s[l], 1, 2), gm_b_s[l][..., None], dya, mix,
                                           f"gmlp_bwd_l{l}")
        small["gm_norm_g"][l], small["gm_w_s"][l], small["gm_b_s"][l] = d_gn, d_ws, d_bs[..., 0]
        d_ypre, d_xin, dw_glu, db_glu, dd = _s5_glu_bwd(sv["ypre"], sv["proj"], s5_d[l][None], wl["s5_w_glu"],
                                                        s5_b_glu[l][None], dyb, 0, mix, f"s5glu_bwd_l{l}")
        g_glu = dw_glu.astype(BF16)
        small["s5_b_glu"][l], small["s5_d"][l] = db_glu, dd
        d_s5in, dbm_re, dbm_im, dcm_re, dcm_im, da_re, da_im = _s5_scan_bwd(
            sv["proj"], *sv["s5p"], *sv["sb"], d_ypre, d_xin, mix, f"s5scan_bwd_l{l}")
        small["s5_c_re"][l] = _s5_c_unblock(dcm_re, s5_groups, S5_GROUP_CH, S5_STATE)
        small["s5_c_im"][l] = _s5_c_unblock(dcm_im, s5_groups, S5_GROUP_CH, S5_STATE)
        _, disc_vjp = jax.vjp(_s5_discretize, s5_lambda_re[l], s5_lambda_im[l], s5_log_dt[l],
                              s5_b_re[l], s5_b_im[l])
        (small["s5_lambda_re"][l], small["s5_lambda_im"][l], small["s5_log_dt"][l],
         small["s5_b_re"][l], small["s5_b_im"][l]) = disc_vjp(
            (da_re.reshape(s5_groups, S5_STATE), da_im.reshape(s5_groups, S5_STATE),
             _s5_b_unblock(dbm_re, s5_groups, S5_STATE, S5_GROUP_CH),
             _s5_b_unblock(dbm_im, s5_groups, S5_STATE, S5_GROUP_CH)))
        above = _chips_plan(pair_sums[l + 1]) if l + 1 < depth else None
        dq, dk, dv, landed = _sb_bwd(sv["proj"], m_after, sv["yc32"], dyc, mix, f"sb_bwd_l{l}", plan=above)
        if above is not None:
            chip_parts[l + 1] = landed
        dproj = jnp.concatenate([d_uv, d_s5in, dq.astype(BF16), dk.astype(BF16), dv.astype(BF16),
                                 dpa, dpb, dpc], axis=1)
        g_in = _grad_cols(f"g_win_l{l}", sv["h_t"], dproj)
        dh = _mm_cols_t(f"d_h_l{l}", dproj, wl["w_in"], 0, F32)
        dx, dxb, small["norm1_g"][l] = _rms_bwd(dh, sv["x"], norm1_g[l][None], dxm, f"rms1_bwd_l{l}")
        parts = {"w_in": g_in, "w_branch": g_br, "w_out": g_out, "w_mlp_in": g_mi, "w_mlp_out": g_mo,
                 "s5_w_glu": g_glu}
        gs = [parts[n].reshape(NDEV, *big_w[n]) for n in BIG]
        r1 = _run_plan(_pair_plan(gs), f"reduce_pair_l{l}")
        pair_sums[l] = [_pair_add(g, rr, core, f"pair_add_{n}_l{l}") for g, rr, n in zip(gs, r1, BIG)]

    grad_x = dx[None]
    chip_parts[0] = _run_plan(_chips_plan(pair_sums[0]), "reduce_chips_l0")

    grads, deltas, new_m, new_v = {}, {}, {}, {}
    for t, n in enumerate(BIG):
        shp = (depth, *big_w[n])
        outs = [lax.empty(shp, F32) for _ in range(4)]
        for l in range(depth):
            outs = _adamw_big(pair_sums[l][t], chip_parts[l][t], chip, W[n].reshape(shp), M1[n].reshape(shp),
                              V2[n].reshape(shp), l, outs, f"adamw_{n}_l{l}")
        grads[n], deltas[n], new_m[n], new_v[n] = (o.reshape(W[n].shape) for o in outs)

    small_full = {n: jnp.stack(v) for n, v in small.items()}
    small_full["final_g"] = d_final_g[0]
    full_shapes = [W[n].shape if n != "b_gate" else (depth, 3, D) for n in SMALL]
    packed = _pack([small_full[n].reshape(s) for n, s in zip(SMALL, full_shapes)])
    gathered = _run_plan(_gather_plan([(packed, None)]), "gather_small")[0]
    summed = _unpack(_sum_devices(gathered, "sum_small"), full_shapes)
    for n, g in zip(SMALL, summed):
        grads[n] = g if n != "b_gate" else lax.dynamic_slice_in_dim(g, dev * bw, bw, axis=2)
    shapes = [W[n].shape for n in SMALL]
    d_s, m_s, v_s = _adamw_flat(_pack([grads[n] for n in SMALL]), _pack([W[n] for n in SMALL]),
                                _pack([M1[n] for n in SMALL]), _pack([V2[n] for n in SMALL]), "adamw_small")
    for n, d, mm, vv in zip(SMALL, _unpack(d_s, shapes), _unpack(m_s, shapes), _unpack(v_s, shapes)):
        deltas[n], new_m[n], new_v[n] = d, mm, vv

    return (loss, grad_x, *[grads[n] for n in WEIGHTS], *[deltas[n] for n in WEIGHTS],
            *[new_m[n] for n in WEIGHTS], *[new_v[n] for n in WEIGHTS])
```

```python
import functools
import math
from typing import Callable, NamedTuple

import jax
import jax.numpy as jnp
from jax import lax
from jax.experimental import pallas as pl
from jax.experimental.pallas import tpu as pltpu

F32 = jnp.float32
BF16 = jnp.bfloat16
MESH = pl.DeviceIdType.MESH
NDEV = 8
NCHIP = 4

EPS = 1e-6
CHUNK = 128
S5_GROUP_CH = 16
S5_STATE = 64
HEAD_DIM = 128
DT_MIN = 1e-3
DT_MAX = 1e-1
ADAM_LR = 0.001
ADAM_B1 = 0.9
ADAM_B2 = 0.999
ADAM_EPS = 1e-08
ADAM_WD = 0.01
ADAM_STEP = 10

V7X_VMEM_BYTES = 64 * 2**20
VMEM_LIMIT_BYTES = V7X_VMEM_BYTES - 8 * 2**20
SUBLANES = 8
LANES = 128

NN = (((1,), (0,)), ((), ()))
NT = (((1,), (1,)), ((), ()))
TN = (((0,), (0,)), ((), ()))

HBM = pl.BlockSpec(memory_space=pltpu.HBM)
ANY = pl.BlockSpec(memory_space=pl.ANY)


def _cp(**kw):
    return pltpu.CompilerParams(vmem_limit_bytes=VMEM_LIMIT_BYTES, **kw)


def _tile(n, pref, align=SUBLANES):
    if n <= pref:
        return n
    t = (pref // align) * align
    while t >= align:
        if n % t == 0:
            return t
        t -= align
    return n


def _dot(a, b, dims=NN):
    return lax.dot_general(a, b, dims, preferred_element_type=F32)


def _sds(shape, dtype):
    return jax.ShapeDtypeStruct(tuple(shape), dtype)


def _mesh_pos():
    return lax.axis_index("x"), lax.axis_index("y"), lax.axis_index("c")


class _Plan(NamedTuple):
    operands: list
    out_shape: list
    scratch: list
    start: Callable
    finish: Callable


def _gather_plan(sources):
    n = len(sources)

    def copies(ins, outs, sems):
        send_sems, recv_sems, local_sems = sems
        x, y, c = _mesh_pos()
        me, sib = (x, y, c), (x, y, 1 - c)
        chips = [(1 - x, y), (x, 1 - y), (1 - x, 1 - y)]

        def src(t):
            layer = sources[t][1]
            return ins[t] if layer is None else ins[t].at[layer]

        def slot(t, p):
            return outs[t].at[4 * p[0] + 2 * p[1] + p[2]]

        def copy(t, k, block, to, own=False):
            dst = slot(t, block)
            return pltpu.make_async_remote_copy(
                src_ref=src(t) if own else dst, dst_ref=dst,
                send_sem=send_sems.at[t, k], recv_sem=recv_sems.at[t, k],
                device_id=to, device_id_type=MESH)

        mine = [pltpu.make_async_copy(src(t), slot(t, me), local_sems.at[t]) for t in range(n)]
        first = []
        for t in range(n):
            first.append(copy(t, 0, me, sib, own=True))
            first += [copy(t, 1 + j, me, (*chip, c), own=True) for j, chip in enumerate(chips)]
        return me, sib, c, chips, copy, mine, first

    def start(ins, outs, sems):
        *_, mine, first = copies(ins, outs, sems)
        for cp in mine + first:
            cp.start()

    def finish(ins, outs, sems):
        me, sib, c, chips, copy, mine, first = copies(ins, outs, sems)
        passed = []
        for j, chip in enumerate(chips):
            for t in range(n):
                copy(t, 1 + j, (*chip, c), me).wait_recv()
                fw = copy(t, 4 + j, (*chip, c), sib)
                fw.start()
                passed.append(fw)
        for t in range(n):
            copy(t, 0, sib, me).wait_recv()
            for j, chip in enumerate(chips):
                copy(t, 4 + j, (*chip, 1 - c), me).wait_recv()
        for cp in first + passed:
            cp.wait_send()
        for cp in mine:
            cp.wait()

    return _Plan(
        operands=[a for a, _ in sources],
        out_shape=[_sds((NDEV,) + (a.shape if layer is None else a.shape[1:]), a.dtype) for a, layer in sources],
        scratch=[pltpu.SemaphoreType.DMA((n, 7)), pltpu.SemaphoreType.DMA((n, 7)), pltpu.SemaphoreType.DMA((n,))],
        start=start, finish=finish)


def _pair_plan(gs):
    n = len(gs)

    def copies(ins, outs, sems):
        send_sems, recv_sems = sems
        x, y, c = _mesh_pos()
        return [pltpu.make_async_remote_copy(
            src_ref=ins[t].at[2 * p + (1 - c)], dst_ref=outs[t].at[p],
            send_sem=send_sems.at[t, p], recv_sem=recv_sems.at[t, p],
            device_id=(x, y, 1 - c), device_id_type=MESH) for t in range(n) for p in range(NCHIP)]

    def start(ins, outs, sems):
        for cp in copies(ins, outs, sems):
            cp.start()

    def finish(ins, outs, sems):
        for cp in copies(ins, outs, sems):
            cp.wait()

    return _Plan(operands=list(gs), out_shape=[_sds((NCHIP,) + g.shape[1:], g.dtype) for g in gs],
                 scratch=[pltpu.SemaphoreType.DMA((n, NCHIP)), pltpu.SemaphoreType.DMA((n, NCHIP))],
                 start=start, finish=finish)


def _chips_plan(ss):
    n = len(ss)

    def copies(ins, outs, sems):
        send_sems, recv_sems = sems
        x, y, c = _mesh_pos()
        chips = [(1 - x, y), (x, 1 - y), (1 - x, 1 - y)]
        return [pltpu.make_async_remote_copy(
            src_ref=ins[t].at[2 * chip[0] + chip[1]], dst_ref=outs[t].at[j],
            send_sem=send_sems.at[t, j], recv_sem=recv_sems.at[t, j],
            device_id=(*chip, c), device_id_type=MESH) for t in range(n) for j, chip in enumerate(chips)]

    def start(ins, outs, sems):
        for cp in copies(ins, outs, sems):
            cp.start()

    def finish(ins, outs, sems):
        for cp in copies(ins, outs, sems):
            cp.wait()

    return _Plan(operands=list(ss), out_shape=[_sds((3,) + s.shape[1:], s.dtype) for s in ss],
                 scratch=[pltpu.SemaphoreType.DMA((n, 3)), pltpu.SemaphoreType.DMA((n, 3))],
                 start=start, finish=finish)


def _run_plan(plan, name):
    n_in, n_out = len(plan.operands), len(plan.out_shape)

    def body(*refs):
        ins, outs, sems = refs[:n_in], refs[n_in:n_in + n_out], refs[n_in + n_out:]
        plan.start(ins, outs, sems)
        plan.finish(ins, outs, sems)

    return list(pl.pallas_call(
        body, name=name, out_shape=plan.out_shape, in_specs=[HBM] * n_in, out_specs=[HBM] * n_out,
        scratch_shapes=plan.scratch)(*plan.operands))


def _rows_tile(r, c, itemsize=4):
    return _tile(r, max(SUBLANES, (2**20 // itemsize) // c))


def _pair_add(g, r1, core, name):
    _, r, c = g.shape
    tr = _rows_tile(r, c)
    g4 = g.reshape(NCHIP, 2, r, c)

    def body(core_ref, g_ref, r_ref, o_ref):
        o_ref[...] = (g_ref[...].astype(F32) + r_ref[...].astype(F32)).astype(o_ref.dtype)

    return pl.pallas_call(
        body, name=name,
        grid_spec=pltpu.PrefetchScalarGridSpec(
            num_scalar_prefetch=1, grid=(NCHIP, r // tr),
            in_specs=[pl.BlockSpec((None, None, tr, c), lambda p, i, cr: (p, cr[0], i, 0)),
                      pl.BlockSpec((None, tr, c), lambda p, i, cr: (p, i, 0))],
            out_specs=pl.BlockSpec((None, tr, c), lambda p, i, cr: (p, i, 0))),
        out_shape=_sds((NCHIP, r, c), BF16),
        compiler_params=_cp(),
    )(core, g4, r1)


def _adamw_math(g, w, m, v):
    m = ADAM_B1 * m + (1.0 - ADAM_B1) * g
    v = ADAM_B2 * v + (1.0 - ADAM_B2) * (g * g)
    m_hat = m / (1.0 - ADAM_B1 ** ADAM_STEP)
    v_hat = v / (1.0 - ADAM_B2 ** ADAM_STEP)
    delta = -ADAM_LR * (m_hat / (jnp.sqrt(v_hat) + ADAM_EPS) + ADAM_WD * w)
    return delta, m, v


def _adamw_big(s, r2, chip, w, m, v, layer, into, name):
    _, r, c = w.shape
    tr = _tile(r, max(SUBLANES, _rows_tile(r, c) // 2))

    def body(chip_ref, s_ref, ra_ref, rb_ref, rc_ref, w_ref, m_ref, v_ref, *rest):
        g_out, d_out, m_out, v_out = rest[4:]
        g = ((s_ref[...].astype(F32) + ra_ref[...].astype(F32)) + rb_ref[...].astype(F32)) + rc_ref[...].astype(F32)
        d, mm, vv = _adamw_math(g, w_ref[...], m_ref[...], v_ref[...])
        g_out[...] = g
        d_out[...] = d
        m_out[...] = mm
        v_out[...] = vv

    wspec = pl.BlockSpec((None, tr, c), lambda i, cr: (layer, i, 0))
    rspec = [pl.BlockSpec((None, tr, c), functools.partial(lambda i, cr, j: (j, i, 0), j=j)) for j in range(3)]
    return pl.pallas_call(
        body, name=name,
        grid_spec=pltpu.PrefetchScalarGridSpec(
            num_scalar_prefetch=1, grid=(r // tr,),
            in_specs=[pl.BlockSpec((None, tr, c), lambda i, cr: (cr[0], i, 0)),
                      *rspec, wspec, wspec, wspec, ANY, ANY, ANY, ANY],
            out_specs=[wspec] * 4),
        out_shape=[_sds(w.shape, F32)] * 4,
        input_output_aliases={8: 0, 9: 1, 10: 2, 11: 3},
        compiler_params=_cp(),
    )(chip, s, r2, r2, r2, w, m, v, *into)


def _sum_devices(parts, name):
    _, r, c = parts.shape
    tr = _tile(r, 512)

    def body(p_ref, o_ref):
        acc = p_ref[0]
        for k in range(1, NDEV):
            acc = acc + p_ref[k]
        o_ref[...] = acc

    return pl.pallas_call(
        body, name=name, grid=(r // tr,),
        in_specs=[pl.BlockSpec((NDEV, tr, c), lambda i: (0, i, 0))],
        out_specs=pl.BlockSpec((tr, c), lambda i: (i, 0)),
        out_shape=_sds((r, c), F32), compiler_params=_cp(),
    )(parts)


def _adamw_flat(g, w, m, v, name):
    r, c = w.shape
    tr = _tile(r, 512)

    def body(g_ref, w_ref, m_ref, v_ref, d_out, m_out, v_out):
        d, mm, vv = _adamw_math(g_ref[...], w_ref[...], m_ref[...], v_ref[...])
        d_out[...] = d
        m_out[...] = mm
        v_out[...] = vv

    spec = pl.BlockSpec((tr, c), lambda i: (i, 0))
    return pl.pallas_call(
        body, name=name, grid=(r // tr,), in_specs=[spec] * 4, out_specs=[spec] * 3,
        out_shape=[_sds(w.shape, F32)] * 3, compiler_params=_cp(),
    )(g, w, m, v)


def _matmul(name, a, b, *, grid, a_spec, b_spec, o_spec, out_shape, dims, acc_shape,
            extra=(), extra_specs=(), a_pro=None, epi=None, into=None, t_spec=None, t_shape=None, plan=None):
    nk = grid[2]
    n_extra = len(extra)
    n_out = 1 + (t_spec is not None)
    n_scratch = int(nk > 1)
    n_pi, n_po = (len(plan.operands), len(plan.out_shape)) if plan is not None else (0, 0)

    def body(*refs):
        a_ref, b_ref = refs[0], refs[1]
        ex = refs[2:2 + n_extra]
        p_ins, outs, p_outs, scratch, p_sems = _split_plan_refs(
            refs[2 + n_extra + (into is not None):], n_pi, n_out, n_po, n_scratch)
        o_ref = outs[0]
        i, j, k = pl.program_id(0), pl.program_id(1), pl.program_id(2)

        if plan is not None:
            @pl.when((i == 0) & (j == 0) & (k == 0))
            def _():
                plan.start(p_ins, p_outs, p_sems)

        def product():
            av = a_ref[...]
            if a_pro is not None:
                av = a_pro(av)
            return _dot(av, b_ref[...], dims)

        def finish(r):
            if epi is not None:
                r = epi(r, *[e[...] for e in ex])
            o_ref[...] = r.astype(o_ref.dtype)
            if t_spec is not None:
                outs[1][...] = r.T.astype(outs[1].dtype)

        if nk == 1:
            finish(product())
        else:
            acc_ref = scratch[0]

            @pl.when(k == 0)
            def _():
                acc_ref[...] = jnp.zeros_like(acc_ref)

            acc_ref[...] += product()

            @pl.when(k == nk - 1)
            def _():
                finish(acc_ref[...])

        if plan is not None:
            @pl.when((i == grid[0] - 1) & (j == grid[1] - 1) & (k == nk - 1))
            def _():
                plan.finish(p_ins, p_outs, p_sems)

    operands = [a, b, *extra]
    in_specs = [a_spec, b_spec, *extra_specs]
    aliases = {}
    if into is not None:
        operands.append(into)
        in_specs.append(ANY)
        aliases = {len(operands) - 1: 0}
        out_shape = _sds(into.shape, into.dtype)
    o_specs, out_shapes = [o_spec], [out_shape]
    if t_spec is not None:
        o_specs, out_shapes = o_specs + [t_spec], out_shapes + [t_shape]
    scratch_shapes = [] if nk == 1 else [pltpu.VMEM(acc_shape, F32)]
    if plan is not None:
        operands, in_specs = operands + plan.operands, in_specs + [HBM] * n_pi
        o_specs, out_shapes = o_specs + [HBM] * n_po, out_shapes + plan.out_shape
        scratch_shapes = scratch_shapes + plan.scratch
    res = pl.pallas_call(
        body, name=name, grid=grid, in_specs=in_specs, out_specs=o_specs, out_shape=out_shapes,
        scratch_shapes=scratch_shapes, input_output_aliases=aliases, compiler_params=_cp(),
    )(*operands)
    main = res[0] if n_out == 1 else list(res[:n_out])
    return main if plan is None else (main, list(res[n_out:]))


def _rms_fwd(x, g, name):
    L, D = x.shape
    tm = _tile(L, 256)

    def body(x_ref, g_ref, o_ref, ot_ref):
        xf = x_ref[...]
        rstd = lax.rsqrt(jnp.mean(xf * xf, axis=-1, keepdims=True) + EPS)
        y = xf * rstd * g_ref[...]
        o_ref[...] = y.astype(o_ref.dtype)
        ot_ref[...] = y.T.astype(ot_ref.dtype)

    return pl.pallas_call(
        body, name=name, grid=(L // tm,),
        in_specs=[pl.BlockSpec((tm, D), lambda i: (i, 0)), pl.BlockSpec((1, D), lambda i: (0, 0))],
        out_specs=[pl.BlockSpec((tm, D), lambda i: (i, 0)), pl.BlockSpec((D, tm), lambda i: (0, i))],
        out_shape=[_sds((L, D), BF16), _sds((D, L), BF16)], compiler_params=_cp(),
    )(x, g)


def _rms_bwd(dh, x, g, dres, name):
    L, D = x.shape
    tm = _tile(L, 256)

    def body(dh_ref, x_ref, g_ref, dr_ref, dx_ref, dxb_ref, dg_ref):
        @pl.when(pl.program_id(0) == 0)
        def _():
            dg_ref[...] = jnp.zeros_like(dg_ref)

        xf = x_ref[...]
        dhf = dh_ref[...].astype(F32)
        rstd = lax.rsqrt(jnp.mean(xf * xf, axis=-1, keepdims=True) + EPS)
        xhat = xf * rstd
        dg_ref[...] += jnp.sum(dhf * xhat, axis=0, keepdims=True)
        dxh = dhf * g_ref[...]
        dx = dr_ref[...] + rstd * (dxh - xhat * jnp.mean(dxh * xhat, axis=-1, keepdims=True))
        dx_ref[...] = dx
        dxb_ref[...] = dx.astype(BF16)

    row = pl.BlockSpec((tm, D), lambda i: (i, 0))
    vec = pl.BlockSpec((1, D), lambda i: (0, 0))
    return pl.pallas_call(
        body, name=name, grid=(L // tm,), in_specs=[row, row, vec, row], out_specs=[row, row, vec],
        out_shape=[_sds((L, D), F32), _sds((L, D), BF16), _sds((1, D), F32)], compiler_params=_cp(),
    )(dh, x, g, dres)


def _loss_head(x, g, target, name):
    L, D = x.shape
    tm = _tile(L, 256)

    def body(x_ref, g_ref, t_ref, loss_ref, dx_ref, dxb_ref, dg_ref):
        @pl.when(pl.program_id(0) == 0)
        def _():
            dg_ref[...] = jnp.zeros_like(dg_ref)
            loss_ref[...] = jnp.zeros_like(loss_ref)

        xf = x_ref[...]
        gv = g_ref[...]
        rstd = lax.rsqrt(jnp.mean(xf * xf, axis=-1, keepdims=True) + EPS)
        xhat = xf * rstd
        err = xhat * gv - t_ref[...]
        part = jnp.sum(jnp.sum(err * err, axis=-1, keepdims=True), axis=0, keepdims=True)
        loss_ref[...] += jnp.broadcast_to(part * (0.5 / D), loss_ref.shape)
        dy = err * (1.0 / D)
        dg_ref[...] += jnp.sum(dy * xhat, axis=0, keepdims=True)
        dxh = dy * gv
        dx = rstd * (dxh - xhat * jnp.mean(dxh * xhat, axis=-1, keepdims=True))
        dx_ref[...] = dx
        dxb_ref[...] = dx.astype(BF16)

    row = pl.BlockSpec((tm, D), lambda i: (i, 0))
    vec = pl.BlockSpec((1, D), lambda i: (0, 0))
    lspec = pl.BlockSpec((SUBLANES, LANES), lambda i: (0, 0))
    return pl.pallas_call(
        body, name=name, grid=(L // tm,), in_specs=[row, vec, row], out_specs=[lspec, row, row, vec],
        out_shape=[_sds((SUBLANES, LANES), F32), _sds((L, D), F32), _sds((L, D), BF16), _sds((1, D), F32)],
        compiler_params=_cp(),
    )(x, g, target)


_GELU_C = math.sqrt(2.0 / math.pi)
_GELU_K = 0.044715


def _gelu(x):
    return 0.5 * x * (1.0 + jnp.tanh(_GELU_C * (x + _GELU_K * (x * x * x))))


def _gelu_and_grad(x):
    t = jnp.tanh(_GELU_C * (x + _GELU_K * (x * x * x)))
    val = 0.5 * x * (1.0 + t)
    grad = 0.5 * (1.0 + t) + 0.5 * x * (1.0 - t * t) * (_GELU_C * (1.0 + 3.0 * _GELU_K * (x * x)))
    return val, grad


def _sigmoid(x):
    e = jnp.exp(-jnp.abs(x))
    return jnp.where(x >= 0, 1.0, e) / (1.0 + e)


def _tril_mask():
    r = lax.broadcasted_iota(jnp.int32, (CHUNK, CHUNK), 0)
    c = lax.broadcasted_iota(jnp.int32, (CHUNK, CHUNK), 1)
    return r >= c


def _gmlp_fwd(proj, norm_g, w_s, b_col, mix, name):
    L = proj.shape[0]
    groups = mix // CHUNK
    tt = _tile(L, 2 * CHUNK)

    def body(uv_ref, g_ref, w_ref, b_ref, o_ref):
        z = _gelu(uv_ref[...].astype(F32))
        u, v = z[:, :mix], z[:, mix:]
        vn = v * lax.rsqrt(jnp.mean(v * v, axis=-1, keepdims=True) + EPS) * g_ref[...]
        mask = _tril_mask()
        for gi in range(groups):
            wt = jnp.where(mask, w_ref[gi], 0.0).astype(BF16)
            cols = slice(gi * CHUNK, (gi + 1) * CHUNK)
            for cc in range(tt // CHUNK):
                rows = slice(cc * CHUNK, (cc + 1) * CHUNK)
                mixed = _dot(wt, vn[rows, cols].astype(BF16)) + b_ref[gi]
                o_ref[rows, cols] = (u[rows, cols] * mixed).astype(o_ref.dtype)

    return pl.pallas_call(
        body, name=name, grid=(L // tt,),
        in_specs=[pl.BlockSpec((tt, 2 * mix), lambda i: (i, 0)),
                  pl.BlockSpec((1, mix), lambda i: (0, 0)),
                  pl.BlockSpec((groups, CHUNK, CHUNK), lambda i: (0, 0, 0)),
                  pl.BlockSpec((groups, CHUNK, 1), lambda i: (0, 0, 0))],
        out_specs=pl.BlockSpec((tt, mix), lambda i: (i, 0)),
        out_shape=_sds((L, mix), BF16), compiler_params=_cp(),
    )(proj, norm_g, w_s, b_col)


def _gmlp_bwd(proj, norm_g, w_s, w_st, b_col, dy, mix, name):
    L = proj.shape[0]
    groups = mix // CHUNK
    tt = _tile(L, 2 * CHUNK)

    def body(uv_ref, g_ref, w_ref, wt_ref, b_ref, dy_ref, duv_ref, dg_ref, dw_ref, db_ref, du_s, dvn_s):
        @pl.when(pl.program_id(0) == 0)
        def _():
            dg_ref[...] = jnp.zeros_like(dg_ref)
            dw_ref[...] = jnp.zeros_like(dw_ref)
            db_ref[...] = jnp.zeros_like(db_ref)

        z, zgrad = _gelu_and_grad(uv_ref[...].astype(F32))
        u, v = z[:, :mix], z[:, mix:]
        rstd = lax.rsqrt(jnp.mean(v * v, axis=-1, keepdims=True) + EPS)
        vhat = v * rstd
        gv = g_ref[...]
        vn = vhat * gv
        dyf = dy_ref[...].astype(F32)
        mask = _tril_mask()
        r = lax.broadcasted_iota(jnp.int32, (CHUNK, CHUNK), 0)
        c = lax.broadcasted_iota(jnp.int32, (CHUNK, CHUNK), 1)
        for gi in range(groups):
            w_low = jnp.where(mask, w_ref[gi], 0.0).astype(BF16)
            w_up = jnp.where(r <= c, wt_ref[gi], 0.0).astype(BF16)
            cols = slice(gi * CHUNK, (gi + 1) * CHUNK)
            for cc in range(tt // CHUNK):
                rows = slice(cc * CHUNK, (cc + 1) * CHUNK)
                vnb = vn[rows, cols].astype(BF16)
                mixed = _dot(w_low, vnb) + b_ref[gi]
                dyb = dyf[rows, cols]
                dm = dyb * u[rows, cols]
                dmb = dm.astype(BF16)
                du_s[rows, cols] = dyb * mixed
                dvn_s[rows, cols] = _dot(w_up, dmb)
                dw_ref[gi] += jnp.where(mask, _dot(dmb, vnb, NT), 0.0)
                db_ref[gi] += jnp.sum(dm, axis=1, keepdims=True)
        dvn = dvn_s[...]
        dg_ref[...] += jnp.sum(dvn * vhat, axis=0, keepdims=True)
        dvh = dvn * gv
        dv = rstd * (dvh - vhat * jnp.mean(dvh * vhat, axis=-1, keepdims=True))
        duv_ref[:, :mix] = (du_s[...] * zgrad[:, :mix]).astype(duv_ref.dtype)
        duv_ref[:, mix:] = (dv * zgrad[:, mix:]).astype(duv_ref.dtype)

    wspec = pl.BlockSpec((groups, CHUNK, CHUNK), lambda i: (0, 0, 0))
    bspec = pl.BlockSpec((groups, CHUNK, 1), lambda i: (0, 0, 0))
    gspec = pl.BlockSpec((1, mix), lambda i: (0, 0))
    return pl.pallas_call(
        body, name=name, grid=(L // tt,),
        in_specs=[pl.BlockSpec((tt, 2 * mix), lambda i: (i, 0)), gspec, wspec, wspec, bspec,
                  pl.BlockSpec((tt, mix), lambda i: (i, 0))],
        out_specs=[pl.BlockSpec((tt, 2 * mix), lambda i: (i, 0)), gspec, wspec, bspec],
        out_shape=[_sds((L, 2 * mix), BF16), _sds((1, mix), F32),
                   _sds((groups, CHUNK, CHUNK), F32), _sds((groups, CHUNK, 1), F32)],
        scratch_shapes=[pltpu.VMEM((tt, mix), F32), pltpu.VMEM((tt, mix), F32)],
        compiler_params=_cp(),
    )(proj, norm_g, w_s, w_st, b_col, dy)


S5_CB = 512
S5_UB = 128
S5_LEVELS = (1, 2, 4)


def _s5_fwd_consts(ar, ai):
    rows = lax.broadcasted_iota(jnp.int32, ar.shape, 0)
    out = []
    for d in S5_LEVELS:
        m = rows >= d
        out.append((jnp.where(m, ar, 0.0), jnp.where(m, ai, 0.0)))
        ar, ai = ar * ar - ai * ai, 2.0 * ar * ai
    return out


def _s5_rev_consts(ar, ai):
    rows = lax.broadcasted_iota(jnp.int32, ar.shape, 0)
    out = []
    for d in S5_LEVELS:
        m = rows < SUBLANES - d
        out.append((jnp.where(m, ar, 0.0), jnp.where(m, ai, 0.0)))
        ar, ai = ar * ar - ai * ai, 2.0 * ar * ai
    return out


def _scan8(xr, xi, consts, reverse):
    for (cr, ci), d in zip(consts, S5_LEVELS):
        sh = SUBLANES - d if reverse else d
        pr = pltpu.roll(xr, sh, 0)
        pi = pltpu.roll(xi, sh, 0)
        xr, xi = xr + (cr * pr - ci * pi), xi + (cr * pi + ci * pr)
    return xr, xi


def _row_bcast(x, row):
    rows = lax.broadcasted_iota(jnp.int32, x.shape, 0)
    return jnp.broadcast_to(jnp.sum(jnp.where(rows == row, x, 0.0), axis=0, keepdims=True), x.shape)


def _s5_forward_block(u_ref, bre_ref, bim_ref, sr, si, ar, ai, pwr, pwi, carry, tt):
    consts = _s5_fwd_consts(ar, ai)
    u = u_ref[...]
    sr[...] = _dot(u, bre_ref[...])
    si[...] = _dot(u, bim_ref[...])

    def step(r, cs):
        cr, ci = cs
        o = pl.multiple_of(r * SUBLANES, SUBLANES)
        xr, xi = _scan8(sr[pl.ds(o, SUBLANES), :], si[pl.ds(o, SUBLANES), :], consts, False)
        xr, xi = xr + (pwr * cr - pwi * ci), xi + (pwr * ci + pwi * cr)
        sr[pl.ds(o, SUBLANES), :] = xr
        si[pl.ds(o, SUBLANES), :] = xi
        return _row_bcast(xr, SUBLANES - 1), _row_bcast(xi, SUBLANES - 1)

    return lax.fori_loop(0, tt // SUBLANES, step, carry)


def _s5_powers(ar, ai):
    rows = lax.broadcasted_iota(jnp.int32, ar.shape, 0)
    return _scan8(jnp.where(rows == 0, ar, 0.0), jnp.where(rows == 0, ai, 0.0), _s5_fwd_consts(ar, ai), False)


def _s5_scan_fwd(proj, a_re, a_im, b_re, b_im, c_re, c_im, mix, name):
    L = proj.shape[0]
    S = a_re.shape[1]
    nj = mix // S5_UB
    tt = _tile(L, 256)
    ni = L // tt
    ucol = 2 * mix // S5_UB

    def body(u_ref, ar_ref, ai_ref, bre_ref, bim_ref, cre_ref, cim_ref, y_ref, sr, si, car, cai):
        ar = jnp.broadcast_to(ar_ref[...], (SUBLANES, S5_CB))
        ai = jnp.broadcast_to(ai_ref[...], (SUBLANES, S5_CB))

        @pl.when(pl.program_id(1) == 0)
        def _():
            car[...] = jnp.zeros_like(car)
            cai[...] = jnp.zeros_like(cai)

        pwr, pwi = _s5_powers(ar, ai)
        cr, ci = _s5_forward_block(u_ref, bre_ref, bim_ref, sr, si, ar, ai, pwr, pwi,
                                   (car[...], cai[...]), tt)
        car[...] = cr
        cai[...] = ci
        y_ref[...] = _dot(sr[...].astype(BF16), cre_ref[...]) - _dot(si[...].astype(BF16), cim_ref[...])

    avec = pl.BlockSpec((1, S5_CB), lambda j, i: (0, j))
    bspec = pl.BlockSpec((S5_UB, S5_CB), lambda j, i: (j, 0))
    cspec = pl.BlockSpec((S5_CB, S5_UB), lambda j, i: (j, 0))
    states = pl.BlockSpec((tt, S5_CB), lambda j, i: (i, j))
    return pl.pallas_call(
        body, name=name, grid=(nj, ni),
        in_specs=[pl.BlockSpec((tt, S5_UB), lambda j, i: (i, ucol + j)), avec, avec, bspec, bspec, cspec, cspec],
        out_specs=[pl.BlockSpec((tt, S5_UB), lambda j, i: (i, j)), states, states],
        out_shape=[_sds((L, mix), F32), _sds((L, S), F32), _sds((L, S), F32)],
        scratch_shapes=[pltpu.VMEM((SUBLANES, S5_CB), F32), pltpu.VMEM((SUBLANES, S5_CB), F32)],
        compiler_params=_cp(),
    )(proj, a_re, a_im, b_re, b_im, c_re, c_im)


def _s5_scan_bwd(proj, a_re, a_im, b_re, b_im, c_re, c_im, s_re, s_im, dy, dxin, mix, name):
    L = proj.shape[0]
    S = a_re.shape[1]
    nj = mix // S5_UB
    tt = _tile(L, 256)
    ni = L // tt
    ucol = 2 * mix // S5_UB
    nb = tt // SUBLANES

    def body(u_ref, ar_ref, ai_ref, bre_ref, bim_ref, cre_ref, cim_ref, sr, si, dy_ref, dx_ref,
             du_ref, dbr_ref, dbi_ref, dcr_ref, dci_ref, dar_ref, dai_ref,
             gr, gi, car, cai, acr, aci):
        first = pl.program_id(1) == 0
        ar = jnp.broadcast_to(ar_ref[...], (SUBLANES, S5_CB))
        ai = jnp.broadcast_to(ai_ref[...], (SUBLANES, S5_CB))

        @pl.when(first)
        def _():
            for ref in (car, cai, acr, aci, dbr_ref, dbi_ref, dcr_ref, dci_ref):
                ref[...] = jnp.zeros_like(ref)

        dyb = dy_ref[...].astype(BF16)
        gr[...] = _dot(dyb, cre_ref[...], NT)
        gi[...] = -_dot(dyb, cim_ref[...], NT)

        nai = -ai
        consts = _s5_rev_consts(ar, nai)
        rows = lax.broadcasted_iota(jnp.int32, ar.shape, 0)
        last = rows == SUBLANES - 1
        qr, qi = _scan8(jnp.where(last, ar, 0.0), jnp.where(last, nai, 0.0), consts, True)

        def step(k, cs):
            cr, ci, dr, di = cs
            o = pl.multiple_of((nb - 1 - k) * SUBLANES, SUBLANES)
            xr, xi = _scan8(gr[pl.ds(o, SUBLANES), :], gi[pl.ds(o, SUBLANES), :], consts, True)
            xr, xi = xr + (qr * cr - qi * ci), xi + (qr * ci + qi * cr)
            gr[pl.ds(o, SUBLANES), :] = xr
            gi[pl.ds(o, SUBLANES), :] = xi
            hr = jnp.where(last, cr, pltpu.roll(xr, SUBLANES - 1, 0))
            hi = jnp.where(last, ci, pltpu.roll(xi, SUBLANES - 1, 0))
            s_r = sr[pl.ds(o, SUBLANES), :]
            s_i = si[pl.ds(o, SUBLANES), :]
            dr = dr + (hr * s_r + hi * s_i)
            di = di + (hi * s_r - hr * s_i)
            return _row_bcast(xr, 0), _row_bcast(xi, 0), dr, di

        cr, ci, dr, di = lax.fori_loop(0, nb, step, (car[...], cai[...], acr[...], aci[...]))
        car[...] = cr
        cai[...] = ci
        acr[...] = dr
        aci[...] = di
        dar_ref[...] = jnp.sum(dr, axis=0, keepdims=True)
        dai_ref[...] = jnp.sum(di, axis=0, keepdims=True)

        u = u_ref[...]
        gbr = gr[...].astype(BF16)
        gbi = gi[...].astype(BF16)
        dbr_ref[...] += _dot(u, gbr, TN)
        dbi_ref[...] += _dot(u, gbi, TN)
        du = _dot(gbr, bre_ref[...], NT) + _dot(gbi, bim_ref[...], NT)
        du_ref[...] = (du + dx_ref[...].astype(F32)).astype(du_ref.dtype)
        dyf = dy_ref[...].astype(BF16)
        dcr_ref[...] += _dot(sr[...].astype(BF16), dyf, TN)
        dci_ref[...] -= _dot(si[...].astype(BF16), dyf, TN)

    rev = lambda i: ni - 1 - i
    avec = pl.BlockSpec((1, S5_CB), lambda j, i: (0, j))
    bspec = pl.BlockSpec((S5_UB, S5_CB), lambda j, i: (j, 0))
    cspec = pl.BlockSpec((S5_CB, S5_UB), lambda j, i: (j, 0))
    states = pl.BlockSpec((tt, S5_CB), lambda j, i: (rev(i), j))
    tile = pl.BlockSpec((tt, S5_UB), lambda j, i: (rev(i), j))
    return pl.pallas_call(
        body, name=name, grid=(nj, ni),
        in_specs=[pl.BlockSpec((tt, S5_UB), lambda j, i: (rev(i), ucol + j)), avec, avec, bspec, bspec,
                  cspec, cspec, states, states, tile, tile],
        out_specs=[tile, bspec, bspec, cspec, cspec, avec, avec],
        out_shape=[_sds((L, mix), BF16), _sds((mix, S5_CB), F32), _sds((mix, S5_CB), F32),
                   _sds((S, S5_UB), F32), _sds((S, S5_UB), F32), _sds((1, S), F32), _sds((1, S), F32)],
        scratch_shapes=[pltpu.VMEM((tt, S5_CB), F32)] * 2 + [pltpu.VMEM((SUBLANES, S5_CB), F32)] * 4,
        compiler_params=_cp(),
    )(proj, a_re, a_im, b_re, b_im, c_re, c_im, s_re, s_im, dy, dxin)


def _s5_glu_fwd(ypre, proj, d, w_glu, b_glu, layer, mix, name):
    L = ypre.shape[0]
    tm = _tile(L, 512)

    def body(y_ref, x_ref, d_ref, w_ref, b_ref, o_ref):
        g = _gelu(y_ref[...] + d_ref[...] * x_ref[...].astype(F32))
        z = _dot(g.astype(BF16), w_ref[...]) + b_ref[...]
        o_ref[...] = (g * _sigmoid(z)).astype(o_ref.dtype)

    row = pl.BlockSpec((tm, mix), lambda i: (i, 0))
    vec = pl.BlockSpec((1, mix), lambda i: (0, 0))
    return pl.pallas_call(
        body, name=name, grid=(L // tm,),
        in_specs=[row, pl.BlockSpec((tm, mix), lambda i: (i, 2)), vec,
                  pl.BlockSpec((None, mix, mix), lambda i: (layer, 0, 0)), vec],
        out_specs=row, out_shape=_sds((L, mix), BF16), compiler_params=_cp(),
    )(ypre, proj, d, w_glu, b_glu)


def _s5_glu_bwd(ypre, proj, d, w_glu, b_glu, dout, layer, mix, name):
    L = ypre.shape[0]
    tm = _tile(L, 512)

    def body(y_ref, x_ref, d_ref, w_ref, b_ref, do_ref, dy_ref, dx_ref, dw_ref, db_ref, dd_ref):
        @pl.when(pl.program_id(0) == 0)
        def _():
            dw_ref[...] = jnp.zeros_like(dw_ref)
            db_ref[...] = jnp.zeros_like(db_ref)
            dd_ref[...] = jnp.zeros_like(dd_ref)

        xin = x_ref[...].astype(F32)
        dv = d_ref[...]
        g, ggrad = _gelu_and_grad(y_ref[...] + dv * xin)
        gb = g.astype(BF16)
        w = w_ref[...]
        sg = _sigmoid(_dot(gb, w) + b_ref[...])
        do = do_ref[...].astype(F32)
        dz = do * g * sg * (1.0 - sg)
        dzb = dz.astype(BF16)
        dg = do * sg + _dot(dzb, w, NT)
        dw_ref[...] += _dot(gb, dzb, TN)
        db_ref[...] += jnp.sum(dz, axis=0, keepdims=True)
        dyv = dg * ggrad
        dd_ref[...] += jnp.sum(dyv * xin, axis=0, keepdims=True)
        dy_ref[...] = dyv.astype(dy_ref.dtype)
        dx_ref[...] = (dyv * dv).astype(dx_ref.dtype)

    row = pl.BlockSpec((tm, mix), lambda i: (i, 0))
    vec = pl.BlockSpec((1, mix), lambda i: (0, 0))
    mat = pl.BlockSpec((mix, mix), lambda i: (0, 0))
    return pl.pallas_call(
        body, name=name, grid=(L // tm,),
        in_specs=[row, pl.BlockSpec((tm, mix), lambda i: (i, 2)), vec,
                  pl.BlockSpec((None, mix, mix), lambda i: (layer, 0, 0)), vec, row],
        out_specs=[row, row, mat, vec, vec],
        out_shape=[_sds((L, mix), BF16), _sds((L, mix), BF16), _sds((mix, mix), F32),
                   _sds((1, mix), F32), _sds((1, mix), F32)],
        compiler_params=_cp(),
    )(ypre, proj, d, w_glu, b_glu, dout)


def _after_matrix(t):
    j = lax.broadcasted_iota(jnp.int32, (t, t), 0)
    s = lax.broadcasted_iota(jnp.int32, (t, t), 1)
    return jnp.where(j > s, 1.0, 0.0).astype(BF16)


def _suffix_sum(x, m_after):
    hi = x.astype(BF16)
    lo = (x - hi.astype(F32)).astype(BF16)
    return _dot(hi, m_after) + _dot(lo, m_after)


LOG2E = 1.4426950408889634


def _sb_block(q, k, diagonal):
    z2 = _dot(q, k, NT) * (HEAD_DIM ** -0.5 * LOG2E)
    e = jnp.exp2(-jnp.abs(z2))
    sp2 = jnp.maximum(z2, 0.0) + jnp.log(1.0 + e) * LOG2E
    if not diagonal:
        return z2, sp2, e, -sp2, None
    mask = lax.broadcasted_iota(jnp.int32, z2.shape, 1) < lax.broadcasted_iota(jnp.int32, z2.shape, 0)
    return z2, sp2, e, jnp.where(mask, -sp2, 0.0), mask


def _pair_tables(nt):
    qs = [i for i in range(nt) for _ in range(i + 1)]
    ks = [i - j for i in range(nt) for j in range(i + 1)]
    return jnp.asarray(qs, jnp.int32), jnp.asarray(ks, jnp.int32)


def _split_plan_refs(rest, n_pi, n_out, n_po, n_scratch):
    a = n_pi + n_out
    b = a + n_po
    return rest[:n_pi], rest[n_pi:a], rest[a:b], rest[b:b + n_scratch], rest[b + n_scratch:]


def _sb_fwd(proj, m_after, mix, name, plan=None):
    L = proj.shape[0]
    heads = mix // HEAD_DIM
    T = m_after.shape[0]
    nt = L // T
    npairs = nt * (nt + 1) // 2
    qc, kc, vc = (3 * mix // HEAD_DIM, 4 * mix // HEAD_DIM, 5 * mix // HEAD_DIM)
    n_pi, n_po = (len(plan.operands), len(plan.out_shape)) if plan is not None else (0, 0)

    def body(qt, kt, q_ref, k_ref, v_ref, m_ref, *rest):
        p_ins, (o_ref, o32_ref), p_outs, (acc, ra), p_sems = _split_plan_refs(rest, n_pi, 2, n_po, 2)
        h, p = pl.program_id(0), pl.program_id(1)
        qi, ki = qt[p], kt[p]

        if plan is not None:
            @pl.when((h == 0) & (p == 0))
            def _():
                plan.start(p_ins, p_outs, p_sems)

        def block(diagonal):
            z2, sp2, _, lg2, mask = _sb_block(q_ref[...], k_ref[...], diagonal)
            after = _suffix_sum(lg2, m_ref[...])
            if not diagonal:
                after = after + ra[...]
            w = jnp.exp2(z2 - sp2 + after)
            if diagonal:
                w = jnp.where(mask, w, 0.0)
            pv = _dot(w.astype(BF16), v_ref[...])
            rs = jnp.sum(lg2, axis=1, keepdims=True)
            if diagonal:
                acc[...] = pv
                ra[...] = rs
            else:
                acc[...] += pv
                ra[...] += rs

        pl.when(ki == qi)(functools.partial(block, True))
        pl.when(ki != qi)(functools.partial(block, False))

        @pl.when(ki == 0)
        def _():
            o_ref[...] = acc[...].astype(o_ref.dtype)
            o32_ref[...] = acc[...]

        if plan is not None:
            @pl.when((h == heads - 1) & (p == npairs - 1))
            def _():
                plan.finish(p_ins, p_outs, p_sems)

    qtab, ktab = _pair_tables(nt)
    kv = lambda col: pl.BlockSpec((T, HEAD_DIM), lambda h, p, qt, kt: (kt[p], col + h))
    qo = pl.BlockSpec((T, HEAD_DIM), lambda h, p, qt, kt: (qt[p], h))
    outs = pl.pallas_call(
        body, name=name,
        grid_spec=pltpu.PrefetchScalarGridSpec(
            num_scalar_prefetch=2, grid=(heads, npairs),
            in_specs=[pl.BlockSpec((T, HEAD_DIM), lambda h, p, qt, kt: (qt[p], qc + h)), kv(kc), kv(vc),
                      pl.BlockSpec((T, T), lambda h, p, qt, kt: (0, 0))] + [HBM] * n_pi,
            out_specs=[qo, qo] + [HBM] * n_po,
            scratch_shapes=[pltpu.VMEM((T, HEAD_DIM), F32), pltpu.VMEM((T, 1), F32)]
            + (plan.scratch if plan is not None else [])),
        out_shape=[_sds((L, mix), BF16), _sds((L, mix), F32)] + (plan.out_shape if plan is not None else []),
        compiler_params=_cp(),
    )(qtab, ktab, proj, proj, proj, m_after, *(plan.operands if plan is not None else []))
    return outs[0], outs[1], list(outs[2:])


def _sb_bwd(proj, m_after, out, dout, mix, name, plan=None):
    L = proj.shape[0]
    heads = mix // HEAD_DIM
    T = m_after.shape[0]
    nt = L // T
    npairs = nt * (nt + 1) // 2
    qc, kc, vc = (3 * mix // HEAD_DIM, 4 * mix // HEAD_DIM, 5 * mix // HEAD_DIM)
    scale = HEAD_DIM ** -0.5
    n_pi, n_po = (len(plan.operands), len(plan.out_shape)) if plan is not None else (0, 0)

    def body(qt, kt, q_ref, k_ref, v_ref, m_ref, o_ref, do_ref, *rest):
        p_ins, (dq_ref, dk_ref, dv_ref), p_outs, (dq_acc, ra, rp, delta), p_sems = _split_plan_refs(
            rest, n_pi, 3, n_po, 4)
        h, p = pl.program_id(0), pl.program_id(1)
        qi, ki = qt[p], kt[p]

        if plan is not None:
            @pl.when((h == 0) & (p == 0))
            def _():
                plan.start(p_ins, p_outs, p_sems)

        @pl.when(p == 0)
        def _():
            dk_ref[...] = jnp.zeros_like(dk_ref)
            dv_ref[...] = jnp.zeros_like(dv_ref)

        def block(diagonal):
            q, k, v, do = q_ref[...], k_ref[...], v_ref[...], do_ref[...]
            z2, sp2, e, lg2, mask = _sb_block(q, k, diagonal)
            after = _suffix_sum(lg2, m_ref[...])
            if diagonal:
                dl = jnp.sum(do.astype(F32) * o_ref[...], axis=1, keepdims=True)
                delta[...] = dl
            else:
                dl = delta[...]
                after = after + ra[...]
            w = jnp.exp2(z2 - sp2 + after)
            if diagonal:
                w = jnp.where(mask, w, 0.0)
            wb = w.astype(BF16)
            pm = wb.astype(F32) * _dot(do, v, NT)
            suffix = _suffix_sum(pm, m_ref[...])
            if not diagonal:
                suffix = suffix + rp[...]
            before = dl - suffix - pm
            beta = jnp.where(z2 >= 0, 1.0, e) / (1.0 + e)
            dz = (pm * (1.0 - beta) - beta * before) * scale
            if diagonal:
                dz = jnp.where(mask, dz, 0.0)
            dzb = dz.astype(BF16)
            rows = pl.ds(pl.multiple_of(ki * T, T), T)
            dk_ref[rows, :] += _dot(dzb, q, TN)
            dv_ref[rows, :] += _dot(wb, do, TN)
            dq = _dot(dzb, k)
            rs_a = jnp.sum(lg2, axis=1, keepdims=True)
            rs_p = jnp.sum(pm, axis=1, keepdims=True)
            if diagonal:
                dq_acc[...] = dq
                ra[...] = rs_a
                rp[...] = rs_p
            else:
                dq_acc[...] += dq
                ra[...] += rs_a
                rp[...] += rs_p

        pl.when(ki == qi)(functools.partial(block, True))
        pl.when(ki != qi)(functools.partial(block, False))

        @pl.when(ki == 0)
        def _():
            dq_ref[...] = dq_acc[...]

        if plan is not None:
            @pl.when((h == heads - 1) & (p == npairs - 1))
            def _():
                plan.finish(p_ins, p_outs, p_sems)

    qtab, ktab = _pair_tables(nt)
    kv = lambda col: pl.BlockSpec((T, HEAD_DIM), lambda h, p, qt, kt: (kt[p], col + h))
    qo = pl.BlockSpec((T, HEAD_DIM), lambda h, p, qt, kt: (qt[p], h))
    whole = pl.BlockSpec((L, HEAD_DIM), lambda h, p, qt, kt: (0, h))
    outs = pl.pallas_call(
        body, name=name,
        grid_spec=pltpu.PrefetchScalarGridSpec(
            num_scalar_prefetch=2, grid=(heads, npairs),
            in_specs=[pl.BlockSpec((T, HEAD_DIM), lambda h, p, qt, kt: (qt[p], qc + h)), kv(kc), kv(vc),
                      pl.BlockSpec((T, T), lambda h, p, qt, kt: (0, 0)), qo, qo] + [HBM] * n_pi,
            out_specs=[qo, whole, whole] + [HBM] * n_po,
            scratch_shapes=[pltpu.VMEM((T, HEAD_DIM), F32), pltpu.VMEM((T, 1), F32), pltpu.VMEM((T, 1), F32),
                            pltpu.VMEM((T, 1), F32)] + (plan.scratch if plan is not None else [])),
        out_shape=[_sds((L, mix), F32)] * 3 + (plan.out_shape if plan is not None else []),
        compiler_params=_cp(),
    )(qtab, ktab, proj, proj, proj, m_after, out, dout, *(plan.operands if plan is not None else []))
    return outs[0], outs[1], outs[2], list(outs[3:])


def _merge_fwd(ys, w_branch, proj, b_gate, layer, b_layer, name):
    L, mix = ys[0].shape
    bw = w_branch.shape[-1]
    D = bw * NDEV
    tm = _tile(L, 512)
    gc = 6 * mix // bw

    def body(ya, yb, yc, w_ref, pa, pb, pc, b_ref, o_ref, ot_ref):
        acc = None
        for n, (y_ref, p_ref) in enumerate(((ya, pa), (yb, pb), (yc, pc))):
            gate = _sigmoid(p_ref[...].astype(F32) + b_ref[n:n + 1, :])
            term = gate * _dot(y_ref[...], w_ref[n])
            acc = term if acc is None else acc + term
        o_ref[...] = acc.astype(o_ref.dtype)
        ot_ref[...] = acc.T.astype(ot_ref.dtype)

    yspec = pl.BlockSpec((tm, mix), lambda i, j: (i, 0))
    pspec = [pl.BlockSpec((tm, bw), functools.partial(lambda i, j, n: (i, gc + n * NDEV + j), n=n)) for n in range(3)]
    return pl.pallas_call(
        body, name=name, grid=(L // tm, NDEV),
        in_specs=[yspec, yspec, yspec,
                  pl.BlockSpec((None, None, 3, mix, bw), lambda i, j: (layer, j, 0, 0, 0)),
                  *pspec, pl.BlockSpec((None, None, 3, bw), lambda i, j: (b_layer, j, 0, 0))],
        out_specs=[pl.BlockSpec((tm, bw), lambda i, j: (i, j)), pl.BlockSpec((bw, tm), lambda i, j: (j, i))],
        out_shape=[_sds((L, D), BF16), _sds((D, L), BF16)], compiler_params=_cp(),
    )(*ys, w_branch, proj, proj, proj, b_gate)


def _merge_bwd(ys, w_branch, proj, b_gate, dmerged, layer, b_layer, name):
    L, mix = ys[0].shape
    bw = w_branch.shape[-1]
    D = bw * NDEV
    tm = _tile(L, 512)
    gc = 6 * mix // bw

    def body(ya, yb, yc, w_ref, pa, pb, pc, b_ref, dm_ref,
             dpa, dpb, dpc, dba, dbb, dbc, dya, dyb, dyc, dbg_ref, acc):
        i, j = pl.program_id(0), pl.program_id(1)

        @pl.when((i == 0) & (j == 0))
        def _():
            dbg_ref[...] = jnp.zeros_like(dbg_ref)

        @pl.when(j == 0)
        def _():
            acc[...] = jnp.zeros_like(acc)

        dm = dm_ref[...].astype(F32)
        for n, (y_ref, p_ref, dp_ref, db_ref) in enumerate(((ya, pa, dpa, dba), (yb, pb, dpb, dbb), (yc, pc, dpc, dbc))):
            gate = _sigmoid(p_ref[...].astype(F32) + b_ref[n:n + 1, :])
            br = _dot(y_ref[...], w_ref[n])
            dp = dm * br * gate * (1.0 - gate)
            dp_ref[...] = dp.astype(dp_ref.dtype)
            dbg_ref[j, n:n + 1, :] += jnp.sum(dp, axis=0, keepdims=True)
            dbr = (dm * gate).astype(BF16)
            db_ref[...] = dbr
            acc[n] += _dot(dbr, w_ref[n], NT)

        @pl.when(j == NDEV - 1)
        def _():
            for n, dy_ref in enumerate((dya, dyb, dyc)):
                dy_ref[...] = acc[n].astype(dy_ref.dtype)

    yspec = pl.BlockSpec((tm, mix), lambda i, j: (i, 0))
    ospec = pl.BlockSpec((tm, bw), lambda i, j: (i, j))
    pspec = [pl.BlockSpec((tm, bw), functools.partial(lambda i, j, n: (i, gc + n * NDEV + j), n=n)) for n in range(3)]
    return pl.pallas_call(
        body, name=name, grid=(L // tm, NDEV),
        in_specs=[yspec, yspec, yspec,
                  pl.BlockSpec((None, None, 3, mix, bw), lambda i, j: (layer, j, 0, 0, 0)),
                  *pspec, pl.BlockSpec((None, None, 3, bw), lambda i, j: (b_layer, j, 0, 0)), ospec],
        out_specs=[ospec] * 6 + [yspec] * 3 + [pl.BlockSpec((NDEV, 3, bw), lambda i, j: (0, 0, 0))],
        out_shape=[_sds((L, D), BF16)] * 6 + [_sds((L, mix), BF16)] * 3 + [_sds((NDEV, 3, bw), F32)],
        scratch_shapes=[pltpu.VMEM((3, tm, mix), F32)],
        compiler_params=_cp(),
    )(*ys, w_branch, proj, proj, proj, b_gate, dmerged)


def _ktile(n):
    for t in (1024, 768, 512, 384, 256, 128, 64, 32, 16, 8):
        if n % t == 0:
            return t
    return n


MM_ROWS = 1024
MM_COLS = 1024
MM_DEPTH = 2048


def _mm_cols(name, a, wg, layer, out_dtype, epi=None, with_transpose=False, plan=None):
    M, K = a.shape
    nb = wg.shape[3]
    tm, tn = _tile(M, MM_ROWS), _tile(nb, MM_COLS, LANES)
    r = nb // tn
    t_out = dict(t_spec=pl.BlockSpec((tn, tm), lambda i, j, k: (j, i)),
                 t_shape=_sds((NDEV * nb, M), out_dtype)) if with_transpose else {}
    return _matmul(
        name, a, wg, grid=(M // tm, NDEV * r, 1),
        a_spec=pl.BlockSpec((tm, K), lambda i, j, k: (i, 0)),
        b_spec=pl.BlockSpec((None, None, K, tn), lambda i, j, k: (layer, j // r, 0, j % r)),
        o_spec=pl.BlockSpec((tm, tn), lambda i, j, k: (i, j)),
        out_shape=_sds((M, NDEV * nb), out_dtype), dims=NN, acc_shape=(tm, tn), epi=epi, plan=plan, **t_out)


def _mm_cols_t(name, a, wg, layer, out_dtype):
    M = a.shape[0]
    K, nb = wg.shape[2], wg.shape[3]
    tm, tn, tk = _tile(M, MM_ROWS), _tile(K, MM_COLS, LANES), _ktile(nb)
    r = nb // tk
    return _matmul(
        name, a, wg, grid=(M // tm, K // tn, NDEV * r),
        a_spec=pl.BlockSpec((tm, tk), lambda i, j, k: (i, k)),
        b_spec=pl.BlockSpec((None, None, tn, tk), lambda i, j, k: (layer, k // r, j, k % r)),
        o_spec=pl.BlockSpec((tm, tn), lambda i, j, k: (i, j)),
        out_shape=_sds((M, K), out_dtype), dims=NT, acc_shape=(tm, tn))


def _mm_rows(name, a, wn, layer, out_dtype, res, a_pro=None):
    M, K = a.shape
    N = wn.shape[2]
    tm, tn, tk = _tile(M, MM_ROWS), _tile(N, MM_COLS // 2, LANES), _tile(K, MM_DEPTH, LANES)
    tile = pl.BlockSpec((tm, tn), lambda i, j, k: (i, j))
    return _matmul(
        name, a, wn, grid=(M // tm, N // tn, K // tk),
        a_spec=pl.BlockSpec((tm, tk), lambda i, j, k: (i, k)),
        b_spec=pl.BlockSpec((None, tk, tn), lambda i, j, k: (layer, k, j)),
        o_spec=tile, out_shape=_sds((M, N), out_dtype), dims=NN, acc_shape=(tm, tn),
        extra=(res,), extra_specs=(tile,), epi=lambda acc, rv: acc + rv, a_pro=a_pro)


def _mm_rows_t(name, a, wn, layer, out_dtype, extra=(), epi=None):
    M, N = a.shape
    K = wn.shape[1]
    tm, tn = _tile(M, MM_ROWS), _tile(K, MM_COLS, LANES)
    tile = pl.BlockSpec((tm, tn), lambda i, j, k: (i, j))
    return _matmul(
        name, a, wn, grid=(M // tm, K // tn, 1),
        a_spec=pl.BlockSpec((tm, N), lambda i, j, k: (i, 0)),
        b_spec=pl.BlockSpec((None, tn, N), lambda i, j, k: (layer, j, 0)),
        o_spec=tile, out_shape=_sds((M, K), out_dtype), dims=NT, acc_shape=(tm, tn),
        extra=extra, extra_specs=(tile,) * len(extra), epi=epi)


def _mm_grad(name, a, dy, *, tokens_last, o_block, o_map, tn, out_shape=None, into=None, a_pro=None):
    K, L = a.shape if tokens_last else a.shape[::-1]
    N = dy.shape[1]
    tm, tt = _tile(K, MM_ROWS), _tile(L, MM_ROWS, LANES)
    a_spec = (pl.BlockSpec((tm, tt), lambda i, j, k: (i, k)) if tokens_last
              else pl.BlockSpec((tt, tm), lambda i, j, k: (k, i)))
    return _matmul(
        name, a, dy, grid=(K // tm, N // tn, L // tt), a_spec=a_spec,
        b_spec=pl.BlockSpec((tt, tn), lambda i, j, k: (k, j)),
        o_spec=pl.BlockSpec(o_block(tm, tn), o_map), out_shape=out_shape,
        dims=NN if tokens_last else TN, acc_shape=(tm, tn), a_pro=a_pro, into=into)


def _grad_cols(name, a_t, dy):
    nb = dy.shape[1] // NDEV
    tn = _tile(nb, MM_COLS, LANES)
    r = nb // tn
    return _mm_grad(name, a_t, dy, tokens_last=True, o_block=lambda tm, t: (None, tm, t),
                    o_map=lambda i, j, k: (j // r, i, j % r), tn=tn,
                    out_shape=_sds((NDEV, a_t.shape[0], nb), BF16))


def _grad_rows(name, a_t, dy, a_pro=None):
    tn = _tile(dy.shape[1], MM_COLS, LANES)
    return _mm_grad(name, a_t, dy, tokens_last=True, o_block=lambda tm, t: (tm, t),
                    o_map=lambda i, j, k: (i, j), tn=tn,
                    out_shape=_sds((a_t.shape[0], dy.shape[1]), BF16), a_pro=a_pro)


def _grad_branch(name, y, dbr, into, n):
    bw = into.shape[3]
    return _mm_grad(name, y, dbr, tokens_last=False, o_block=lambda tm, t: (None, None, tm, t),
                    o_map=lambda i, j, k: (j, n, i, 0), tn=bw, into=into)


def _s5_discretize(lam_re, lam_im, log_dt, b_re, b_im):
    dt = jnp.exp(log_dt)[:, None]
    mag = jnp.exp(lam_re * dt)
    ab_re = mag * jnp.cos(lam_im * dt)
    ab_im = mag * jnp.sin(lam_im * dt)
    den = lam_re * lam_re + lam_im * lam_im
    n_re = ab_re - 1.0
    n_im = ab_im
    k_re = (n_re * lam_re + n_im * lam_im) / den
    k_im = (n_im * lam_re - n_re * lam_im) / den
    bb_re = k_re[..., None] * b_re - k_im[..., None] * b_im
    bb_im = k_re[..., None] * b_im + k_im[..., None] * b_re
    return ab_re, ab_im, bb_re, bb_im


def _s5_b_blocks(bb):
    g, p, h = bb.shape
    t = bb.reshape(g // 8, 8, p, h)
    return jnp.einsum('jiph,ik->jihkp', t, jnp.eye(8, dtype=bb.dtype)).reshape(g * h, 8 * p)


def _s5_b_unblock(m, g, p, h):
    t = m.reshape(g // 8, 8, h, 8, p)
    return jnp.einsum('jihkp,ik->jiph', t, jnp.eye(8, dtype=m.dtype)).reshape(g, p, h)


def _s5_c_blocks(c):
    g, h, p = c.shape
    t = c.reshape(g // 8, 8, h, p)
    return jnp.einsum('jihp,ik->jipkh', t, jnp.eye(8, dtype=c.dtype)).reshape(g * p, 8 * h)


def _s5_c_unblock(m, g, h, p):
    t = m.reshape(g // 8, 8, p, 8, h)
    return jnp.einsum('jipkh,ik->jihp', t, jnp.eye(8, dtype=m.dtype)).reshape(g, h, p)


BIG = ("w_in", "w_branch", "w_out", "w_mlp_in", "w_mlp_out", "s5_w_glu")
SMALL = ("norm1_g", "gm_norm_g", "gm_w_s", "gm_b_s", "s5_lambda_re", "s5_lambda_im", "s5_log_dt",
         "s5_b_re", "s5_b_im", "s5_c_re", "s5_c_im", "s5_d", "s5_b_glu", "norm2_g", "final_g", "b_gate")
WEIGHTS = ("norm1_g", "w_in", "b_gate", "gm_norm_g", "gm_w_s", "gm_b_s", "s5_lambda_re", "s5_lambda_im",
           "s5_log_dt", "s5_b_re", "s5_b_im", "s5_c_re", "s5_c_im", "s5_d", "s5_w_glu", "s5_b_glu",
           "w_branch", "w_out", "norm2_g", "w_mlp_in", "w_mlp_out", "final_g")
FLAT_ROWS = 512


def _pack(arrays):
    flat = jnp.concatenate([a.reshape(-1) for a in arrays])
    unit = FLAT_ROWS * LANES
    pad = (-flat.shape[0]) % unit
    return jnp.pad(flat, (0, pad)).reshape(-1, LANES)


def _unpack(flat2d, shapes):
    flat = flat2d.reshape(-1)
    out, off = [], 0
    for s in shapes:
        n = math.prod(s)
        out.append(flat[off:off + n].reshape(s))
        off += n
    return out


def kernel(x, norm1_g, w_in, b_gate, gm_norm_g, gm_w_s, gm_b_s, s5_lambda_re, s5_lambda_im, s5_log_dt, s5_b_re, s5_b_im, s5_c_re, s5_c_im, s5_d, s5_w_glu, s5_b_glu, w_branch, w_out, norm2_g, w_mlp_in, w_mlp_out, final_g, loss_target, m_norm1_g, m_w_in, m_b_gate, m_gm_norm_g, m_gm_w_s, m_gm_b_s, m_s5_lambda_re, m_s5_lambda_im, m_s5_log_dt, m_s5_b_re, m_s5_b_im, m_s5_c_re, m_s5_c_im, m_s5_d, m_s5_w_glu, m_s5_b_glu, m_w_branch, m_w_out, m_norm2_g, m_w_mlp_in, m_w_mlp_out, m_final_g, v_norm1_g, v_w_in, v_b_gate, v_gm_norm_g, v_gm_w_s, v_gm_b_s, v_s5_lambda_re, v_s5_lambda_im, v_s5_log_dt, v_s5_b_re, v_s5_b_im, v_s5_c_re, v_s5_c_im, v_s5_d, v_s5_w_glu, v_s5_b_glu, v_w_branch, v_w_out, v_norm2_g, v_w_mlp_in, v_w_mlp_out, v_final_g):
    P = dict(locals())
    W = {n: P[n] for n in WEIGHTS}
    M1 = {n: P["m_" + n] for n in WEIGHTS}
    V2 = {n: P["v_" + n] for n in WEIGHTS}

    L, D = x.shape[1], x.shape[2]
    depth = norm1_g.shape[0]
    mix = D // 2
    nb_in, ffb, bw = w_in.shape[2], w_mlp_in.shape[2], w_branch.shape[3]
    ff = ffb * NDEV
    s5_groups = mix // S5_GROUP_CH
    assert mix % S5_UB == 0 and L % CHUNK == 0 and w_in.shape[2] * NDEV == 6 * mix + 3 * D

    xi, yi, ci = _mesh_pos()
    core = ci.astype(jnp.int32).reshape(1)
    chip = (2 * xi + yi).astype(jnp.int32).reshape(1)
    dev = 4 * xi + 2 * yi + ci

    local16 = {n: P[n].astype(BF16) for n in BIG}
    natural = {"w_in": lambda g: g[None], "w_branch": lambda g: g[None], "w_mlp_in": lambda g: g[None],
               "w_out": lambda g: g.reshape(1, D, D), "w_mlp_out": lambda g: g.reshape(1, ff, D),
               "s5_w_glu": lambda g: g.reshape(1, mix, mix)}
    behind_proj = ("w_branch", "w_out", "s5_w_glu", "w_mlp_in")

    first = _run_plan(_gather_plan([(local16["w_in"], 0), (b_gate, None)]), "gather_first")
    w_in_next = natural["w_in"](first[0])
    bg = jnp.swapaxes(first[1], 0, 1)

    T = _tile(L, 512)
    m_after = _after_matrix(T)

    xcur = x[0]
    saved = []
    for l in range(depth):
        wl = {"w_in": w_in_next}
        h, h_t = _rms_fwd(xcur, norm1_g[l][None], f"rms1_l{l}")
        proj, landed = _mm_cols(f"proj_l{l}", h, wl["w_in"], 0, BF16,
                                plan=_gather_plan([(local16[n], l) for n in behind_proj]))
        wl.update({n: natural[n](g) for n, g in zip(behind_proj, landed)})
        ya = _gmlp_fwd(proj, gm_norm_g[l][None], gm_w_s[l], gm_b_s[l][..., None], mix, f"gmlp_l{l}")
        ab_re, ab_im, bb_re, bb_im = _s5_discretize(s5_lambda_re[l], s5_lambda_im[l], s5_log_dt[l],
                                                    s5_b_re[l], s5_b_im[l])
        s5p = (ab_re.reshape(1, -1), ab_im.reshape(1, -1),
               _s5_b_blocks(bb_re).astype(BF16), _s5_b_blocks(bb_im).astype(BF16),
               _s5_c_blocks(s5_c_re[l]).astype(BF16), _s5_c_blocks(s5_c_im[l]).astype(BF16))
        ypre, sb_re, sb_im = _s5_scan_fwd(proj, *s5p, mix, f"s5scan_l{l}")
        yb = _s5_glu_fwd(ypre, proj, s5_d[l][None], wl["s5_w_glu"], s5_b_glu[l][None], 0, mix, f"s5glu_l{l}")
        ahead = [(local16["w_mlp_out"], l)] + ([(local16["w_in"], l + 1)] if l + 1 < depth else [])
        yc, yc32, landed = _sb_fwd(proj, m_after, mix, f"sb_l{l}", plan=_gather_plan(ahead))
        wl["w_mlp_out"] = natural["w_mlp_out"](landed[0])
        if l + 1 < depth:
            w_in_next = natural["w_in"](landed[1])
        merged, merged_t = _merge_fwd((ya, yb, yc), wl["w_branch"], proj, bg, 0, l, f"merge_l{l}")
        xmid = _mm_rows(f"wout_l{l}", merged, wl["w_out"], 0, F32, xcur)
        h2, h2_t = _rms_fwd(xmid, norm2_g[l][None], f"rms2_l{l}")
        r, r_t = _mm_cols(f"mlpin_l{l}", h2, wl["w_mlp_in"], 0, BF16, epi=lambda acc: jnp.maximum(acc, 0.0),
                          with_transpose=True)
        xout = _mm_rows(f"mlpout_l{l}", r, wl["w_mlp_out"], 0, F32, xmid, a_pro=lambda t: t * t)
        saved.append(dict(x=xcur, h_t=h_t, proj=proj, ys=(ya, yb, yc), s5p=s5p, ypre=ypre, sb=(sb_re, sb_im),
                          yc32=yc32, merged_t=merged_t, xmid=xmid, h2_t=h2_t, r=r, r_t=r_t, w=wl))
        xcur = xout

    loss_tile, dx, dxb, d_final_g = _loss_head(xcur, final_g[None], loss_target[0], "loss_head")
    loss = lax.psum(loss_tile[0, 0], ("x", "y", "c"))

    big_w = {"w_in": (D, nb_in), "w_branch": (3 * mix, bw), "w_out": (D // NDEV, D),
             "w_mlp_in": (D, ffb), "w_mlp_out": (ffb, D), "s5_w_glu": (mix // NDEV, mix)}
    small = {n: [None] * depth for n in SMALL if n != "final_g"}
    pair_sums = [None] * depth
    chip_parts = [None] * depth

    for l in reversed(range(depth)):
        sv = saved[l]
        wl = sv["w"]
        d_a = _mm_rows_t(f"d_act_l{l}", dxb, wl["w_mlp_out"], 0, BF16, extra=(sv["r"],),
                         epi=lambda acc, rv: acc * (2.0 * rv.astype(F32)))
        g_mo = _grad_rows(f"g_mlpout_l{l}", sv["r_t"], dxb, a_pro=lambda t: t * t)
        g_mi = _grad_cols(f"g_mlpin_l{l}", sv["h2_t"], d_a)
        dh2 = _mm_cols_t(f"d_h2_l{l}", d_a, wl["w_mlp_in"], 0, F32)
        dxm, dxmb, small["norm2_g"][l] = _rms_bwd(dh2, sv["xmid"], norm2_g[l][None], dx, f"rms2_bwd_l{l}")
        d_merged = _mm_rows_t(f"d_merged_l{l}", dxmb, wl["w_out"], 0, BF16)
        g_out = _grad_rows(f"g_wout_l{l}", sv["merged_t"], dxmb)
        (dpa, dpb, dpc, dba, dbb, dbc, dya, dyb, dyc, dbg) = _merge_bwd(
            sv["ys"], wl["w_branch"], sv["proj"], bg, d_merged, 0, l, f"merge_bwd_l{l}")
        small["b_gate"][l] = jnp.transpose(dbg, (1, 0, 2)).reshape(3, D)
        g_br = lax.empty((NDEV, 3, mix, bw), BF16)
        for n, dbr in enumerate((dba, dbb, dbc)):
            g_br = _grad_branch(f"g_branch{n}_l{l}", sv["ys"][n], dbr, g_br, n)
        d_uv, d_gn, d_ws, d_bs = _gmlp_bwd(sv["proj"], gm_norm_g[l][None], gm_w_s[l],
                                           jnp.swapaxes(gm_w_s[l], 1, 2), gm_b_s[l][..., None], dya, mix,
                                           f"gmlp_bwd_l{l}")
        small["gm_norm_g"][l], small["gm_w_s"][l], small["gm_b_s"][l] = d_gn, d_ws, d_bs[..., 0]
        d_ypre, d_xin, dw_glu, db_glu, dd = _s5_glu_bwd(sv["ypre"], sv["proj"], s5_d[l][None], wl["s5_w_glu"],
                                                        s5_b_glu[l][None], dyb, 0, mix, f"s5glu_bwd_l{l}")
        g_glu = dw_glu.astype(BF16)
        small["s5_b_glu"][l], small["s5_d"][l] = db_glu, dd
        d_s5in, dbm_re, dbm_im, dcm_re, dcm_im, da_re, da_im = _s5_scan_bwd(
            sv["proj"], *sv["s5p"], *sv["sb"], d_ypre, d_xin, mix, f"s5scan_bwd_l{l}")
        small["s5_c_re"][l] = _s5_c_unblock(dcm_re, s5_groups, S5_GROUP_CH, S5_STATE)
        small["s5_c_im"][l] = _s5_c_unblock(dcm_im, s5_groups, S5_GROUP_CH, S5_STATE)
        _, disc_vjp = jax.vjp(_s5_discretize, s5_lambda_re[l], s5_lambda_im[l], s5_log_dt[l],
                              s5_b_re[l], s5_b_im[l])
        (small["s5_lambda_re"][l], small["s5_lambda_im"][l], small["s5_log_dt"][l],
         small["s5_b_re"][l], small["s5_b_im"][l]) = disc_vjp(
            (da_re.reshape(s5_groups, S5_STATE), da_im.reshape(s5_groups, S5_STATE),
             _s5_b_unblock(dbm_re, s5_groups, S5_STATE, S5_GROUP_CH),
             _s5_b_unblock(dbm_im, s5_groups, S5_STATE, S5_GROUP_CH)))
        above = _chips_plan(pair_sums[l + 1]) if l + 1 < depth else None
        dq, dk, dv, landed = _sb_bwd(sv["proj"], m_after, sv["yc32"], dyc, mix, f"sb_bwd_l{l}", plan=above)
        if above is not None:
            chip_parts[l + 1] = landed
        dproj = jnp.concatenate([d_uv, d_s5in, dq.astype(BF16), dk.astype(BF16), dv.astype(BF16),
                                 dpa, dpb, dpc], axis=1)
        g_in = _grad_cols(f"g_win_l{l}", sv["h_t"], dproj)
        dh = _mm_cols_t(f"d_h_l{l}", dproj, wl["w_in"], 0, F32)
        dx, dxb, small["norm1_g"][l] = _rms_bwd(dh, sv["x"], norm1_g[l][None], dxm, f"rms1_bwd_l{l}")
        parts = {"w_in": g_in, "w_branch": g_br, "w_out": g_out, "w_mlp_in": g_mi, "w_mlp_out": g_mo,
                 "s5_w_glu": g_glu}
        gs = [parts[n].reshape(NDEV, *big_w[n]) for n in BIG]
        r1 = _run_plan(_pair_plan(gs), f"reduce_pair_l{l}")
        pair_sums[l] = [_pair_add(g, rr, core, f"pair_add_{n}_l{l}") for g, rr, n in zip(gs, r1, BIG)]

    grad_x = dx[None]
    chip_parts[0] = _run_plan(_chips_plan(pair_sums[0]), "reduce_chips_l0")

    grads, deltas, new_m, new_v = {}, {}, {}, {}
    for t, n in enumerate(BIG):
        shp = (depth, *big_w[n])
        outs = [lax.empty(shp, F32) for _ in range(4)]
        for l in range(depth):
            outs = _adamw_big(pair_sums[l][t], chip_parts[l][t], chip, W[n].reshape(shp), M1[n].reshape(shp),
                              V2[n].reshape(shp), l, outs, f"adamw_{n}_l{l}")
        grads[n], deltas[n], new_m[n], new_v[n] = (o.reshape(W[n].shape) for o in outs)

    small_full = {n: jnp.stack(v) for n, v in small.items()}
    small_full["final_g"] = d_final_g[0]
    full_shapes = [W[n].shape if n != "b_gate" else (depth, 3, D) for n in SMALL]
    packed = _pack([small_full[n].reshape(s) for n, s in zip(SMALL, full_shapes)])
    gathered = _run_plan(_gather_plan([(packed, None)]), "gather_small")[0]
    summed = _unpack(_sum_devices(gathered, "sum_small"), full_shapes)
    for n, g in zip(SMALL, summed):
        grads[n] = g if n != "b_gate" else lax.dynamic_slice_in_dim(g, dev * bw, bw, axis=2)
    shapes = [W[n].shape for n in SMALL]
    d_s, m_s, v_s = _adamw_flat(_pack([grads[n] for n in SMALL]), _pack([W[n] for n in SMALL]),
                                _pack([M1[n] for n in SMALL]), _pack([V2[n] for n in SMALL]), "adamw_small")
    for n, d, mm, vv in zip(SMALL, _unpack(d_s, shapes), _unpack(m_s, shapes), _unpack(v_s, shapes)):
        deltas[n], new_m[n], new_v[n] = d, mm, vv

    return (loss, grad_x, *[grads[n] for n in WEIGHTS], *[deltas[n] for n in WEIGHTS],
            *[new_m[n] for n in WEIGHTS], *[new_v[n] for n in WEIGHTS])
```

```python
import functools
import math
from typing import Callable, NamedTuple

import jax
import jax.numpy as jnp
from jax import lax
from jax.experimental import pallas as pl
from jax.experimental.pallas import tpu as pltpu

F32 = jnp.float32
BF16 = jnp.bfloat16
MESH = pl.DeviceIdType.MESH
NDEV = 8
NCHIP = 4

EPS = 1e-6
CHUNK = 128
S5_GROUP_CH = 16
S5_STATE = 64
HEAD_DIM = 128
DT_MIN = 1e-3
DT_MAX = 1e-1
ADAM_LR = 0.001
ADAM_B1 = 0.9
ADAM_B2 = 0.999
ADAM_EPS = 1e-08
ADAM_WD = 0.01
ADAM_STEP = 10

V7X_VMEM_BYTES = 64 * 2**20
VMEM_LIMIT_BYTES = V7X_VMEM_BYTES - 8 * 2**20
SUBLANES = 8
LANES = 128

NN = (((1,), (0,)), ((), ()))
NT = (((1,), (1,)), ((), ()))
TN = (((0,), (0,)), ((), ()))

HBM = pl.BlockSpec(memory_space=pltpu.HBM)
ANY = pl.BlockSpec(memory_space=pl.ANY)


def _cp(**kw):
    return pltpu.CompilerParams(vmem_limit_bytes=VMEM_LIMIT_BYTES, **kw)


def _tile(n, pref, align=SUBLANES):
    if n <= pref:
        return n
    t = (pref // align) * align
    while t >= align:
        if n % t == 0:
            return t
        t -= align
    return n


def _dot(a, b, dims=NN):
    return lax.dot_general(a, b, dims, preferred_element_type=F32)


def _sds(shape, dtype):
    return jax.ShapeDtypeStruct(tuple(shape), dtype)


def _mesh_pos():
    return lax.axis_index("x"), lax.axis_index("y"), lax.axis_index("c")


class _Plan(NamedTuple):
    operands: list
    out_shape: list
    scratch: list
    start: Callable
    finish: Callable


def _gather_plan(sources):
    n = len(sources)

    def copies(ins, outs, sems):
        send_sems, recv_sems, local_sems = sems
        x, y, c = _mesh_pos()
        me, sib = (x, y, c), (x, y, 1 - c)
        chips = [(1 - x, y), (x, 1 - y), (1 - x, 1 - y)]

        def src(t):
            layer = sources[t][1]
            return ins[t] if layer is None else ins[t].at[layer]

        def slot(t, p):
            return outs[t].at[4 * p[0] + 2 * p[1] + p[2]]

        def copy(t, k, block, to, own=False):
            dst = slot(t, block)
            return pltpu.make_async_remote_copy(
                src_ref=src(t) if own else dst, dst_ref=dst,
                send_sem=send_sems.at[t, k], recv_sem=recv_sems.at[t, k],
                device_id=to, device_id_type=MESH)

        mine = [pltpu.make_async_copy(src(t), slot(t, me), local_sems.at[t]) for t in range(n)]
        first = []
        for t in range(n):
            first.append(copy(t, 0, me, sib, own=True))
            first += [copy(t, 1 + j, me, (*chip, c), own=True) for j, chip in enumerate(chips)]
        return me, sib, c, chips, copy, mine, first

    def start(ins, outs, sems):
        *_, mine, first = copies(ins, outs, sems)
        for cp in mine + first:
            cp.start()

    def finish(ins, outs, sems):
        me, sib, c, chips, copy, mine, first = copies(ins, outs, sems)
        passed = []
        for j, chip in enumerate(chips):
            for t in range(n):
                copy(t, 1 + j, (*chip, c), me).wait_recv()
                fw = copy(t, 4 + j, (*chip, c), sib)
                fw.start()
                passed.append(fw)
        for t in range(n):
            copy(t, 0, sib, me).wait_recv()
            for j, chip in enumerate(chips):
                copy(t, 4 + j, (*chip, 1 - c), me).wait_recv()
        for cp in first + passed:
            cp.wait_send()
        for cp in mine:
            cp.wait()

    return _Plan(
        operands=[a for a, _ in sources],
        out_shape=[_sds((NDEV,) + (a.shape if layer is None else a.shape[1:]), a.dtype) for a, layer in sources],
        scratch=[pltpu.SemaphoreType.DMA((n, 7)), pltpu.SemaphoreType.DMA((n, 7)), pltpu.SemaphoreType.DMA((n,))],
        start=start, finish=finish)


def _pair_plan(gs):
    n = len(gs)

    def copies(ins, outs, sems):
        send_sems, recv_sems = sems
        x, y, c = _mesh_pos()
        return [pltpu.make_async_remote_copy(
            src_ref=ins[t].at[2 * p + (1 - c)], dst_ref=outs[t].at[p],
            send_sem=send_sems.at[t, p], recv_sem=recv_sems.at[t, p],
            device_id=(x, y, 1 - c), device_id_type=MESH) for t in range(n) for p in range(NCHIP)]

    def start(ins, outs, sems):
        for cp in copies(ins, outs, sems):
            cp.start()

    def finish(ins, outs, sems):
        for cp in copies(ins, outs, sems):
            cp.wait()

    return _Plan(operands=list(gs), out_shape=[_sds((NCHIP,) + g.shape[1:], g.dtype) for g in gs],
                 scratch=[pltpu.SemaphoreType.DMA((n, NCHIP)), pltpu.SemaphoreType.DMA((n, NCHIP))],
                 start=start, finish=finish)


def _chips_plan(ss):
    n = len(ss)

    def copies(ins, outs, sems):
        send_sems, recv_sems = sems
        x, y, c = _mesh_pos()
        chips = [(1 - x, y), (x, 1 - y), (1 - x, 1 - y)]
        return [pltpu.make_async_remote_copy(
            src_ref=ins[t].at[2 * chip[0] + chip[1]], dst_ref=outs[t].at[j],
            send_sem=send_sems.at[t, j], recv_sem=recv_sems.at[t, j],
            device_id=(*chip, c), device_id_type=MESH) for t in range(n) for j, chip in enumerate(chips)]

    def start(ins, outs, sems):
        for cp in copies(ins, outs, sems):
            cp.start()

    def finish(ins, outs, sems):
        for cp in copies(ins, outs, sems):
            cp.wait()

    return _Plan(operands=list(ss), out_shape=[_sds((3,) + s.shape[1:], s.dtype) for s in ss],
                 scratch=[pltpu.SemaphoreType.DMA((n, 3)), pltpu.SemaphoreType.DMA((n, 3))],
                 start=start, finish=finish)


def _run_plan(plan, name):
    n_in, n_out = len(plan.operands), len(plan.out_shape)

    def body(*refs):
        ins, outs, sems = refs[:n_in], refs[n_in:n_in + n_out], refs[n_in + n_out:]
        plan.start(ins, outs, sems)
        plan.finish(ins, outs, sems)

    return list(pl.pallas_call(
        body, name=name, out_shape=plan.out_shape, in_specs=[HBM] * n_in, out_specs=[HBM] * n_out,
        scratch_shapes=plan.scratch)(*plan.operands))


def _rows_tile(r, c, itemsize=4):
    return _tile(r, max(SUBLANES, (2**20 // itemsize) // c))


def _pair_add(g, r1, core, name):
    _, r, c = g.shape
    tr = _rows_tile(r, c)
    g4 = g.reshape(NCHIP, 2, r, c)

    def body(core_ref, g_ref, r_ref, o_ref):
        o_ref[...] = (g_ref[...].astype(F32) + r_ref[...].astype(F32)).astype(o_ref.dtype)

    return pl.pallas_call(
        body, name=name,
        grid_spec=pltpu.PrefetchScalarGridSpec(
            num_scalar_prefetch=1, grid=(NCHIP, r // tr),
            in_specs=[pl.BlockSpec((None, None, tr, c), lambda p, i, cr: (p, cr[0], i, 0)),
                      pl.BlockSpec((None, tr, c), lambda p, i, cr: (p, i, 0))],
            out_specs=pl.BlockSpec((None, tr, c), lambda p, i, cr: (p, i, 0))),
        out_shape=_sds((NCHIP, r, c), BF16),
        compiler_params=_cp(),
    )(core, g4, r1)


def _adamw_math(g, w, m, v):
    m = ADAM_B1 * m + (1.0 - ADAM_B1) * g
    v = ADAM_B2 * v + (1.0 - ADAM_B2) * (g * g)
    m_hat = m / (1.0 - ADAM_B1 ** ADAM_STEP)
    v_hat = v / (1.0 - ADAM_B2 ** ADAM_STEP)
    delta = -ADAM_LR * (m_hat / (jnp.sqrt(v_hat) + ADAM_EPS) + ADAM_WD * w)
    return delta, m, v


def _adamw_big(s, r2, chip, w, m, v, layer, into, name):
    _, r, c = w.shape
    tr = _tile(r, max(SUBLANES, _rows_tile(r, c) // 2))

    def body(chip_ref, s_ref, ra_ref, rb_ref, rc_ref, w_ref, m_ref, v_ref, *rest):
        g_out, d_out, m_out, v_out = rest[4:]
        g = ((s_ref[...].astype(F32) + ra_ref[...].astype(F32)) + rb_ref[...].astype(F32)) + rc_ref[...].astype(F32)
        d, mm, vv = _adamw_math(g, w_ref[...], m_ref[...], v_ref[...])
        g_out[...] = g
        d_out[...] = d
        m_out[...] = mm
        v_out[...] = vv

    wspec = pl.BlockSpec((None, tr, c), lambda i, cr: (layer, i, 0))
    rspec = [pl.BlockSpec((None, tr, c), functools.partial(lambda i, cr, j: (j, i, 0), j=j)) for j in range(3)]
    return pl.pallas_call(
        body, name=name,
        grid_spec=pltpu.PrefetchScalarGridSpec(
            num_scalar_prefetch=1, grid=(r // tr,),
            in_specs=[pl.BlockSpec((None, tr, c), lambda i, cr: (cr[0], i, 0)),
                      *rspec, wspec, wspec, wspec, ANY, ANY, ANY, ANY],
            out_specs=[wspec] * 4),
        out_shape=[_sds(w.shape, F32)] * 4,
        input_output_aliases={8: 0, 9: 1, 10: 2, 11: 3},
        compiler_params=_cp(),
    )(chip, s, r2, r2, r2, w, m, v, *into)


def _sum_devices(parts, name):
    _, r, c = parts.shape
    tr = _tile(r, 512)

    def body(p_ref, o_ref):
        acc = p_ref[0]
        for k in range(1, NDEV):
            acc = acc + p_ref[k]
        o_ref[...] = acc

    return pl.pallas_call(
        body, name=name, grid=(r // tr,),
        in_specs=[pl.BlockSpec((NDEV, tr, c), lambda i: (0, i, 0))],
        out_specs=pl.BlockSpec((tr, c), lambda i: (i, 0)),
        out_shape=_sds((r, c), F32), compiler_params=_cp(),
    )(parts)


def _adamw_flat(g, w, m, v, name):
    r, c = w.shape
    tr = _tile(r, 512)

    def body(g_ref, w_ref, m_ref, v_ref, d_out, m_out, v_out):
        d, mm, vv = _adamw_math(g_ref[...], w_ref[...], m_ref[...], v_ref[...])
        d_out[...] = d
        m_out[...] = mm
        v_out[...] = vv

    spec = pl.BlockSpec((tr, c), lambda i: (i, 0))
    return pl.pallas_call(
        body, name=name, grid=(r // tr,), in_specs=[spec] * 4, out_specs=[spec] * 3,
        out_shape=[_sds(w.shape, F32)] * 3, compiler_params=_cp(),
    )(g, w, m, v)


def _matmul(name, a, b, *, grid, a_spec, b_spec, o_spec, out_shape, dims, acc_shape,
            extra=(), extra_specs=(), a_pro=None, epi=None, into=None, t_spec=None, t_shape=None, plan=None):
    nk = grid[2]
    n_extra = len(extra)
    n_out = 1 + (t_spec is not None)
    n_scratch = int(nk > 1)
    n_pi, n_po = (len(plan.operands), len(plan.out_shape)) if plan is not None else (0, 0)

    def body(*refs):
        a_ref, b_ref = refs[0], refs[1]
        ex = refs[2:2 + n_extra]
        p_ins, outs, p_outs, scratch, p_sems = _split_plan_refs(
            refs[2 + n_extra + (into is not None):], n_pi, n_out, n_po, n_scratch)
        o_ref = outs[0]
        i, j, k = pl.program_id(0), pl.program_id(1), pl.program_id(2)

        if plan is not None:
            @pl.when((i == 0) & (j == 0) & (k == 0))
            def _():
                plan.start(p_ins, p_outs, p_sems)

        def product():
            av = a_ref[...]
            if a_pro is not None:
                av = a_pro(av)
            return _dot(av, b_ref[...], dims)

        def finish(r):
            if epi is not None:
                r = epi(r, *[e[...] for e in ex])
            o_ref[...] = r.astype(o_ref.dtype)
            if t_spec is not None:
                outs[1][...] = r.T.astype(outs[1].dtype)

        if nk == 1:
            finish(product())
        else:
            acc_ref = scratch[0]

            @pl.when(k == 0)
            def _():
                acc_ref[...] = jnp.zeros_like(acc_ref)

            acc_ref[...] += product()

            @pl.when(k == nk - 1)
            def _():
                finish(acc_ref[...])

        if plan is not None:
            @pl.when((i == grid[0] - 1) & (j == grid[1] - 1) & (k == nk - 1))
            def _():
                plan.finish(p_ins, p_outs, p_sems)

    operands = [a, b, *extra]
    in_specs = [a_spec, b_spec, *extra_specs]
    aliases = {}
    if into is not None:
        operands.append(into)
        in_specs.append(ANY)
        aliases = {len(operands) - 1: 0}
        out_shape = _sds(into.shape, into.dtype)
    o_specs, out_shapes = [o_spec], [out_shape]
    if t_spec is not None:
        o_specs, out_shapes = o_specs + [t_spec], out_shapes + [t_shape]
    scratch_shapes = [] if nk == 1 else [pltpu.VMEM(acc_shape, F32)]
    if plan is not None:
        operands, in_specs = operands + plan.operands, in_specs + [HBM] * n_pi
        o_specs, out_shapes = o_specs + [HBM] * n_po, out_shapes + plan.out_shape
        scratch_shapes = scratch_shapes + plan.scratch
    res = pl.pallas_call(
        body, name=name, grid=grid, in_specs=in_specs, out_specs=o_specs, out_shape=out_shapes,
        scratch_shapes=scratch_shapes, input_output_aliases=aliases, compiler_params=_cp(),
    )(*operands)
    main = res[0] if n_out == 1 else list(res[:n_out])
    return main if plan is None else (main, list(res[n_out:]))


def _rms_fwd(x, g, name):
    L, D = x.shape
    tm = _tile(L, 256)

    def body(x_ref, g_ref, o_ref, ot_ref):
        xf = x_ref[...]
        rstd = lax.rsqrt(jnp.mean(xf * xf, axis=-1, keepdims=True) + EPS)
        y = xf * rstd * g_ref[...]
        o_ref[...] = y.astype(o_ref.dtype)
        ot_ref[...] = y.T.astype(ot_ref.dtype)

    return pl.pallas_call(
        body, name=name, grid=(L // tm,),
        in_specs=[pl.BlockSpec((tm, D), lambda i: (i, 0)), pl.BlockSpec((1, D), lambda i: (0, 0))],
        out_specs=[pl.BlockSpec((tm, D), lambda i: (i, 0)), pl.BlockSpec((D, tm), lambda i: (0, i))],
        out_shape=[_sds((L, D), BF16), _sds((D, L), BF16)], compiler_params=_cp(),
    )(x, g)


def _rms_bwd(dh, x, g, dres, name):
    L, D = x.shape
    tm = _tile(L, 256)

    def body(dh_ref, x_ref, g_ref, dr_ref, dx_ref, dxb_ref, dg_ref):
        @pl.when(pl.program_id(0) == 0)
        def _():
            dg_ref[...] = jnp.zeros_like(dg_ref)

        xf = x_ref[...]
        dhf = dh_ref[...].astype(F32)
        rstd = lax.rsqrt(jnp.mean(xf * xf, axis=-1, keepdims=True) + EPS)
        xhat = xf * rstd
        dg_ref[...] += jnp.sum(dhf * xhat, axis=0, keepdims=True)
        dxh = dhf * g_ref[...]
        dx = dr_ref[...] + rstd * (dxh - xhat * jnp.mean(dxh * xhat, axis=-1, keepdims=True))
        dx_ref[...] = dx
        dxb_ref[...] = dx.astype(BF16)

    row = pl.BlockSpec((tm, D), lambda i: (i, 0))
    vec = pl.BlockSpec((1, D), lambda i: (0, 0))
    return pl.pallas_call(
        body, name=name, grid=(L // tm,), in_specs=[row, row, vec, row], out_specs=[row, row, vec],
        out_shape=[_sds((L, D), F32), _sds((L, D), BF16), _sds((1, D), F32)], compiler_params=_cp(),
    )(dh, x, g, dres)


def _loss_head(x, g, target, name):
    L, D = x.shape
    tm = _tile(L, 256)

    def body(x_ref, g_ref, t_ref, loss_ref, dx_ref, dxb_ref, dg_ref):
        @pl.when(pl.program_id(0) == 0)
        def _():
            dg_ref[...] = jnp.zeros_like(dg_ref)
            loss_ref[...] = jnp.zeros_like(loss_ref)

        xf = x_ref[...]
        gv = g_ref[...]
        rstd = lax.rsqrt(jnp.mean(xf * xf, axis=-1, keepdims=True) + EPS)
        xhat = xf * rstd
        err = xhat * gv - t_ref[...]
        part = jnp.sum(jnp.sum(err * err, axis=-1, keepdims=True), axis=0, keepdims=True)
        loss_ref[...] += jnp.broadcast_to(part * (0.5 / D), loss_ref.shape)
        dy = err * (1.0 / D)
        dg_ref[...] += jnp.sum(dy * xhat, axis=0, keepdims=True)
        dxh = dy * gv
        dx = rstd * (dxh - xhat * jnp.mean(dxh * xhat, axis=-1, keepdims=True))
        dx_ref[...] = dx
        dxb_ref[...] = dx.astype(BF16)

    row = pl.BlockSpec((tm, D), lambda i: (i, 0))
    vec = pl.BlockSpec((1, D), lambda i: (0, 0))
    lspec = pl.BlockSpec((SUBLANES, LANES), lambda i: (0, 0))
    return pl.pallas_call(
        body, name=name, grid=(L // tm,), in_specs=[row, vec, row], out_specs=[lspec, row, row, vec],
        out_shape=[_sds((SUBLANES, LANES), F32), _sds((L, D), F32), _sds((L, D), BF16), _sds((1, D), F32)],
        compiler_params=_cp(),
    )(x, g, target)


_GELU_C = math.sqrt(2.0 / math.pi)
_GELU_K = 0.044715


def _gelu(x):
    return 0.5 * x * (1.0 + jnp.tanh(_GELU_C * (x + _GELU_K * (x * x * x))))


def _gelu_and_grad(x):
    t = jnp.tanh(_GELU_C * (x + _GELU_K * (x * x * x)))
    val = 0.5 * x * (1.0 + t)
    grad = 0.5 * (1.0 + t) + 0.5 * x * (1.0 - t * t) * (_GELU_C * (1.0 + 3.0 * _GELU_K * (x * x)))
    return val, grad


def _sigmoid(x):
    e = jnp.exp(-jnp.abs(x))
    return jnp.where(x >= 0, 1.0, e) / (1.0 + e)


def _tril_mask():
    r = lax.broadcasted_iota(jnp.int32, (CHUNK, CHUNK), 0)
    c = lax.broadcasted_iota(jnp.int32, (CHUNK, CHUNK), 1)
    return r >= c


def _gmlp_fwd(proj, norm_g, w_s, b_col, mix, name):
    L = proj.shape[0]
    groups = mix // CHUNK
    tt = _tile(L, 2 * CHUNK)

    def body(uv_ref, g_ref, w_ref, b_ref, o_ref):
        z = _gelu(uv_ref[...].astype(F32))
        u, v = z[:, :mix], z[:, mix:]
        vn = v * lax.rsqrt(jnp.mean(v * v, axis=-1, keepdims=True) + EPS) * g_ref[...]
        mask = _tril_mask()
        for gi in range(groups):
            wt = jnp.where(mask, w_ref[gi], 0.0).astype(BF16)
            cols = slice(gi * CHUNK, (gi + 1) * CHUNK)
            for cc in range(tt // CHUNK):
                rows = slice(cc * CHUNK, (cc + 1) * CHUNK)
                mixed = _dot(wt, vn[rows, cols].astype(BF16)) + b_ref[gi]
                o_ref[rows, cols] = (u[rows, cols] * mixed).astype(o_ref.dtype)

    return pl.pallas_call(
        body, name=name, grid=(L // tt,),
        in_specs=[pl.BlockSpec((tt, 2 * mix), lambda i: (i, 0)),
                  pl.BlockSpec((1, mix), lambda i: (0, 0)),
                  pl.BlockSpec((groups, CHUNK, CHUNK), lambda i: (0, 0, 0)),
                  pl.BlockSpec((groups, CHUNK, 1), lambda i: (0, 0, 0))],
        out_specs=pl.BlockSpec((tt, mix), lambda i: (i, 0)),
        out_shape=_sds((L, mix), BF16), compiler_params=_cp(),
    )(proj, norm_g, w_s, b_col)


def _gmlp_bwd(proj, norm_g, w_s, w_st, b_col, dy, mix, name):
    L = proj.shape[0]
    groups = mix // CHUNK
    tt = _tile(L, 2 * CHUNK)

    def body(uv_ref, g_ref, w_ref, wt_ref, b_ref, dy_ref, duv_ref, dg_ref, dw_ref, db_ref, du_s, dvn_s):
        @pl.when(pl.program_id(0) == 0)
        def _():
            dg_ref[...] = jnp.zeros_like(dg_ref)
            dw_ref[...] = jnp.zeros_like(dw_ref)
            db_ref[...] = jnp.zeros_like(db_ref)

        z, zgrad = _gelu_and_grad(uv_ref[...].astype(F32))
        u, v = z[:, :mix], z[:, mix:]
        rstd = lax.rsqrt(jnp.mean(v * v, axis=-1, keepdims=True) + EPS)
        vhat = v * rstd
        gv = g_ref[...]
        vn = vhat * gv
        dyf = dy_ref[...].astype(F32)
        mask = _tril_mask()
        r = lax.broadcasted_iota(jnp.int32, (CHUNK, CHUNK), 0)
        c = lax.broadcasted_iota(jnp.int32, (CHUNK, CHUNK), 1)
        for gi in range(groups):
            w_low = jnp.where(mask, w_ref[gi], 0.0).astype(BF16)
            w_up = jnp.where(r <= c, wt_ref[gi], 0.0).astype(BF16)
            cols = slice(gi * CHUNK, (gi + 1) * CHUNK)
            for cc in range(tt // CHUNK):
                rows = slice(cc * CHUNK, (cc + 1) * CHUNK)
                vnb = vn[rows, cols].astype(BF16)
                mixed = _dot(w_low, vnb) + b_ref[gi]
                dyb = dyf[rows, cols]
                dm = dyb * u[rows, cols]
                dmb = dm.astype(BF16)
                du_s[rows, cols] = dyb * mixed
                dvn_s[rows, cols] = _dot(w_up, dmb)
                dw_ref[gi] += jnp.where(mask, _dot(dmb, vnb, NT), 0.0)
                db_ref[gi] += jnp.sum(dm, axis=1, keepdims=True)
        dvn = dvn_s[...]
        dg_ref[...] += jnp.sum(dvn * vhat, axis=0, keepdims=True)
        dvh = dvn * gv
        dv = rstd * (dvh - vhat * jnp.mean(dvh * vhat, axis=-1, keepdims=True))
        duv_ref[:, :mix] = (du_s[...] * zgrad[:, :mix]).astype(duv_ref.dtype)
        duv_ref[:, mix:] = (dv * zgrad[:, mix:]).astype(duv_ref.dtype)

    wspec = pl.BlockSpec((groups, CHUNK, CHUNK), lambda i: (0, 0, 0))
    bspec = pl.BlockSpec((groups, CHUNK, 1), lambda i: (0, 0, 0))
    gspec = pl.BlockSpec((1, mix), lambda i: (0, 0))
    return pl.pallas_call(
        body, name=name, grid=(L // tt,),
        in_specs=[pl.BlockSpec((tt, 2 * mix), lambda i: (i, 0)), gspec, wspec, wspec, bspec,
                  pl.BlockSpec((tt, mix), lambda i: (i, 0))],
        out_specs=[pl.BlockSpec((tt, 2 * mix), lambda i: (i, 0)), gspec, wspec, bspec],
        out_shape=[_sds((L, 2 * mix), BF16), _sds((1, mix), F32),
                   _sds((groups, CHUNK, CHUNK), F32), _sds((groups, CHUNK, 1), F32)],
        scratch_shapes=[pltpu.VMEM((tt, mix), F32), pltpu.VMEM((tt, mix), F32)],
        compiler_params=_cp(),
    )(proj, norm_g, w_s, w_st, b_col, dy)


S5_CB = 512
S5_UB = 128
S5_LEVELS = (1, 2, 4)


def _s5_fwd_consts(ar, ai):
    rows = lax.broadcasted_iota(jnp.int32, ar.shape, 0)
    out = []
    for d in S5_LEVELS:
        m = rows >= d
        out.append((jnp.where(m, ar, 0.0), jnp.where(m, ai, 0.0)))
        ar, ai = ar * ar - ai * ai, 2.0 * ar * ai
    return out


def _s5_rev_consts(ar, ai):
    rows = lax.broadcasted_iota(jnp.int32, ar.shape, 0)
    out = []
    for d in S5_LEVELS:
        m = rows < SUBLANES - d
        out.append((jnp.where(m, ar, 0.0), jnp.where(m, ai, 0.0)))
        ar, ai = ar * ar - ai * ai, 2.0 * ar * ai
    return out


def _scan8(xr, xi, consts, reverse):
    for (cr, ci), d in zip(consts, S5_LEVELS):
        sh = SUBLANES - d if reverse else d
        pr = pltpu.roll(xr, sh, 0)
        pi = pltpu.roll(xi, sh, 0)
        xr, xi = xr + (cr * pr - ci * pi), xi + (cr * pi + ci * pr)
    return xr, xi


def _s5_forward_block(u_ref, bre_ref, bim_ref, sr, si, ar, ai, pwr, pwi, carry, tt):
    consts = _s5_fwd_consts(ar, ai)
    u = u_ref[...]
    sr[...] = _dot(u, bre_ref[...])
    si[...] = _dot(u, bim_ref[...])

    def step(r, cs):
        cr, ci = cs
        o = pl.multiple_of(r * SUBLANES, SUBLANES)
        xr, xi = _scan8(sr[pl.ds(o, SUBLANES), :], si[pl.ds(o, SUBLANES), :], consts, False)
        xr, xi = xr + (pwr * cr - pwi * ci), xi + (pwr * ci + pwi * cr)
        sr[pl.ds(o, SUBLANES), :] = xr
        si[pl.ds(o, SUBLANES), :] = xi
        return (jnp.broadcast_to(sr[pl.ds(o + SUBLANES - 1, 1), :], xr.shape),
                jnp.broadcast_to(si[pl.ds(o + SUBLANES - 1, 1), :], xi.shape))

    return lax.fori_loop(0, tt // SUBLANES, step, carry)


def _s5_powers(ar, ai):
    rows = lax.broadcasted_iota(jnp.int32, ar.shape, 0)
    return _scan8(jnp.where(rows == 0, ar, 0.0), jnp.where(rows == 0, ai, 0.0), _s5_fwd_consts(ar, ai), False)


def _s5_scan_fwd(proj, a_re, a_im, b_re, b_im, c_re, c_im, mix, name, plan=None):
    L = proj.shape[0]
    S = a_re.shape[1]
    nj = mix // S5_UB
    tt = _tile(L, 256)
    ni = L // tt
    ucol = 2 * mix // S5_UB

    n_pi, n_po = (len(plan.operands), len(plan.out_shape)) if plan is not None else (0, 0)

    def body(u_ref, ar_ref, ai_ref, bre_ref, bim_ref, cre_ref, cim_ref, *rest):
        p_ins, (y_ref, sr, si), p_outs, (car, cai), p_sems = _split_plan_refs(rest, n_pi, 3, n_po, 2)
        j, i = pl.program_id(0), pl.program_id(1)
        ar = jnp.broadcast_to(ar_ref[...], (SUBLANES, S5_CB))
        ai = jnp.broadcast_to(ai_ref[...], (SUBLANES, S5_CB))

        if plan is not None:
            @pl.when((j == 0) & (i == 0))
            def _():
                plan.start(p_ins, p_outs, p_sems)

        @pl.when(i == 0)
        def _():
            car[...] = jnp.zeros_like(car)
            cai[...] = jnp.zeros_like(cai)

        pwr, pwi = _s5_powers(ar, ai)
        cr, ci = _s5_forward_block(u_ref, bre_ref, bim_ref, sr, si, ar, ai, pwr, pwi,
                                   (car[...], cai[...]), tt)
        car[...] = cr
        cai[...] = ci
        y_ref[...] = _dot(sr[...].astype(BF16), cre_ref[...]) - _dot(si[...].astype(BF16), cim_ref[...])

        if plan is not None:
            @pl.when((j == nj - 1) & (i == ni - 1))
            def _():
                plan.finish(p_ins, p_outs, p_sems)

    avec = pl.BlockSpec((1, S5_CB), lambda j, i: (0, j))
    bspec = pl.BlockSpec((S5_UB, S5_CB), lambda j, i: (j, 0))
    cspec = pl.BlockSpec((S5_CB, S5_UB), lambda j, i: (j, 0))
    states = pl.BlockSpec((tt, S5_CB), lambda j, i: (i, j))
    res = pl.pallas_call(
        body, name=name, grid=(nj, ni),
        in_specs=[pl.BlockSpec((tt, S5_UB), lambda j, i: (i, ucol + j)), avec, avec, bspec, bspec, cspec, cspec]
        + [HBM] * n_pi,
        out_specs=[pl.BlockSpec((tt, S5_UB), lambda j, i: (i, j)), states, states] + [HBM] * n_po,
        out_shape=[_sds((L, mix), F32), _sds((L, S), F32), _sds((L, S), F32)]
        + (plan.out_shape if plan is not None else []),
        scratch_shapes=[pltpu.VMEM((SUBLANES, S5_CB), F32), pltpu.VMEM((SUBLANES, S5_CB), F32)]
        + (plan.scratch if plan is not None else []),
        compiler_params=_cp(),
    )(proj, a_re, a_im, b_re, b_im, c_re, c_im, *(plan.operands if plan is not None else []))
    return res[0], res[1], res[2], list(res[3:])


def _s5_scan_bwd(proj, a_re, a_im, b_re, b_im, c_re, c_im, s_re, s_im, dy, dxin, mix, name):
    L = proj.shape[0]
    S = a_re.shape[1]
    nj = mix // S5_UB
    tt = _tile(L, 256)
    ni = L // tt
    ucol = 2 * mix // S5_UB
    nb = tt // SUBLANES

    def body(u_ref, ar_ref, ai_ref, bre_ref, bim_ref, cre_ref, cim_ref, sr, si, dy_ref, dx_ref,
             du_ref, dbr_ref, dbi_ref, dcr_ref, dci_ref, dar_ref, dai_ref,
             gr, gi, car, cai, acr, aci):
        first = pl.program_id(1) == 0
        ar = jnp.broadcast_to(ar_ref[...], (SUBLANES, S5_CB))
        ai = jnp.broadcast_to(ai_ref[...], (SUBLANES, S5_CB))

        @pl.when(first)
        def _():
            for ref in (car, cai, acr, aci, dbr_ref, dbi_ref, dcr_ref, dci_ref):
                ref[...] = jnp.zeros_like(ref)

        dyb = dy_ref[...].astype(BF16)
        gr[...] = _dot(dyb, cre_ref[...], NT)
        gi[...] = -_dot(dyb, cim_ref[...], NT)

        nai = -ai
        consts = _s5_rev_consts(ar, nai)
        rows = lax.broadcasted_iota(jnp.int32, ar.shape, 0)
        last = rows == SUBLANES - 1
        qr, qi = _scan8(jnp.where(last, ar, 0.0), jnp.where(last, nai, 0.0), consts, True)

        def step(k, cs):
            cr, ci, dr, di = cs
            o = pl.multiple_of((nb - 1 - k) * SUBLANES, SUBLANES)
            xr, xi = _scan8(gr[pl.ds(o, SUBLANES), :], gi[pl.ds(o, SUBLANES), :], consts, True)
            xr, xi = xr + (qr * cr - qi * ci), xi + (qr * ci + qi * cr)
            gr[pl.ds(o, SUBLANES), :] = xr
            gi[pl.ds(o, SUBLANES), :] = xi
            hr = jnp.where(last, cr, pltpu.roll(xr, SUBLANES - 1, 0))
            hi = jnp.where(last, ci, pltpu.roll(xi, SUBLANES - 1, 0))
            s_r = sr[pl.ds(o, SUBLANES), :]
            s_i = si[pl.ds(o, SUBLANES), :]
            dr = dr + (hr * s_r + hi * s_i)
            di = di + (hi * s_r - hr * s_i)
            return (jnp.broadcast_to(gr[pl.ds(o, 1), :], xr.shape),
                    jnp.broadcast_to(gi[pl.ds(o, 1), :], xi.shape), dr, di)

        cr, ci, dr, di = lax.fori_loop(0, nb, step, (car[...], cai[...], acr[...], aci[...]))
        car[...] = cr
        cai[...] = ci
        acr[...] = dr
        aci[...] = di
        dar_ref[...] = jnp.sum(dr, axis=0, keepdims=True)
        dai_ref[...] = jnp.sum(di, axis=0, keepdims=True)

        u = u_ref[...]
        gbr = gr[...].astype(BF16)
        gbi = gi[...].astype(BF16)
        dbr_ref[...] += _dot(u, gbr, TN)
        dbi_ref[...] += _dot(u, gbi, TN)
        du = _dot(gbr, bre_ref[...], NT) + _dot(gbi, bim_ref[...], NT)
        du_ref[...] = (du + dx_ref[...].astype(F32)).astype(du_ref.dtype)
        dyf = dy_ref[...].astype(BF16)
        dcr_ref[...] += _dot(sr[...].astype(BF16), dyf, TN)
        dci_ref[...] -= _dot(si[...].astype(BF16), dyf, TN)

    rev = lambda i: ni - 1 - i
    avec = pl.BlockSpec((1, S5_CB), lambda j, i: (0, j))
    bspec = pl.BlockSpec((S5_UB, S5_CB), lambda j, i: (j, 0))
    cspec = pl.BlockSpec((S5_CB, S5_UB), lambda j, i: (j, 0))
    states = pl.BlockSpec((tt, S5_CB), lambda j, i: (rev(i), j))
    tile = pl.BlockSpec((tt, S5_UB), lambda j, i: (rev(i), j))
    return pl.pallas_call(
        body, name=name, grid=(nj, ni),
        in_specs=[pl.BlockSpec((tt, S5_UB), lambda j, i: (rev(i), ucol + j)), avec, avec, bspec, bspec,
                  cspec, cspec, states, states, tile, tile],
        out_specs=[tile, bspec, bspec, cspec, cspec, avec, avec],
        out_shape=[_sds((L, mix), BF16), _sds((mix, S5_CB), F32), _sds((mix, S5_CB), F32),
                   _sds((S, S5_UB), F32), _sds((S, S5_UB), F32), _sds((1, S), F32), _sds((1, S), F32)],
        scratch_shapes=[pltpu.VMEM((tt, S5_CB), F32)] * 2 + [pltpu.VMEM((SUBLANES, S5_CB), F32)] * 4,
        compiler_params=_cp(),
    )(proj, a_re, a_im, b_re, b_im, c_re, c_im, s_re, s_im, dy, dxin)


def _s5_glu_fwd(ypre, proj, d, w_glu, b_glu, layer, mix, name):
    L = ypre.shape[0]
    tm = _tile(L, 512)

    def body(y_ref, x_ref, d_ref, w_ref, b_ref, o_ref):
        g = _gelu(y_ref[...] + d_ref[...] * x_ref[...].astype(F32))
        z = _dot(g.astype(BF16), w_ref[...]) + b_ref[...]
        o_ref[...] = (g * _sigmoid(z)).astype(o_ref.dtype)

    row = pl.BlockSpec((tm, mix), lambda i: (i, 0))
    vec = pl.BlockSpec((1, mix), lambda i: (0, 0))
    return pl.pallas_call(
        body, name=name, grid=(L // tm,),
        in_specs=[row, pl.BlockSpec((tm, mix), lambda i: (i, 2)), vec,
                  pl.BlockSpec((None, mix, mix), lambda i: (layer, 0, 0)), vec],
        out_specs=row, out_shape=_sds((L, mix), BF16), compiler_params=_cp(),
    )(ypre, proj, d, w_glu, b_glu)


def _s5_glu_bwd(ypre, proj, d, w_glu, b_glu, dout, layer, mix, name):
    L = ypre.shape[0]
    tm = _tile(L, 512)

    def body(y_ref, x_ref, d_ref, w_ref, b_ref, do_ref, dy_ref, dx_ref, dw_ref, db_ref, dd_ref):
        @pl.when(pl.program_id(0) == 0)
        def _():
            dw_ref[...] = jnp.zeros_like(dw_ref)
            db_ref[...] = jnp.zeros_like(db_ref)
            dd_ref[...] = jnp.zeros_like(dd_ref)

        xin = x_ref[...].astype(F32)
        dv = d_ref[...]
        g, ggrad = _gelu_and_grad(y_ref[...] + dv * xin)
        gb = g.astype(BF16)
        w = w_ref[...]
        sg = _sigmoid(_dot(gb, w) + b_ref[...])
        do = do_ref[...].astype(F32)
        dz = do * g * sg * (1.0 - sg)
        dzb = dz.astype(BF16)
        dg = do * sg + _dot(dzb, w, NT)
        dw_ref[...] += _dot(gb, dzb, TN)
        db_ref[...] += jnp.sum(dz, axis=0, keepdims=True)
        dyv = dg * ggrad
        dd_ref[...] += jnp.sum(dyv * xin, axis=0, keepdims=True)
        dy_ref[...] = dyv.astype(dy_ref.dtype)
        dx_ref[...] = (dyv * dv).astype(dx_ref.dtype)

    row = pl.BlockSpec((tm, mix), lambda i: (i, 0))
    vec = pl.BlockSpec((1, mix), lambda i: (0, 0))
    mat = pl.BlockSpec((mix, mix), lambda i: (0, 0))
    return pl.pallas_call(
        body, name=name, grid=(L // tm,),
        in_specs=[row, pl.BlockSpec((tm, mix), lambda i: (i, 2)), vec,
                  pl.BlockSpec((None, mix, mix), lambda i: (layer, 0, 0)), vec, row],
        out_specs=[row, row, mat, vec, vec],
        out_shape=[_sds((L, mix), BF16), _sds((L, mix), BF16), _sds((mix, mix), F32),
                   _sds((1, mix), F32), _sds((1, mix), F32)],
        compiler_params=_cp(),
    )(ypre, proj, d, w_glu, b_glu, dout)


def _after_matrix(t):
    j = lax.broadcasted_iota(jnp.int32, (t, t), 0)
    s = lax.broadcasted_iota(jnp.int32, (t, t), 1)
    return jnp.where(j > s, 1.0, 0.0).astype(BF16)


SB_BLOCK = 512
SB_SUB = 128


def _suffix_sum(x, m_sub):
    sub = m_sub.shape[0]
    hi = x.astype(BF16)
    lo = (x - hi.astype(F32)).astype(BF16)
    pieces, carry = [], None
    for c in reversed(range(x.shape[1] // sub)):
        cols = slice(c * sub, (c + 1) * sub)
        s = _dot(hi[:, cols], m_sub) + _dot(lo[:, cols], m_sub)
        if carry is not None:
            s = s + carry
        pieces.append(s)
        carry = s[:, 0:1] + x[:, c * sub:c * sub + 1]
    return jnp.concatenate(pieces[::-1], axis=1), carry


LOG2E = 1.4426950408889634


def _sb_block(q, k, diagonal):
    z2 = _dot(q, k, NT) * (HEAD_DIM ** -0.5 * LOG2E)
    e = jnp.exp2(-jnp.abs(z2))
    sp2 = jnp.maximum(z2, 0.0) + jnp.log(1.0 + e) * LOG2E
    if not diagonal:
        return z2, sp2, e, -sp2, None
    mask = lax.broadcasted_iota(jnp.int32, z2.shape, 1) < lax.broadcasted_iota(jnp.int32, z2.shape, 0)
    return z2, sp2, e, jnp.where(mask, -sp2, 0.0), mask


def _pair_tables(nt):
    qs = [i for i in range(nt) for _ in range(i + 1)]
    ks = [i - j for i in range(nt) for j in range(i + 1)]
    return jnp.asarray(qs, jnp.int32), jnp.asarray(ks, jnp.int32)


def _split_plan_refs(rest, n_pi, n_out, n_po, n_scratch):
    a = n_pi + n_out
    b = a + n_po
    return rest[:n_pi], rest[n_pi:a], rest[a:b], rest[b:b + n_scratch], rest[b + n_scratch:]


def _sb_fwd(proj, m_after, mix, name, plan=None):
    L = proj.shape[0]
    heads = mix // HEAD_DIM
    T = _tile(L, SB_BLOCK)
    sub = m_after.shape[0]
    nt = L // T
    npairs = nt * (nt + 1) // 2
    qc, kc, vc = (3 * mix // HEAD_DIM, 4 * mix // HEAD_DIM, 5 * mix // HEAD_DIM)
    n_pi, n_po = (len(plan.operands), len(plan.out_shape)) if plan is not None else (0, 0)

    def body(qt, kt, q_ref, k_ref, v_ref, m_ref, *rest):
        p_ins, (o_ref, o32_ref), p_outs, (acc, ra), p_sems = _split_plan_refs(rest, n_pi, 2, n_po, 2)
        h, p = pl.program_id(0), pl.program_id(1)
        qi, ki = qt[p], kt[p]

        if plan is not None:
            @pl.when((h == 0) & (p == 0))
            def _():
                plan.start(p_ins, p_outs, p_sems)

        def block(diagonal):
            z2, sp2, _, lg2, mask = _sb_block(q_ref[...], k_ref[...], diagonal)
            after, rs = _suffix_sum(lg2, m_ref[...])
            if not diagonal:
                after = after + ra[...]
            w = jnp.exp2(z2 - sp2 + after)
            if diagonal:
                w = jnp.where(mask, w, 0.0)
            pv = _dot(w.astype(BF16), v_ref[...])
            if diagonal:
                acc[...] = pv
                ra[...] = rs
            else:
                acc[...] += pv
                ra[...] += rs

        pl.when(ki == qi)(functools.partial(block, True))
        pl.when(ki != qi)(functools.partial(block, False))

        @pl.when(ki == 0)
        def _():
            o_ref[...] = acc[...].astype(o_ref.dtype)
            o32_ref[...] = acc[...]

        if plan is not None:
            @pl.when((h == heads - 1) & (p == npairs - 1))
            def _():
                plan.finish(p_ins, p_outs, p_sems)

    qtab, ktab = _pair_tables(nt)
    kv = lambda col: pl.BlockSpec((T, HEAD_DIM), lambda h, p, qt, kt: (kt[p], col + h))
    qo = pl.BlockSpec((T, HEAD_DIM), lambda h, p, qt, kt: (qt[p], h))
    outs = pl.pallas_call(
        body, name=name,
        grid_spec=pltpu.PrefetchScalarGridSpec(
            num_scalar_prefetch=2, grid=(heads, npairs),
            in_specs=[pl.BlockSpec((T, HEAD_DIM), lambda h, p, qt, kt: (qt[p], qc + h)), kv(kc), kv(vc),
                      pl.BlockSpec((sub, sub), lambda h, p, qt, kt: (0, 0))] + [HBM] * n_pi,
            out_specs=[qo, qo] + [HBM] * n_po,
            scratch_shapes=[pltpu.VMEM((T, HEAD_DIM), F32), pltpu.VMEM((T, 1), F32)]
            + (plan.scratch if plan is not None else [])),
        out_shape=[_sds((L, mix), BF16), _sds((L, mix), F32)] + (plan.out_shape if plan is not None else []),
        compiler_params=_cp(),
    )(qtab, ktab, proj, proj, proj, m_after, *(plan.operands if plan is not None else []))
    return outs[0], outs[1], list(outs[2:])


def _sb_bwd(proj, m_after, out, dout, mix, name, plan=None):
    L = proj.shape[0]
    heads = mix // HEAD_DIM
    T = _tile(L, SB_BLOCK)
    sub = m_after.shape[0]
    nt = L // T
    npairs = nt * (nt + 1) // 2
    qc, kc, vc = (3 * mix // HEAD_DIM, 4 * mix // HEAD_DIM, 5 * mix // HEAD_DIM)
    scale = HEAD_DIM ** -0.5
    n_pi, n_po = (len(plan.operands), len(plan.out_shape)) if plan is not None else (0, 0)

    def body(qt, kt, q_ref, k_ref, v_ref, m_ref, o_ref, do_ref, *rest):
        p_ins, (dq_ref, dk_ref, dv_ref), p_outs, (dq_acc, ra, rp, delta), p_sems = _split_plan_refs(
            rest, n_pi, 3, n_po, 4)
        h, p = pl.program_id(0), pl.program_id(1)
        qi, ki = qt[p], kt[p]

        if plan is not None:
            @pl.when((h == 0) & (p == 0))
            def _():
                plan.start(p_ins, p_outs, p_sems)

        @pl.when(p == 0)
        def _():
            dk_ref[...] = jnp.zeros_like(dk_ref)
            dv_ref[...] = jnp.zeros_like(dv_ref)

        def block(diagonal):
            q, k, v, do = q_ref[...], k_ref[...], v_ref[...], do_ref[...]
            z2, sp2, e, lg2, mask = _sb_block(q, k, diagonal)
            after, rs_a = _suffix_sum(lg2, m_ref[...])
            if diagonal:
                dl = jnp.sum(do.astype(F32) * o_ref[...], axis=1, keepdims=True)
                delta[...] = dl
            else:
                dl = delta[...]
                after = after + ra[...]
            w = jnp.exp2(z2 - sp2 + after)
            if diagonal:
                w = jnp.where(mask, w, 0.0)
            wb = w.astype(BF16)
            pm = wb.astype(F32) * _dot(do, v, NT)
            suffix, rs_p = _suffix_sum(pm, m_ref[...])
            if not diagonal:
                suffix = suffix + rp[...]
            before = dl - suffix - pm
            beta = jnp.where(z2 >= 0, 1.0, e) / (1.0 + e)
            dz = (pm * (1.0 - beta) - beta * before) * scale
            if diagonal:
                dz = jnp.where(mask, dz, 0.0)
            dzb = dz.astype(BF16)
            rows = pl.ds(pl.multiple_of(ki * T, T), T)
            dk_ref[rows, :] += _dot(dzb, q, TN)
            dv_ref[rows, :] += _dot(wb, do, TN)
            dq = _dot(dzb, k)
            if diagonal:
                dq_acc[...] = dq
                ra[...] = rs_a
                rp[...] = rs_p
            else:
                dq_acc[...] += dq
                ra[...] += rs_a
                rp[...] += rs_p

        pl.when(ki == qi)(functools.partial(block, True))
        pl.when(ki != qi)(functools.partial(block, False))

        @pl.when(ki == 0)
        def _():
            dq_ref[...] = dq_acc[...]

        if plan is not None:
            @pl.when((h == heads - 1) & (p == npairs - 1))
            def _():
                plan.finish(p_ins, p_outs, p_sems)

    qtab, ktab = _pair_tables(nt)
    kv = lambda col: pl.BlockSpec((T, HEAD_DIM), lambda h, p, qt, kt: (kt[p], col + h))
    qo = pl.BlockSpec((T, HEAD_DIM), lambda h, p, qt, kt: (qt[p], h))
    whole = pl.BlockSpec((L, HEAD_DIM), lambda h, p, qt, kt: (0, h))
    outs = pl.pallas_call(
        body, name=name,
        grid_spec=pltpu.PrefetchScalarGridSpec(
            num_scalar_prefetch=2, grid=(heads, npairs),
            in_specs=[pl.BlockSpec((T, HEAD_DIM), lambda h, p, qt, kt: (qt[p], qc + h)), kv(kc), kv(vc),
                      pl.BlockSpec((sub, sub), lambda h, p, qt, kt: (0, 0)), qo, qo] + [HBM] * n_pi,
            out_specs=[qo, whole, whole] + [HBM] * n_po,
            scratch_shapes=[pltpu.VMEM((T, HEAD_DIM), F32), pltpu.VMEM((T, 1), F32), pltpu.VMEM((T, 1), F32),
                            pltpu.VMEM((T, 1), F32)] + (plan.scratch if plan is not None else [])),
        out_shape=[_sds((L, mix), F32)] * 3 + (plan.out_shape if plan is not None else []),
        compiler_params=_cp(),
    )(qtab, ktab, proj, proj, proj, m_after, out, dout, *(plan.operands if plan is not None else []))
    return outs[0], outs[1], outs[2], list(outs[3:])


def _merge_fwd(ys, w_branch, proj, b_gate, layer, b_layer, name):
    L, mix = ys[0].shape
    bw = w_branch.shape[-1]
    D = bw * NDEV
    tm = _tile(L, 512)
    gc = 6 * mix // bw

    def body(ya, yb, yc, w_ref, pa, pb, pc, b_ref, o_ref, ot_ref):
        acc = None
        for n, (y_ref, p_ref) in enumerate(((ya, pa), (yb, pb), (yc, pc))):
            gate = _sigmoid(p_ref[...].astype(F32) + b_ref[n:n + 1, :])
            term = gate * _dot(y_ref[...], w_ref[n])
            acc = term if acc is None else acc + term
        o_ref[...] = acc.astype(o_ref.dtype)
        ot_ref[...] = acc.T.astype(ot_ref.dtype)

    yspec = pl.BlockSpec((tm, mix), lambda i, j: (i, 0))
    pspec = [pl.BlockSpec((tm, bw), functools.partial(lambda i, j, n: (i, gc + n * NDEV + j), n=n)) for n in range(3)]
    return pl.pallas_call(
        body, name=name, grid=(L // tm, NDEV),
        in_specs=[yspec, yspec, yspec,
                  pl.BlockSpec((None, None, 3, mix, bw), lambda i, j: (layer, j, 0, 0, 0)),
                  *pspec, pl.BlockSpec((None, None, 3, bw), lambda i, j: (b_layer, j, 0, 0))],
        out_specs=[pl.BlockSpec((tm, bw), lambda i, j: (i, j)), pl.BlockSpec((bw, tm), lambda i, j: (j, i))],
        out_shape=[_sds((L, D), BF16), _sds((D, L), BF16)], compiler_params=_cp(),
    )(*ys, w_branch, proj, proj, proj, b_gate)


def _merge_bwd(ys, w_branch, proj, b_gate, dmerged, layer, b_layer, name):
    L, mix = ys[0].shape
    bw = w_branch.shape[-1]
    D = bw * NDEV
    tm = _tile(L, 512)
    gc = 6 * mix // bw

    def body(ya, yb, yc, w_ref, pa, pb, pc, b_ref, dm_ref,
             dpa, dpb, dpc, dba, dbb, dbc, dya, dyb, dyc, dbg_ref, acc):
        i, j = pl.program_id(0), pl.program_id(1)

        @pl.when((i == 0) & (j == 0))
        def _():
            dbg_ref[...] = jnp.zeros_like(dbg_ref)

        @pl.when(j == 0)
        def _():
            acc[...] = jnp.zeros_like(acc)

        dm = dm_ref[...].astype(F32)
        for n, (y_ref, p_ref, dp_ref, db_ref) in enumerate(((ya, pa, dpa, dba), (yb, pb, dpb, dbb), (yc, pc, dpc, dbc))):
            gate = _sigmoid(p_ref[...].astype(F32) + b_ref[n:n + 1, :])
            br = _dot(y_ref[...], w_ref[n])
            dp = dm * br * gate * (1.0 - gate)
            dp_ref[...] = dp.astype(dp_ref.dtype)
            dbg_ref[j, n:n + 1, :] += jnp.sum(dp, axis=0, keepdims=True)
            dbr = (dm * gate).astype(BF16)
            db_ref[...] = dbr
            acc[n] += _dot(dbr, w_ref[n], NT)

        @pl.when(j == NDEV - 1)
        def _():
            for n, dy_ref in enumerate((dya, dyb, dyc)):
                dy_ref[...] = acc[n].astype(dy_ref.dtype)

    yspec = pl.BlockSpec((tm, mix), lambda i, j: (i, 0))
    ospec = pl.BlockSpec((tm, bw), lambda i, j: (i, j))
    pspec = [pl.BlockSpec((tm, bw), functools.partial(lambda i, j, n: (i, gc + n * NDEV + j), n=n)) for n in range(3)]
    return pl.pallas_call(
        body, name=name, grid=(L // tm, NDEV),
        in_specs=[yspec, yspec, yspec,
                  pl.BlockSpec((None, None, 3, mix, bw), lambda i, j: (layer, j, 0, 0, 0)),
                  *pspec, pl.BlockSpec((None, None, 3, bw), lambda i, j: (b_layer, j, 0, 0)), ospec],
        out_specs=[ospec] * 6 + [yspec] * 3 + [pl.BlockSpec((NDEV, 3, bw), lambda i, j: (0, 0, 0))],
        out_shape=[_sds((L, D), BF16)] * 6 + [_sds((L, mix), BF16)] * 3 + [_sds((NDEV, 3, bw), F32)],
        scratch_shapes=[pltpu.VMEM((3, tm, mix), F32)],
        compiler_params=_cp(),
    )(*ys, w_branch, proj, proj, proj, b_gate, dmerged)


def _ktile(n):
    for t in (1024, 768, 512, 384, 256, 128, 64, 32, 16, 8):
        if n % t == 0:
            return t
    return n


MM_ROWS = 1024
MM_COLS = 1024
MM_DEPTH = 2048


def _mm_cols(name, a, wg, layer, out_dtype, epi=None, with_transpose=False, plan=None):
    M, K = a.shape
    nb = wg.shape[3]
    tm, tn = _tile(M, MM_ROWS), _tile(nb, MM_COLS, LANES)
    r = nb // tn
    t_out = dict(t_spec=pl.BlockSpec((tn, tm), lambda i, j, k: (j, i)),
                 t_shape=_sds((NDEV * nb, M), out_dtype)) if with_transpose else {}
    return _matmul(
        name, a, wg, grid=(M // tm, NDEV * r, 1),
        a_spec=pl.BlockSpec((tm, K), lambda i, j, k: (i, 0)),
        b_spec=pl.BlockSpec((None, None, K, tn), lambda i, j, k: (layer, j // r, 0, j % r)),
        o_spec=pl.BlockSpec((tm, tn), lambda i, j, k: (i, j)),
        out_shape=_sds((M, NDEV * nb), out_dtype), dims=NN, acc_shape=(tm, tn), epi=epi, plan=plan, **t_out)


def _mm_cols_t(name, a, wg, layer, out_dtype):
    M = a.shape[0]
    K, nb = wg.shape[2], wg.shape[3]
    tm, tn, tk = _tile(M, MM_ROWS), _tile(K, MM_COLS, LANES), _ktile(nb)
    r = nb // tk
    return _matmul(
        name, a, wg, grid=(M // tm, K // tn, NDEV * r),
        a_spec=pl.BlockSpec((tm, tk), lambda i, j, k: (i, k)),
        b_spec=pl.BlockSpec((None, None, tn, tk), lambda i, j, k: (layer, k // r, j, k % r)),
        o_spec=pl.BlockSpec((tm, tn), lambda i, j, k: (i, j)),
        out_shape=_sds((M, K), out_dtype), dims=NT, acc_shape=(tm, tn))


def _mm_rows(name, a, wn, layer, out_dtype, res, a_pro=None):
    M, K = a.shape
    N = wn.shape[2]
    tm, tn, tk = _tile(M, MM_ROWS), _tile(N, MM_COLS // 2, LANES), _tile(K, MM_DEPTH, LANES)
    tile = pl.BlockSpec((tm, tn), lambda i, j, k: (i, j))
    return _matmul(
        name, a, wn, grid=(M // tm, N // tn, K // tk),
        a_spec=pl.BlockSpec((tm, tk), lambda i, j, k: (i, k)),
        b_spec=pl.BlockSpec((None, tk, tn), lambda i, j, k: (layer, k, j)),
        o_spec=tile, out_shape=_sds((M, N), out_dtype), dims=NN, acc_shape=(tm, tn),
        extra=(res,), extra_specs=(tile,), epi=lambda acc, rv: acc + rv, a_pro=a_pro)


def _mm_rows_t(name, a, wn, layer, out_dtype, extra=(), epi=None):
    M, N = a.shape
    K = wn.shape[1]
    tm, tn = _tile(M, MM_ROWS), _tile(K, MM_COLS, LANES)
    tile = pl.BlockSpec((tm, tn), lambda i, j, k: (i, j))
    return _matmul(
        name, a, wn, grid=(M // tm, K // tn, 1),
        a_spec=pl.BlockSpec((tm, N), lambda i, j, k: (i, 0)),
        b_spec=pl.BlockSpec((None, tn, N), lambda i, j, k: (layer, j, 0)),
        o_spec=tile, out_shape=_sds((M, K), out_dtype), dims=NT, acc_shape=(tm, tn),
        extra=extra, extra_specs=(tile,) * len(extra), epi=epi)


def _mm_grad(name, a, dy, *, tokens_last, o_block, o_map, tn, out_shape=None, into=None, a_pro=None):
    K, L = a.shape if tokens_last else a.shape[::-1]
    N = dy.shape[1]
    tm, tt = _tile(K, MM_ROWS), _tile(L, MM_ROWS, LANES)
    a_spec = (pl.BlockSpec((tm, tt), lambda i, j, k: (i, k)) if tokens_last
              else pl.BlockSpec((tt, tm), lambda i, j, k: (k, i)))
    return _matmul(
        name, a, dy, grid=(K // tm, N // tn, L // tt), a_spec=a_spec,
        b_spec=pl.BlockSpec((tt, tn), lambda i, j, k: (k, j)),
        o_spec=pl.BlockSpec(o_block(tm, tn), o_map), out_shape=out_shape,
        dims=NN if tokens_last else TN, acc_shape=(tm, tn), a_pro=a_pro, into=into)


def _grad_cols(name, a_t, dy):
    nb = dy.shape[1] // NDEV
    tn = _tile(nb, MM_COLS, LANES)
    r = nb // tn
    return _mm_grad(name, a_t, dy, tokens_last=True, o_block=lambda tm, t: (None, tm, t),
                    o_map=lambda i, j, k: (j // r, i, j % r), tn=tn,
                    out_shape=_sds((NDEV, a_t.shape[0], nb), BF16))


def _grad_rows(name, a_t, dy, a_pro=None):
    tn = _tile(dy.shape[1], MM_COLS, LANES)
    return _mm_grad(name, a_t, dy, tokens_last=True, o_block=lambda tm, t: (tm, t),
                    o_map=lambda i, j, k: (i, j), tn=tn,
                    out_shape=_sds((a_t.shape[0], dy.shape[1]), BF16), a_pro=a_pro)


def _grad_branch(name, y, dbr, into, n):
    bw = into.shape[3]
    return _mm_grad(name, y, dbr, tokens_last=False, o_block=lambda tm, t: (None, None, tm, t),
                    o_map=lambda i, j, k: (j, n, i, 0), tn=bw, into=into)


def _s5_discretize(lam_re, lam_im, log_dt, b_re, b_im):
    dt = jnp.exp(log_dt)[:, None]
    mag = jnp.exp(lam_re * dt)
    ab_re = mag * jnp.cos(lam_im * dt)
    ab_im = mag * jnp.sin(lam_im * dt)
    den = lam_re * lam_re + lam_im * lam_im
    n_re = ab_re - 1.0
    n_im = ab_im
    k_re = (n_re * lam_re + n_im * lam_im) / den
    k_im = (n_im * lam_re - n_re * lam_im) / den
    bb_re = k_re[..., None] * b_re - k_im[..., None] * b_im
    bb_im = k_re[..., None] * b_im + k_im[..., None] * b_re
    return ab_re, ab_im, bb_re, bb_im


def _s5_b_blocks(bb):
    g, p, h = bb.shape
    t = bb.reshape(g // 8, 8, p, h)
    return jnp.einsum('jiph,ik->jihkp', t, jnp.eye(8, dtype=bb.dtype)).reshape(g * h, 8 * p)


def _s5_b_unblock(m, g, p, h):
    t = m.reshape(g // 8, 8, h, 8, p)
    return jnp.einsum('jihkp,ik->jiph', t, jnp.eye(8, dtype=m.dtype)).reshape(g, p, h)


def _s5_c_blocks(c):
    g, h, p = c.shape
    t = c.reshape(g // 8, 8, h, p)
    return jnp.einsum('jihp,ik->jipkh', t, jnp.eye(8, dtype=c.dtype)).reshape(g * p, 8 * h)


def _s5_c_unblock(m, g, h, p):
    t = m.reshape(g // 8, 8, p, 8, h)
    return jnp.einsum('jipkh,ik->jihp', t, jnp.eye(8, dtype=m.dtype)).reshape(g, h, p)


BIG = ("w_in", "w_branch", "w_out", "w_mlp_in", "w_mlp_out", "s5_w_glu")
SMALL = ("norm1_g", "gm_norm_g", "gm_w_s", "gm_b_s", "s5_lambda_re", "s5_lambda_im", "s5_log_dt",
         "s5_b_re", "s5_b_im", "s5_c_re", "s5_c_im", "s5_d", "s5_b_glu", "norm2_g", "final_g", "b_gate")
WEIGHTS = ("norm1_g", "w_in", "b_gate", "gm_norm_g", "gm_w_s", "gm_b_s", "s5_lambda_re", "s5_lambda_im",
           "s5_log_dt", "s5_b_re", "s5_b_im", "s5_c_re", "s5_c_im", "s5_d", "s5_w_glu", "s5_b_glu",
           "w_branch", "w_out", "norm2_g", "w_mlp_in", "w_mlp_out", "final_g")
FLAT_ROWS = 512


def _pack(arrays):
    flat = jnp.concatenate([a.reshape(-1) for a in arrays])
    unit = FLAT_ROWS * LANES
    pad = (-flat.shape[0]) % unit
    return jnp.pad(flat, (0, pad)).reshape(-1, LANES)


def _unpack(flat2d, shapes):
    flat = flat2d.reshape(-1)
    out, off = [], 0
    for s in shapes:
        n = math.prod(s)
        out.append(flat[off:off + n].reshape(s))
        off += n
    return out


def kernel(x, norm1_g, w_in, b_gate, gm_norm_g, gm_w_s, gm_b_s, s5_lambda_re, s5_lambda_im, s5_log_dt, s5_b_re, s5_b_im, s5_c_re, s5_c_im, s5_d, s5_w_glu, s5_b_glu, w_branch, w_out, norm2_g, w_mlp_in, w_mlp_out, final_g, loss_target, m_norm1_g, m_w_in, m_b_gate, m_gm_norm_g, m_gm_w_s, m_gm_b_s, m_s5_lambda_re, m_s5_lambda_im, m_s5_log_dt, m_s5_b_re, m_s5_b_im, m_s5_c_re, m_s5_c_im, m_s5_d, m_s5_w_glu, m_s5_b_glu, m_w_branch, m_w_out, m_norm2_g, m_w_mlp_in, m_w_mlp_out, m_final_g, v_norm1_g, v_w_in, v_b_gate, v_gm_norm_g, v_gm_w_s, v_gm_b_s, v_s5_lambda_re, v_s5_lambda_im, v_s5_log_dt, v_s5_b_re, v_s5_b_im, v_s5_c_re, v_s5_c_im, v_s5_d, v_s5_w_glu, v_s5_b_glu, v_w_branch, v_w_out, v_norm2_g, v_w_mlp_in, v_w_mlp_out, v_final_g):
    P = dict(locals())
    W = {n: P[n] for n in WEIGHTS}
    M1 = {n: P["m_" + n] for n in WEIGHTS}
    V2 = {n: P["v_" + n] for n in WEIGHTS}

    L, D = x.shape[1], x.shape[2]
    depth = norm1_g.shape[0]
    mix = D // 2
    nb_in, ffb, bw = w_in.shape[2], w_mlp_in.shape[2], w_branch.shape[3]
    ff = ffb * NDEV
    s5_groups = mix // S5_GROUP_CH
    assert mix % S5_UB == 0 and L % CHUNK == 0 and w_in.shape[2] * NDEV == 6 * mix + 3 * D

    xi, yi, ci = _mesh_pos()
    core = ci.astype(jnp.int32).reshape(1)
    chip = (2 * xi + yi).astype(jnp.int32).reshape(1)
    dev = 4 * xi + 2 * yi + ci

    local16 = {n: P[n].astype(BF16) for n in BIG}
    natural = {"w_in": lambda g: g[None], "w_branch": lambda g: g[None], "w_mlp_in": lambda g: g[None],
               "w_out": lambda g: g.reshape(1, D, D), "w_mlp_out": lambda g: g.reshape(1, ff, D),
               "s5_w_glu": lambda g: g.reshape(1, mix, mix)}
    behind_proj = ("w_branch", "w_out", "s5_w_glu")

    first = _run_plan(_gather_plan([(local16["w_in"], 0), (b_gate, None)]), "gather_first")
    w_in_next = natural["w_in"](first[0])
    bg = jnp.swapaxes(first[1], 0, 1)

    m_after = _after_matrix(min(SB_SUB, _tile(L, SB_BLOCK)))

    xcur = x[0]
    saved = []
    for l in range(depth):
        wl = {"w_in": w_in_next}
        h, h_t = _rms_fwd(xcur, norm1_g[l][None], f"rms1_l{l}")
        proj, landed = _mm_cols(f"proj_l{l}", h, wl["w_in"], 0, BF16,
                                plan=_gather_plan([(local16[n], l) for n in behind_proj]))
        wl.update({n: natural[n](g) for n, g in zip(behind_proj, landed)})
        ya = _gmlp_fwd(proj, gm_norm_g[l][None], gm_w_s[l], gm_b_s[l][..., None], mix, f"gmlp_l{l}")
        ab_re, ab_im, bb_re, bb_im = _s5_discretize(s5_lambda_re[l], s5_lambda_im[l], s5_log_dt[l],
                                                    s5_b_re[l], s5_b_im[l])
        s5p = (ab_re.reshape(1, -1), ab_im.reshape(1, -1),
               _s5_b_blocks(bb_re).astype(BF16), _s5_b_blocks(bb_im).astype(BF16),
               _s5_c_blocks(s5_c_re[l]).astype(BF16), _s5_c_blocks(s5_c_im[l]).astype(BF16))
        ypre, sb_re, sb_im, landed = _s5_scan_fwd(proj, *s5p, mix, f"s5scan_l{l}",
                                                  plan=_gather_plan([(local16["w_mlp_in"], l)]))
        wl["w_mlp_in"] = natural["w_mlp_in"](landed[0])
        yb = _s5_glu_fwd(ypre, proj, s5_d[l][None], wl["s5_w_glu"], s5_b_glu[l][None], 0, mix, f"s5glu_l{l}")
        ahead = _gather_plan([(local16["w_in"], l + 1)]) if l + 1 < depth else None
        yc, yc32, landed = _sb_fwd(proj, m_after, mix, f"sb_l{l}", plan=ahead)
        if ahead is not None:
            w_in_next = natural["w_in"](landed[0])
        merged, merged_t = _merge_fwd((ya, yb, yc), wl["w_branch"], proj, bg, 0, l, f"merge_l{l}")
        xmid = _mm_rows(f"wout_l{l}", merged, wl["w_out"], 0, F32, xcur)
        h2, h2_t = _rms_fwd(xmid, norm2_g[l][None], f"rms2_l{l}")
        (r, r_t), landed = _mm_cols(f"mlpin_l{l}", h2, wl["w_mlp_in"], 0, BF16,
                                    epi=lambda acc: jnp.maximum(acc, 0.0), with_transpose=True,
                                    plan=_gather_plan([(local16["w_mlp_out"], l)]))
        wl["w_mlp_out"] = natural["w_mlp_out"](landed[0])
        xout = _mm_rows(f"mlpout_l{l}", r, wl["w_mlp_out"], 0, F32, xmid, a_pro=lambda t: t * t)
        saved.append(dict(x=xcur, h_t=h_t, proj=proj, ys=(ya, yb, yc), s5p=s5p, ypre=ypre, sb=(sb_re, sb_im),
                          yc32=yc32, merged_t=merged_t, xmid=xmid, h2_t=h2_t, r=r, r_t=r_t, w=wl))
        xcur = xout

    loss_tile, dx, dxb, d_final_g = _loss_head(xcur, final_g[None], loss_target[0], "loss_head")
    loss = lax.psum(loss_tile[0, 0], ("x", "y", "c"))

    big_w = {"w_in": (D, nb_in), "w_branch": (3 * mix, bw), "w_out": (D // NDEV, D),
             "w_mlp_in": (D, ffb), "w_mlp_out": (ffb, D), "s5_w_glu": (mix // NDEV, mix)}
    small = {n: [None] * depth for n in SMALL if n != "final_g"}
    pair_sums = [None] * depth
    chip_parts = [None] * depth

    for l in reversed(range(depth)):
        sv = saved[l]
        wl = sv["w"]
        d_a = _mm_rows_t(f"d_act_l{l}", dxb, wl["w_mlp_out"], 0, BF16, extra=(sv["r"],),
                         epi=lambda acc, rv: acc * (2.0 * rv.astype(F32)))
        g_mo = _grad_rows(f"g_mlpout_l{l}", sv["r_t"], dxb, a_pro=lambda t: t * t)
        g_mi = _grad_cols(f"g_mlpin_l{l}", sv["h2_t"], d_a)
        dh2 = _mm_cols_t(f"d_h2_l{l}", d_a, wl["w_mlp_in"], 0, F32)
        dxm, dxmb, small["norm2_g"][l] = _rms_bwd(dh2, sv["xmid"], norm2_g[l][None], dx, f"rms2_bwd_l{l}")
        d_merged = _mm_rows_t(f"d_merged_l{l}", dxmb, wl["w_out"], 0, BF16)
        g_out = _grad_rows(f"g_wout_l{l}", sv["merged_t"], dxmb)
        (dpa, dpb, dpc, dba, dbb, dbc, dya, dyb, dyc, dbg) = _merge_bwd(
            sv["ys"], wl["w_branch"], sv["proj"], bg, d_merged, 0, l, f"merge_bwd_l{l}")
        small["b_gate"][l] = jnp.transpose(dbg, (1, 0, 2)).reshape(3, D)
        g_br = lax.empty((NDEV, 3, mix, bw), BF16)
        for n, dbr in enumerate((dba, dbb, dbc)):
            g_br = _grad_branch(f"g_branch{n}_l{l}", sv["ys"][n], dbr, g_br, n)
        d_uv, d_gn, d_ws, d_bs = _gmlp_bwd(sv["proj"], gm_norm_g[l][None], gm_w_s[l],
                                           jnp.swapaxes(gm_w_s[l], 1, 2), gm_b_s[l][..., None], dya, mix,
                                           f"gmlp_bwd_l{l}")
        small["gm_norm_g"][l], small["gm_w_s"][l], small["gm_b_s"][l] = d_gn, d_ws, d_bs[..., 0]
        d_ypre, d_xin, dw_glu, db_glu, dd = _s5_glu_bwd(sv["ypre"], sv["proj"], s5_d[l][None], wl["s5_w_glu"],
                                                        s5_b_glu[l][None], dyb, 0, mix, f"s5glu_bwd_l{l}")
        g_glu = dw_glu.astype(BF16)
        small["s5_b_glu"][l], small["s5_d"][l] = db_glu, dd
        d_s5in, dbm_re, dbm_im, dcm_re, dcm_im, da_re, da_im = _s5_scan_bwd(
            sv["proj"], *sv["s5p"], *sv["sb"], d_ypre, d_xin, mix, f"s5scan_bwd_l{l}")
        small["s5_c_re"][l] = _s5_c_unblock(dcm_re, s5_groups, S5_GROUP_CH, S5_STATE)
        small["s5_c_im"][l] = _s5_c_unblock(dcm_im, s5_groups, S5_GROUP_CH, S5_STATE)
        _, disc_vjp = jax.vjp(_s5_discretize, s5_lambda_re[l], s5_lambda_im[l], s5_log_dt[l],
                              s5_b_re[l], s5_b_im[l])
        (small["s5_lambda_re"][l], small["s5_lambda_im"][l], small["s5_log_dt"][l],
         small["s5_b_re"][l], small["s5_b_im"][l]) = disc_vjp(
            (da_re.reshape(s5_groups, S5_STATE), da_im.reshape(s5_groups, S5_STATE),
             _s5_b_unblock(dbm_re, s5_groups, S5_STATE, S5_GROUP_CH),
             _s5_b_unblock(dbm_im, s5_groups, S5_STATE, S5_GROUP_CH)))
        above = _chips_plan(pair_sums[l + 1]) if l + 1 < depth else None
        dq, dk, dv, landed = _sb_bwd(sv["proj"], m_after, sv["yc32"], dyc, mix, f"sb_bwd_l{l}", plan=above)
        if above is not None:
            chip_parts[l + 1] = landed
        dproj = jnp.concatenate([d_uv, d_s5in, dq.astype(BF16), dk.astype(BF16), dv.astype(BF16),
                                 dpa, dpb, dpc], axis=1)
        g_in = _grad_cols(f"g_win_l{l}", sv["h_t"], dproj)
        dh = _mm_cols_t(f"d_h_l{l}", dproj, wl["w_in"], 0, F32)
        dx, dxb, small["norm1_g"][l] = _rms_bwd(dh, sv["x"], norm1_g[l][None], dxm, f"rms1_bwd_l{l}")
        parts = {"w_in": g_in, "w_branch": g_br, "w_out": g_out, "w_mlp_in": g_mi, "w_mlp_out": g_mo,
                 "s5_w_glu": g_glu}
        gs = [parts[n].reshape(NDEV, *big_w[n]) for n in BIG]
        r1 = _run_plan(_pair_plan(gs), f"reduce_pair_l{l}")
        pair_sums[l] = [_pair_add(g, rr, core, f"pair_add_{n}_l{l}") for g, rr, n in zip(gs, r1, BIG)]

    grad_x = dx[None]
    chip_parts[0] = _run_plan(_chips_plan(pair_sums[0]), "reduce_chips_l0")

    grads, deltas, new_m, new_v = {}, {}, {}, {}
    for t, n in enumerate(BIG):
        shp = (depth, *big_w[n])
        outs = [lax.empty(shp, F32) for _ in range(4)]
        for l in range(depth):
            outs = _adamw_big(pair_sums[l][t], chip_parts[l][t], chip, W[n].reshape(shp), M1[n].reshape(shp),
                              V2[n].reshape(shp), l, outs, f"adamw_{n}_l{l}")
        grads[n], deltas[n], new_m[n], new_v[n] = (o.reshape(W[n].shape) for o in outs)

    small_full = {n: jnp.stack(v) for n, v in small.items()}
    small_full["final_g"] = d_final_g[0]
    full_shapes = [W[n].shape if n != "b_gate" else (depth, 3, D) for n in SMALL]
    packed = _pack([small_full[n].reshape(s) for n, s in zip(SMALL, full_shapes)])
    gathered = _run_plan(_gather_plan([(packed, None)]), "gather_small")[0]
    summed = _unpack(_sum_devices(gathered, "sum_small"), full_shapes)
    for n, g in zip(SMALL, summed):
        grads[n] = g if n != "b_gate" else lax.dynamic_slice_in_dim(g, dev * bw, bw, axis=2)
    shapes = [W[n].shape for n in SMALL]
    d_s, m_s, v_s = _adamw_flat(_pack([grads[n] for n in SMALL]), _pack([W[n] for n in SMALL]),
                                _pack([M1[n] for n in SMALL]), _pack([V2[n] for n in SMALL]), "adamw_small")
    for n, d, mm, vv in zip(SMALL, _unpack(d_s, shapes), _unpack(m_s, shapes), _unpack(v_s, shapes)):
        deltas[n], new_m[n], new_v[n] = d, mm, vv

    return (loss, grad_x, *[grads[n] for n in WEIGHTS], *[deltas[n] for n in WEIGHTS],
            *[new_m[n] for n in WEIGHTS], *[new_v[n] for n in WEIGHTS])
```

```python
import functools
import math
from typing import Callable, NamedTuple

import jax
import jax.numpy as jnp
from jax import lax
from jax.experimental import pallas as pl
from jax.experimental.pallas import tpu as pltpu

F32 = jnp.float32
BF16 = jnp.bfloat16
MESH = pl.DeviceIdType.MESH
NDEV = 8
NCHIP = 4

EPS = 1e-6
CHUNK = 128
S5_GROUP_CH = 16
S5_STATE = 64
HEAD_DIM = 128
DT_MIN = 1e-3
DT_MAX = 1e-1
ADAM_LR = 0.001
ADAM_B1 = 0.9
ADAM_B2 = 0.999
ADAM_EPS = 1e-08
ADAM_WD = 0.01
ADAM_STEP = 10

V7X_VMEM_BYTES = 64 * 2**20
VMEM_LIMIT_BYTES = V7X_VMEM_BYTES - 8 * 2**20
SUBLANES = 8
LANES = 128

NN = (((1,), (0,)), ((), ()))
NT = (((1,), (1,)), ((), ()))
TN = (((0,), (0,)), ((), ()))

HBM = pl.BlockSpec(memory_space=pltpu.HBM)
ANY = pl.BlockSpec(memory_space=pl.ANY)


def _cp(**kw):
    return pltpu.CompilerParams(vmem_limit_bytes=VMEM_LIMIT_BYTES, **kw)


def _tile(n, pref, align=SUBLANES):
    if n <= pref:
        return n
    t = (pref // align) * align
    while t >= align:
        if n % t == 0:
            return t
        t -= align
    return n


def _dot(a, b, dims=NN):
    return lax.dot_general(a, b, dims, preferred_element_type=F32)


def _sds(shape, dtype):
    return jax.ShapeDtypeStruct(tuple(shape), dtype)


def _mesh_pos():
    return lax.axis_index("x"), lax.axis_index("y"), lax.axis_index("c")


class _Plan(NamedTuple):
    operands: list
    out_shape: list
    scratch: list
    start: Callable
    finish: Callable


def _gather_plan(sources):
    n = len(sources)

    def copies(ins, outs, sems):
        send_sems, recv_sems, local_sems = sems
        x, y, c = _mesh_pos()
        me, sib = (x, y, c), (x, y, 1 - c)
        chips = [(1 - x, y), (x, 1 - y), (1 - x, 1 - y)]

        def src(t):
            layer = sources[t][1]
            return ins[t] if layer is None else ins[t].at[layer]

        def slot(t, p):
            return outs[t].at[4 * p[0] + 2 * p[1] + p[2]]

        def copy(t, k, block, to, own=False):
            dst = slot(t, block)
            return pltpu.make_async_remote_copy(
                src_ref=src(t) if own else dst, dst_ref=dst,
                send_sem=send_sems.at[t, k], recv_sem=recv_sems.at[t, k],
                device_id=to, device_id_type=MESH)

        mine = [pltpu.make_async_copy(src(t), slot(t, me), local_sems.at[t]) for t in range(n)]
        first = []
        for t in range(n):
            first.append(copy(t, 0, me, sib, own=True))
            first += [copy(t, 1 + j, me, (*chip, c), own=True) for j, chip in enumerate(chips)]
        return me, sib, c, chips, copy, mine, first

    def start(ins, outs, sems):
        *_, mine, first = copies(ins, outs, sems)
        for cp in mine + first:
            cp.start()

    def finish(ins, outs, sems):
        me, sib, c, chips, copy, mine, first = copies(ins, outs, sems)
        passed = []
        for j, chip in enumerate(chips):
            for t in range(n):
                copy(t, 1 + j, (*chip, c), me).wait_recv()
                fw = copy(t, 4 + j, (*chip, c), sib)
                fw.start()
                passed.append(fw)
        for t in range(n):
            copy(t, 0, sib, me).wait_recv()
            for j, chip in enumerate(chips):
                copy(t, 4 + j, (*chip, 1 - c), me).wait_recv()
        for cp in first + passed:
            cp.wait_send()
        for cp in mine:
            cp.wait()

    return _Plan(
        operands=[a for a, _ in sources],
        out_shape=[_sds((NDEV,) + (a.shape if layer is None else a.shape[1:]), a.dtype) for a, layer in sources],
        scratch=[pltpu.SemaphoreType.DMA((n, 7)), pltpu.SemaphoreType.DMA((n, 7)), pltpu.SemaphoreType.DMA((n,))],
        start=start, finish=finish)


def _pair_plan(gs):
    n = len(gs)

    def copies(ins, outs, sems):
        send_sems, recv_sems = sems
        x, y, c = _mesh_pos()
        return [pltpu.make_async_remote_copy(
            src_ref=ins[t].at[2 * p + (1 - c)], dst_ref=outs[t].at[p],
            send_sem=send_sems.at[t, p], recv_sem=recv_sems.at[t, p],
            device_id=(x, y, 1 - c), device_id_type=MESH) for t in range(n) for p in range(NCHIP)]

    def start(ins, outs, sems):
        for cp in copies(ins, outs, sems):
            cp.start()

    def finish(ins, outs, sems):
        for cp in copies(ins, outs, sems):
            cp.wait()

    return _Plan(operands=list(gs), out_shape=[_sds((NCHIP,) + g.shape[1:], g.dtype) for g in gs],
                 scratch=[pltpu.SemaphoreType.DMA((n, NCHIP)), pltpu.SemaphoreType.DMA((n, NCHIP))],
                 start=start, finish=finish)


def _chips_plan(ss):
    n = len(ss)

    def copies(ins, outs, sems):
        send_sems, recv_sems = sems
        x, y, c = _mesh_pos()
        chips = [(1 - x, y), (x, 1 - y), (1 - x, 1 - y)]
        return [pltpu.make_async_remote_copy(
            src_ref=ins[t].at[2 * chip[0] + chip[1]], dst_ref=outs[t].at[j],
            send_sem=send_sems.at[t, j], recv_sem=recv_sems.at[t, j],
            device_id=(*chip, c), device_id_type=MESH) for t in range(n) for j, chip in enumerate(chips)]

    def start(ins, outs, sems):
        for cp in copies(ins, outs, sems):
            cp.start()

    def finish(ins, outs, sems):
        for cp in copies(ins, outs, sems):
            cp.wait()

    return _Plan(operands=list(ss), out_shape=[_sds((3,) + s.shape[1:], s.dtype) for s in ss],
                 scratch=[pltpu.SemaphoreType.DMA((n, 3)), pltpu.SemaphoreType.DMA((n, 3))],
                 start=start, finish=finish)


def _run_plan(plan, name):
    n_in, n_out = len(plan.operands), len(plan.out_shape)

    def body(*refs):
        ins, outs, sems = refs[:n_in], refs[n_in:n_in + n_out], refs[n_in + n_out:]
        plan.start(ins, outs, sems)
        plan.finish(ins, outs, sems)

    return list(pl.pallas_call(
        body, name=name, out_shape=plan.out_shape, in_specs=[HBM] * n_in, out_specs=[HBM] * n_out,
        scratch_shapes=plan.scratch)(*plan.operands))


def _rows_tile(r, c, itemsize=4):
    return _tile(r, max(SUBLANES, (2**20 // itemsize) // c))


def _pair_add(g, r1, core, name):
    _, r, c = g.shape
    tr = _rows_tile(r, c)
    g4 = g.reshape(NCHIP, 2, r, c)

    def body(core_ref, g_ref, r_ref, o_ref):
        o_ref[...] = (g_ref[...].astype(F32) + r_ref[...].astype(F32)).astype(o_ref.dtype)

    return pl.pallas_call(
        body, name=name,
        grid_spec=pltpu.PrefetchScalarGridSpec(
            num_scalar_prefetch=1, grid=(NCHIP, r // tr),
            in_specs=[pl.BlockSpec((None, None, tr, c), lambda p, i, cr: (p, cr[0], i, 0)),
                      pl.BlockSpec((None, tr, c), lambda p, i, cr: (p, i, 0))],
            out_specs=pl.BlockSpec((None, tr, c), lambda p, i, cr: (p, i, 0))),
        out_shape=_sds((NCHIP, r, c), BF16),
        compiler_params=_cp(),
    )(core, g4, r1)


def _adamw_math(g, w, m, v):
    m = ADAM_B1 * m + (1.0 - ADAM_B1) * g
    v = ADAM_B2 * v + (1.0 - ADAM_B2) * (g * g)
    m_hat = m / (1.0 - ADAM_B1 ** ADAM_STEP)
    v_hat = v / (1.0 - ADAM_B2 ** ADAM_STEP)
    delta = -ADAM_LR * (m_hat / (jnp.sqrt(v_hat) + ADAM_EPS) + ADAM_WD * w)
    return delta, m, v


def _adamw_big(s, r2, chip, w, m, v, layer, into, name):
    _, r, c = w.shape
    tr = _tile(r, max(SUBLANES, _rows_tile(r, c) // 2))

    def body(chip_ref, s_ref, ra_ref, rb_ref, rc_ref, w_ref, m_ref, v_ref, *rest):
        g_out, d_out, m_out, v_out = rest[4:]
        g = ((s_ref[...].astype(F32) + ra_ref[...].astype(F32)) + rb_ref[...].astype(F32)) + rc_ref[...].astype(F32)
        d, mm, vv = _adamw_math(g, w_ref[...], m_ref[...], v_ref[...])
        g_out[...] = g
        d_out[...] = d
        m_out[...] = mm
        v_out[...] = vv

    wspec = pl.BlockSpec((None, tr, c), lambda i, cr: (layer, i, 0))
    rspec = [pl.BlockSpec((None, tr, c), functools.partial(lambda i, cr, j: (j, i, 0), j=j)) for j in range(3)]
    return pl.pallas_call(
        body, name=name,
        grid_spec=pltpu.PrefetchScalarGridSpec(
            num_scalar_prefetch=1, grid=(r // tr,),
            in_specs=[pl.BlockSpec((None, tr, c), lambda i, cr: (cr[0], i, 0)),
                      *rspec, wspec, wspec, wspec, ANY, ANY, ANY, ANY],
            out_specs=[wspec] * 4),
        out_shape=[_sds(w.shape, F32)] * 4,
        input_output_aliases={8: 0, 9: 1, 10: 2, 11: 3},
        compiler_params=_cp(),
    )(chip, s, r2, r2, r2, w, m, v, *into)


def _sum_devices(parts, name):
    _, r, c = parts.shape
    tr = _tile(r, 512)

    def body(p_ref, o_ref):
        acc = p_ref[0]
        for k in range(1, NDEV):
            acc = acc + p_ref[k]
        o_ref[...] = acc

    return pl.pallas_call(
        body, name=name, grid=(r // tr,),
        in_specs=[pl.BlockSpec((NDEV, tr, c), lambda i: (0, i, 0))],
        out_specs=pl.BlockSpec((tr, c), lambda i: (i, 0)),
        out_shape=_sds((r, c), F32), compiler_params=_cp(),
    )(parts)


def _adamw_flat(g, w, m, v, name):
    r, c = w.shape
    tr = _tile(r, 512)

    def body(g_ref, w_ref, m_ref, v_ref, d_out, m_out, v_out):
        d, mm, vv = _adamw_math(g_ref[...], w_ref[...], m_ref[...], v_ref[...])
        d_out[...] = d
        m_out[...] = mm
        v_out[...] = vv

    spec = pl.BlockSpec((tr, c), lambda i: (i, 0))
    return pl.pallas_call(
        body, name=name, grid=(r // tr,), in_specs=[spec] * 4, out_specs=[spec] * 3,
        out_shape=[_sds(w.shape, F32)] * 3, compiler_params=_cp(),
    )(g, w, m, v)


def _matmul(name, a, b, *, grid, a_spec, b_spec, o_spec, out_shape, dims, acc_shape,
            extra=(), extra_specs=(), a_pro=None, epi=None, into=None, t_spec=None, t_shape=None, plan=None):
    nk = grid[2]
    n_extra = len(extra)
    n_out = 1 + (t_spec is not None)
    n_scratch = int(nk > 1)
    n_pi, n_po = (len(plan.operands), len(plan.out_shape)) if plan is not None else (0, 0)

    def body(*refs):
        a_ref, b_ref = refs[0], refs[1]
        ex = refs[2:2 + n_extra]
        p_ins, outs, p_outs, scratch, p_sems = _split_plan_refs(
            refs[2 + n_extra + (into is not None):], n_pi, n_out, n_po, n_scratch)
        o_ref = outs[0]
        i, j, k = pl.program_id(0), pl.program_id(1), pl.program_id(2)

        if plan is not None:
            @pl.when((i == 0) & (j == 0) & (k == 0))
            def _():
                plan.start(p_ins, p_outs, p_sems)

        def product():
            av = a_ref[...]
            if a_pro is not None:
                av = a_pro(av)
            return _dot(av, b_ref[...], dims)

        def finish(r):
            if epi is not None:
                r = epi(r, *[e[...] for e in ex])
            o_ref[...] = r.astype(o_ref.dtype)
            if t_spec is not None:
                outs[1][...] = r.T.astype(outs[1].dtype)

        if nk == 1:
            finish(product())
        else:
            acc_ref = scratch[0]

            @pl.when(k == 0)
            def _():
                acc_ref[...] = jnp.zeros_like(acc_ref)

            acc_ref[...] += product()

            @pl.when(k == nk - 1)
            def _():
                finish(acc_ref[...])

        if plan is not None:
            @pl.when((i == grid[0] - 1) & (j == grid[1] - 1) & (k == nk - 1))
            def _():
                plan.finish(p_ins, p_outs, p_sems)

    operands = [a, b, *extra]
    in_specs = [a_spec, b_spec, *extra_specs]
    aliases = {}
    if into is not None:
        operands.append(into)
        in_specs.append(ANY)
        aliases = {len(operands) - 1: 0}
        out_shape = _sds(into.shape, into.dtype)
    o_specs, out_shapes = [o_spec], [out_shape]
    if t_spec is not None:
        o_specs, out_shapes = o_specs + [t_spec], out_shapes + [t_shape]
    scratch_shapes = [] if nk == 1 else [pltpu.VMEM(acc_shape, F32)]
    if plan is not None:
        operands, in_specs = operands + plan.operands, in_specs + [HBM] * n_pi
        o_specs, out_shapes = o_specs + [HBM] * n_po, out_shapes + plan.out_shape
        scratch_shapes = scratch_shapes + plan.scratch
    res = pl.pallas_call(
        body, name=name, grid=grid, in_specs=in_specs, out_specs=o_specs, out_shape=out_shapes,
        scratch_shapes=scratch_shapes, input_output_aliases=aliases, compiler_params=_cp(),
    )(*operands)
    main = res[0] if n_out == 1 else list(res[:n_out])
    return main if plan is None else (main, list(res[n_out:]))


def _rms_fwd(x, g, name):
    L, D = x.shape
    tm = _tile(L, 256)

    def body(x_ref, g_ref, o_ref, ot_ref):
        xf = x_ref[...]
        rstd = lax.rsqrt(jnp.mean(xf * xf, axis=-1, keepdims=True) + EPS)
        y = xf * rstd * g_ref[...]
        o_ref[...] = y.astype(o_ref.dtype)
        ot_ref[...] = y.T.astype(ot_ref.dtype)

    return pl.pallas_call(
        body, name=name, grid=(L // tm,),
        in_specs=[pl.BlockSpec((tm, D), lambda i: (i, 0)), pl.BlockSpec((1, D), lambda i: (0, 0))],
        out_specs=[pl.BlockSpec((tm, D), lambda i: (i, 0)), pl.BlockSpec((D, tm), lambda i: (0, i))],
        out_shape=[_sds((L, D), BF16), _sds((D, L), BF16)], compiler_params=_cp(),
    )(x, g)


def _rms_bwd(dh, x, g, dres, name):
    L, D = x.shape
    tm = _tile(L, 256)

    def body(dh_ref, x_ref, g_ref, dr_ref, dx_ref, dxb_ref, dg_ref):
        @pl.when(pl.program_id(0) == 0)
        def _():
            dg_ref[...] = jnp.zeros_like(dg_ref)

        xf = x_ref[...]
        dhf = dh_ref[...].astype(F32)
        rstd = lax.rsqrt(jnp.mean(xf * xf, axis=-1, keepdims=True) + EPS)
        xhat = xf * rstd
        dg_ref[...] += jnp.sum(dhf * xhat, axis=0, keepdims=True)
        dxh = dhf * g_ref[...]
        dx = dr_ref[...] + rstd * (dxh - xhat * jnp.mean(dxh * xhat, axis=-1, keepdims=True))
        dx_ref[...] = dx
        dxb_ref[...] = dx.astype(BF16)

    row = pl.BlockSpec((tm, D), lambda i: (i, 0))
    vec = pl.BlockSpec((1, D), lambda i: (0, 0))
    return pl.pallas_call(
        body, name=name, grid=(L // tm,), in_specs=[row, row, vec, row], out_specs=[row, row, vec],
        out_shape=[_sds((L, D), F32), _sds((L, D), BF16), _sds((1, D), F32)], compiler_params=_cp(),
    )(dh, x, g, dres)


def _loss_head(x, g, target, name):
    L, D = x.shape
    tm = _tile(L, 256)

    def body(x_ref, g_ref, t_ref, loss_ref, dx_ref, dxb_ref, dg_ref):
        @pl.when(pl.program_id(0) == 0)
        def _():
            dg_ref[...] = jnp.zeros_like(dg_ref)
            loss_ref[...] = jnp.zeros_like(loss_ref)

        xf = x_ref[...]
        gv = g_ref[...]
        rstd = lax.rsqrt(jnp.mean(xf * xf, axis=-1, keepdims=True) + EPS)
        xhat = xf * rstd
        err = xhat * gv - t_ref[...]
        part = jnp.sum(jnp.sum(err * err, axis=-1, keepdims=True), axis=0, keepdims=True)
        loss_ref[...] += jnp.broadcast_to(part * (0.5 / D), loss_ref.shape)
        dy = err * (1.0 / D)
        dg_ref[...] += jnp.sum(dy * xhat, axis=0, keepdims=True)
        dxh = dy * gv
        dx = rstd * (dxh - xhat * jnp.mean(dxh * xhat, axis=-1, keepdims=True))
        dx_ref[...] = dx
        dxb_ref[...] = dx.astype(BF16)

    row = pl.BlockSpec((tm, D), lambda i: (i, 0))
    vec = pl.BlockSpec((1, D), lambda i: (0, 0))
    lspec = pl.BlockSpec((SUBLANES, LANES), lambda i: (0, 0))
    return pl.pallas_call(
        body, name=name, grid=(L // tm,), in_specs=[row, vec, row], out_specs=[lspec, row, row, vec],
        out_shape=[_sds((SUBLANES, LANES), F32), _sds((L, D), F32), _sds((L, D), BF16), _sds((1, D), F32)],
        compiler_params=_cp(),
    )(x, g, target)


_GELU_C = math.sqrt(2.0 / math.pi)
_GELU_K = 0.044715


def _gelu(x):
    return 0.5 * x * (1.0 + jnp.tanh(_GELU_C * (x + _GELU_K * (x * x * x))))


def _gelu_and_grad(x):
    t = jnp.tanh(_GELU_C * (x + _GELU_K * (x * x * x)))
    val = 0.5 * x * (1.0 + t)
    grad = 0.5 * (1.0 + t) + 0.5 * x * (1.0 - t * t) * (_GELU_C * (1.0 + 3.0 * _GELU_K * (x * x)))
    return val, grad


def _sigmoid(x):
    e = jnp.exp(-jnp.abs(x))
    return jnp.where(x >= 0, 1.0, e) / (1.0 + e)


def _tril_mask():
    r = lax.broadcasted_iota(jnp.int32, (CHUNK, CHUNK), 0)
    c = lax.broadcasted_iota(jnp.int32, (CHUNK, CHUNK), 1)
    return r >= c


def _gmlp_fwd(proj, norm_g, w_s, b_col, mix, name):
    L = proj.shape[0]
    groups = mix // CHUNK
    tt = _tile(L, 2 * CHUNK)

    def body(uv_ref, g_ref, w_ref, b_ref, o_ref):
        z = _gelu(uv_ref[...].astype(F32))
        u, v = z[:, :mix], z[:, mix:]
        vn = v * lax.rsqrt(jnp.mean(v * v, axis=-1, keepdims=True) + EPS) * g_ref[...]
        mask = _tril_mask()
        for gi in range(groups):
            wt = jnp.where(mask, w_ref[gi], 0.0).astype(BF16)
            cols = slice(gi * CHUNK, (gi + 1) * CHUNK)
            for cc in range(tt // CHUNK):
                rows = slice(cc * CHUNK, (cc + 1) * CHUNK)
                mixed = _dot(wt, vn[rows, cols].astype(BF16)) + b_ref[gi]
                o_ref[rows, cols] = (u[rows, cols] * mixed).astype(o_ref.dtype)

    return pl.pallas_call(
        body, name=name, grid=(L // tt,),
        in_specs=[pl.BlockSpec((tt, 2 * mix), lambda i: (i, 0)),
                  pl.BlockSpec((1, mix), lambda i: (0, 0)),
                  pl.BlockSpec((groups, CHUNK, CHUNK), lambda i: (0, 0, 0)),
                  pl.BlockSpec((groups, CHUNK, 1), lambda i: (0, 0, 0))],
        out_specs=pl.BlockSpec((tt, mix), lambda i: (i, 0)),
        out_shape=_sds((L, mix), BF16), compiler_params=_cp(),
    )(proj, norm_g, w_s, b_col)


def _gmlp_bwd(proj, norm_g, w_s, w_st, b_col, dy, mix, name):
    L = proj.shape[0]
    groups = mix // CHUNK
    tt = _tile(L, 2 * CHUNK)

    def body(uv_ref, g_ref, w_ref, wt_ref, b_ref, dy_ref, duv_ref, dg_ref, dw_ref, db_ref, du_s, dvn_s):
        @pl.when(pl.program_id(0) == 0)
        def _():
            dg_ref[...] = jnp.zeros_like(dg_ref)
            dw_ref[...] = jnp.zeros_like(dw_ref)
            db_ref[...] = jnp.zeros_like(db_ref)

        z, zgrad = _gelu_and_grad(uv_ref[...].astype(F32))
        u, v = z[:, :mix], z[:, mix:]
        rstd = lax.rsqrt(jnp.mean(v * v, axis=-1, keepdims=True) + EPS)
        vhat = v * rstd
        gv = g_ref[...]
        vn = vhat * gv
        dyf = dy_ref[...].astype(F32)
        mask = _tril_mask()
        r = lax.broadcasted_iota(jnp.int32, (CHUNK, CHUNK), 0)
        c = lax.broadcasted_iota(jnp.int32, (CHUNK, CHUNK), 1)
        for gi in range(groups):
            w_low = jnp.where(mask, w_ref[gi], 0.0).astype(BF16)
            w_up = jnp.where(r <= c, wt_ref[gi], 0.0).astype(BF16)
            cols = slice(gi * CHUNK, (gi + 1) * CHUNK)
            for cc in range(tt // CHUNK):
                rows = slice(cc * CHUNK, (cc + 1) * CHUNK)
                vnb = vn[rows, cols].astype(BF16)
                mixed = _dot(w_low, vnb) + b_ref[gi]
                dyb = dyf[rows, cols]
                dm = dyb * u[rows, cols]
                dmb = dm.astype(BF16)
                du_s[rows, cols] = dyb * mixed
                dvn_s[rows, cols] = _dot(w_up, dmb)
                dw_ref[gi] += jnp.where(mask, _dot(dmb, vnb, NT), 0.0)
                db_ref[gi] += jnp.sum(dm, axis=1, keepdims=True)
        dvn = dvn_s[...]
        dg_ref[...] += jnp.sum(dvn * vhat, axis=0, keepdims=True)
        dvh = dvn * gv
        dv = rstd * (dvh - vhat * jnp.mean(dvh * vhat, axis=-1, keepdims=True))
        duv_ref[:, :mix] = (du_s[...] * zgrad[:, :mix]).astype(duv_ref.dtype)
        duv_ref[:, mix:] = (dv * zgrad[:, mix:]).astype(duv_ref.dtype)

    wspec = pl.BlockSpec((groups, CHUNK, CHUNK), lambda i: (0, 0, 0))
    bspec = pl.BlockSpec((groups, CHUNK, 1), lambda i: (0, 0, 0))
    gspec = pl.BlockSpec((1, mix), lambda i: (0, 0))
    return pl.pallas_call(
        body, name=name, grid=(L // tt,),
        in_specs=[pl.BlockSpec((tt, 2 * mix), lambda i: (i, 0)), gspec, wspec, wspec, bspec,
                  pl.BlockSpec((tt, mix), lambda i: (i, 0))],
        out_specs=[pl.BlockSpec((tt, 2 * mix), lambda i: (i, 0)), gspec, wspec, bspec],
        out_shape=[_sds((L, 2 * mix), BF16), _sds((1, mix), F32),
                   _sds((groups, CHUNK, CHUNK), F32), _sds((groups, CHUNK, 1), F32)],
        scratch_shapes=[pltpu.VMEM((tt, mix), F32), pltpu.VMEM((tt, mix), F32)],
        compiler_params=_cp(),
    )(proj, norm_g, w_s, w_st, b_col, dy)


S5_CB = 512
S5_UB = 128
S5_LEVELS = (1, 2, 4)


def _s5_fwd_consts(ar, ai):
    rows = lax.broadcasted_iota(jnp.int32, ar.shape, 0)
    out = []
    for d in S5_LEVELS:
        m = rows >= d
        out.append((jnp.where(m, ar, 0.0), jnp.where(m, ai, 0.0)))
        ar, ai = ar * ar - ai * ai, 2.0 * ar * ai
    return out


def _s5_rev_consts(ar, ai):
    rows = lax.broadcasted_iota(jnp.int32, ar.shape, 0)
    out = []
    for d in S5_LEVELS:
        m = rows < SUBLANES - d
        out.append((jnp.where(m, ar, 0.0), jnp.where(m, ai, 0.0)))
        ar, ai = ar * ar - ai * ai, 2.0 * ar * ai
    return out


def _scan8(xr, xi, consts, reverse):
    for (cr, ci), d in zip(consts, S5_LEVELS):
        sh = SUBLANES - d if reverse else d
        pr = pltpu.roll(xr, sh, 0)
        pi = pltpu.roll(xi, sh, 0)
        xr, xi = xr + (cr * pr - ci * pi), xi + (cr * pi + ci * pr)
    return xr, xi


def _s5_forward_block(u_ref, bre_ref, bim_ref, sr, si, ar, ai, pwr, pwi, carry, tt):
    consts = _s5_fwd_consts(ar, ai)
    u = u_ref[...]
    sr[...] = _dot(u, bre_ref[...])
    si[...] = _dot(u, bim_ref[...])

    def step(r, cs):
        cr, ci = cs
        o = pl.multiple_of(r * SUBLANES, SUBLANES)
        xr, xi = _scan8(sr[pl.ds(o, SUBLANES), :], si[pl.ds(o, SUBLANES), :], consts, False)
        xr, xi = xr + (pwr * cr - pwi * ci), xi + (pwr * ci + pwi * cr)
        sr[pl.ds(o, SUBLANES), :] = xr
        si[pl.ds(o, SUBLANES), :] = xi
        return (jnp.broadcast_to(sr[pl.ds(o + SUBLANES - 1, 1), :], xr.shape),
                jnp.broadcast_to(si[pl.ds(o + SUBLANES - 1, 1), :], xi.shape))

    return lax.fori_loop(0, tt // SUBLANES, step, carry)


def _s5_powers(ar, ai):
    rows = lax.broadcasted_iota(jnp.int32, ar.shape, 0)
    return _scan8(jnp.where(rows == 0, ar, 0.0), jnp.where(rows == 0, ai, 0.0), _s5_fwd_consts(ar, ai), False)


def _s5_scan_fwd(proj, a_re, a_im, b_re, b_im, c_re, c_im, mix, name, plan=None):
    L = proj.shape[0]
    S = a_re.shape[1]
    nj = mix // S5_UB
    tt = _tile(L, 256)
    ni = L // tt
    ucol = 2 * mix // S5_UB

    n_pi, n_po = (len(plan.operands), len(plan.out_shape)) if plan is not None else (0, 0)

    def body(u_ref, ar_ref, ai_ref, bre_ref, bim_ref, cre_ref, cim_ref, *rest):
        p_ins, (y_ref, sr, si), p_outs, (car, cai), p_sems = _split_plan_refs(rest, n_pi, 3, n_po, 2)
        j, i = pl.program_id(0), pl.program_id(1)
        ar = jnp.broadcast_to(ar_ref[...], (SUBLANES, S5_CB))
        ai = jnp.broadcast_to(ai_ref[...], (SUBLANES, S5_CB))

        if plan is not None:
            @pl.when((j == 0) & (i == 0))
            def _():
                plan.start(p_ins, p_outs, p_sems)

        @pl.when(i == 0)
        def _():
            car[...] = jnp.zeros_like(car)
            cai[...] = jnp.zeros_like(cai)

        pwr, pwi = _s5_powers(ar, ai)
        cr, ci = _s5_forward_block(u_ref, bre_ref, bim_ref, sr, si, ar, ai, pwr, pwi,
                                   (car[...], cai[...]), tt)
        car[...] = cr
        cai[...] = ci
        y_ref[...] = _dot(sr[...].astype(BF16), cre_ref[...]) - _dot(si[...].astype(BF16), cim_ref[...])

        if plan is not None:
            @pl.when((j == nj - 1) & (i == ni - 1))
            def _():
                plan.finish(p_ins, p_outs, p_sems)

    avec = pl.BlockSpec((1, S5_CB), lambda j, i: (0, j))
    bspec = pl.BlockSpec((S5_UB, S5_CB), lambda j, i: (j, 0))
    cspec = pl.BlockSpec((S5_CB, S5_UB), lambda j, i: (j, 0))
    states = pl.BlockSpec((tt, S5_CB), lambda j, i: (i, j))
    res = pl.pallas_call(
        body, name=name, grid=(nj, ni),
        in_specs=[pl.BlockSpec((tt, S5_UB), lambda j, i: (i, ucol + j)), avec, avec, bspec, bspec, cspec, cspec]
        + [HBM] * n_pi,
        out_specs=[pl.BlockSpec((tt, S5_UB), lambda j, i: (i, j)), states, states] + [HBM] * n_po,
        out_shape=[_sds((L, mix), F32), _sds((L, S), F32), _sds((L, S), F32)]
        + (plan.out_shape if plan is not None else []),
        scratch_shapes=[pltpu.VMEM((SUBLANES, S5_CB), F32), pltpu.VMEM((SUBLANES, S5_CB), F32)]
        + (plan.scratch if plan is not None else []),
        compiler_params=_cp(),
    )(proj, a_re, a_im, b_re, b_im, c_re, c_im, *(plan.operands if plan is not None else []))
    return res[0], res[1], res[2], list(res[3:])


def _s5_scan_bwd(proj, a_re, a_im, b_re, b_im, c_re, c_im, s_re, s_im, dy, dxin, mix, name):
    L = proj.shape[0]
    S = a_re.shape[1]
    nj = mix // S5_UB
    tt = _tile(L, 256)
    ni = L // tt
    ucol = 2 * mix // S5_UB
    nb = tt // SUBLANES

    def body(u_ref, ar_ref, ai_ref, bre_ref, bim_ref, cre_ref, cim_ref, sr, si, dy_ref, dx_ref,
             du_ref, dbr_ref, dbi_ref, dcr_ref, dci_ref, dar_ref, dai_ref,
             gr, gi, car, cai, acr, aci):
        first = pl.program_id(1) == 0
        ar = jnp.broadcast_to(ar_ref[...], (SUBLANES, S5_CB))
        ai = jnp.broadcast_to(ai_ref[...], (SUBLANES, S5_CB))

        @pl.when(first)
        def _():
            for ref in (car, cai, acr, aci, dbr_ref, dbi_ref, dcr_ref, dci_ref):
                ref[...] = jnp.zeros_like(ref)

        dyb = dy_ref[...].astype(BF16)
        gr[...] = _dot(dyb, cre_ref[...], NT)
        gi[...] = -_dot(dyb, cim_ref[...], NT)

        nai = -ai
        consts = _s5_rev_consts(ar, nai)
        rows = lax.broadcasted_iota(jnp.int32, ar.shape, 0)
        last = rows == SUBLANES - 1
        qr, qi = _scan8(jnp.where(last, ar, 0.0), jnp.where(last, nai, 0.0), consts, True)

        def step(k, cs):
            cr, ci, dr, di = cs
            o = pl.multiple_of((nb - 1 - k) * SUBLANES, SUBLANES)
            xr, xi = _scan8(gr[pl.ds(o, SUBLANES), :], gi[pl.ds(o, SUBLANES), :], consts, True)
            xr, xi = xr + (qr * cr - qi * ci), xi + (qr * ci + qi * cr)
            gr[pl.ds(o, SUBLANES), :] = xr
            gi[pl.ds(o, SUBLANES), :] = xi
            hr = jnp.where(last, cr, pltpu.roll(xr, SUBLANES - 1, 0))
            hi = jnp.where(last, ci, pltpu.roll(xi, SUBLANES - 1, 0))
            s_r = sr[pl.ds(o, SUBLANES), :]
            s_i = si[pl.ds(o, SUBLANES), :]
            dr = dr + (hr * s_r + hi * s_i)
            di = di + (hi * s_r - hr * s_i)
            return (jnp.broadcast_to(gr[pl.ds(o, 1), :], xr.shape),
                    jnp.broadcast_to(gi[pl.ds(o, 1), :], xi.shape), dr, di)

        cr, ci, dr, di = lax.fori_loop(0, nb, step, (car[...], cai[...], acr[...], aci[...]))
        car[...] = cr
        cai[...] = ci
        acr[...] = dr
        aci[...] = di
        dar_ref[...] = jnp.sum(dr, axis=0, keepdims=True)
        dai_ref[...] = jnp.sum(di, axis=0, keepdims=True)

        u = u_ref[...]
        gbr = gr[...].astype(BF16)
        gbi = gi[...].astype(BF16)
        dbr_ref[...] += _dot(u, gbr, TN)
        dbi_ref[...] += _dot(u, gbi, TN)
        du = _dot(gbr, bre_ref[...], NT) + _dot(gbi, bim_ref[...], NT)
        du_ref[...] = (du + dx_ref[...].astype(F32)).astype(du_ref.dtype)
        dyf = dy_ref[...].astype(BF16)
        dcr_ref[...] += _dot(sr[...].astype(BF16), dyf, TN)
        dci_ref[...] -= _dot(si[...].astype(BF16), dyf, TN)

    rev = lambda i: ni - 1 - i
    avec = pl.BlockSpec((1, S5_CB), lambda j, i: (0, j))
    bspec = pl.BlockSpec((S5_UB, S5_CB), lambda j, i: (j, 0))
    cspec = pl.BlockSpec((S5_CB, S5_UB), lambda j, i: (j, 0))
    states = pl.BlockSpec((tt, S5_CB), lambda j, i: (rev(i), j))
    tile = pl.BlockSpec((tt, S5_UB), lambda j, i: (rev(i), j))
    return pl.pallas_call(
        body, name=name, grid=(nj, ni),
        in_specs=[pl.BlockSpec((tt, S5_UB), lambda j, i: (rev(i), ucol + j)), avec, avec, bspec, bspec,
                  cspec, cspec, states, states, tile, tile],
        out_specs=[tile, bspec, bspec, cspec, cspec, avec, avec],
        out_shape=[_sds((L, mix), BF16), _sds((mix, S5_CB), F32), _sds((mix, S5_CB), F32),
                   _sds((S, S5_UB), F32), _sds((S, S5_UB), F32), _sds((1, S), F32), _sds((1, S), F32)],
        scratch_shapes=[pltpu.VMEM((tt, S5_CB), F32)] * 2 + [pltpu.VMEM((SUBLANES, S5_CB), F32)] * 4,
        compiler_params=_cp(),
    )(proj, a_re, a_im, b_re, b_im, c_re, c_im, s_re, s_im, dy, dxin)


def _s5_glu_fwd(ypre, proj, d, w_glu, b_glu, layer, mix, name):
    L = ypre.shape[0]
    tm = _tile(L, 512)

    def body(y_ref, x_ref, d_ref, w_ref, b_ref, o_ref):
        g = _gelu(y_ref[...] + d_ref[...] * x_ref[...].astype(F32))
        z = _dot(g.astype(BF16), w_ref[...]) + b_ref[...]
        o_ref[...] = (g * _sigmoid(z)).astype(o_ref.dtype)

    row = pl.BlockSpec((tm, mix), lambda i: (i, 0))
    vec = pl.BlockSpec((1, mix), lambda i: (0, 0))
    return pl.pallas_call(
        body, name=name, grid=(L // tm,),
        in_specs=[row, pl.BlockSpec((tm, mix), lambda i: (i, 2)), vec,
                  pl.BlockSpec((None, mix, mix), lambda i: (layer, 0, 0)), vec],
        out_specs=row, out_shape=_sds((L, mix), BF16), compiler_params=_cp(),
    )(ypre, proj, d, w_glu, b_glu)


def _s5_glu_bwd(ypre, proj, d, w_glu, b_glu, dout, layer, mix, name):
    L = ypre.shape[0]
    tm = _tile(L, 512)

    def body(y_ref, x_ref, d_ref, w_ref, b_ref, do_ref, dy_ref, dx_ref, dw_ref, db_ref, dd_ref):
        @pl.when(pl.program_id(0) == 0)
        def _():
            dw_ref[...] = jnp.zeros_like(dw_ref)
            db_ref[...] = jnp.zeros_like(db_ref)
            dd_ref[...] = jnp.zeros_like(dd_ref)

        xin = x_ref[...].astype(F32)
        dv = d_ref[...]
        g, ggrad = _gelu_and_grad(y_ref[...] + dv * xin)
        gb = g.astype(BF16)
        w = w_ref[...]
        sg = _sigmoid(_dot(gb, w) + b_ref[...])
        do = do_ref[...].astype(F32)
        dz = do * g * sg * (1.0 - sg)
        dzb = dz.astype(BF16)
        dg = do * sg + _dot(dzb, w, NT)
        dw_ref[...] += _dot(gb, dzb, TN)
        db_ref[...] += jnp.sum(dz, axis=0, keepdims=True)
        dyv = dg * ggrad
        dd_ref[...] += jnp.sum(dyv * xin, axis=0, keepdims=True)
        dy_ref[...] = dyv.astype(dy_ref.dtype)
        dx_ref[...] = (dyv * dv).astype(dx_ref.dtype)

    row = pl.BlockSpec((tm, mix), lambda i: (i, 0))
    vec = pl.BlockSpec((1, mix), lambda i: (0, 0))
    mat = pl.BlockSpec((mix, mix), lambda i: (0, 0))
    return pl.pallas_call(
        body, name=name, grid=(L // tm,),
        in_specs=[row, pl.BlockSpec((tm, mix), lambda i: (i, 2)), vec,
                  pl.BlockSpec((None, mix, mix), lambda i: (layer, 0, 0)), vec, row],
        out_specs=[row, row, mat, vec, vec],
        out_shape=[_sds((L, mix), BF16), _sds((L, mix), BF16), _sds((mix, mix), F32),
                   _sds((1, mix), F32), _sds((1, mix), F32)],
        compiler_params=_cp(),
    )(ypre, proj, d, w_glu, b_glu, dout)


def _after_matrix(t):
    j = lax.broadcasted_iota(jnp.int32, (t, t), 0)
    s = lax.broadcasted_iota(jnp.int32, (t, t), 1)
    return jnp.where(j > s, 1.0, 0.0).astype(BF16)


SB_BLOCK = 512
SB_SUB = 128


def _suffix_sum(x, m_sub):
    sub = m_sub.shape[0]
    hi = x.astype(BF16)
    lo = (x - hi.astype(F32)).astype(BF16)
    pieces, carry = [], None
    for c in reversed(range(x.shape[1] // sub)):
        cols = slice(c * sub, (c + 1) * sub)
        s = _dot(hi[:, cols], m_sub) + _dot(lo[:, cols], m_sub)
        if carry is not None:
            s = s + carry
        pieces.append(s)
        carry = s[:, 0:1] + x[:, c * sub:c * sub + 1]
    return jnp.concatenate(pieces[::-1], axis=1), carry


LOG2E = 1.4426950408889634


def _sb_block(q, k, diagonal):
    z2 = _dot(q, k, NT) * (HEAD_DIM ** -0.5 * LOG2E)
    e = jnp.exp2(-jnp.abs(z2))
    sp2 = jnp.maximum(z2, 0.0) + jnp.log(1.0 + e) * LOG2E
    if not diagonal:
        return z2, sp2, e, -sp2, None
    mask = lax.broadcasted_iota(jnp.int32, z2.shape, 1) < lax.broadcasted_iota(jnp.int32, z2.shape, 0)
    return z2, sp2, e, jnp.where(mask, -sp2, 0.0), mask


def _pair_tables(nt):
    qs = [i for i in range(nt) for _ in range(i + 1)]
    ks = [i - j for i in range(nt) for j in range(i + 1)]
    return jnp.asarray(qs, jnp.int32), jnp.asarray(ks, jnp.int32)


def _split_plan_refs(rest, n_pi, n_out, n_po, n_scratch):
    a = n_pi + n_out
    b = a + n_po
    return rest[:n_pi], rest[n_pi:a], rest[a:b], rest[b:b + n_scratch], rest[b + n_scratch:]


def _sb_fwd(proj, m_after, mix, name, plan=None):
    L = proj.shape[0]
    heads = mix // HEAD_DIM
    T = _tile(L, SB_BLOCK)
    sub = m_after.shape[0]
    nt = L // T
    npairs = nt * (nt + 1) // 2
    qc, kc, vc = (3 * mix // HEAD_DIM, 4 * mix // HEAD_DIM, 5 * mix // HEAD_DIM)
    n_pi, n_po = (len(plan.operands), len(plan.out_shape)) if plan is not None else (0, 0)

    def body(qt, kt, q_ref, k_ref, v_ref, m_ref, *rest):
        p_ins, (o_ref, o32_ref), p_outs, (acc, ra), p_sems = _split_plan_refs(rest, n_pi, 2, n_po, 2)
        h, p = pl.program_id(0), pl.program_id(1)
        qi, ki = qt[p], kt[p]

        if plan is not None:
            @pl.when((h == 0) & (p == 0))
            def _():
                plan.start(p_ins, p_outs, p_sems)

        def block(diagonal):
            z2, sp2, _, lg2, mask = _sb_block(q_ref[...], k_ref[...], diagonal)
            after, rs = _suffix_sum(lg2, m_ref[...])
            if not diagonal:
                after = after + ra[...]
            w = jnp.exp2(z2 - sp2 + after)
            if diagonal:
                w = jnp.where(mask, w, 0.0)
            pv = _dot(w.astype(BF16), v_ref[...])
            if diagonal:
                acc[...] = pv
                ra[...] = rs
            else:
                acc[...] += pv
                ra[...] += rs

        pl.when(ki == qi)(functools.partial(block, True))
        pl.when(ki != qi)(functools.partial(block, False))

        @pl.when(ki == 0)
        def _():
            o_ref[...] = acc[...].astype(o_ref.dtype)
            o32_ref[...] = acc[...]

        if plan is not None:
            @pl.when((h == heads - 1) & (p == npairs - 1))
            def _():
                plan.finish(p_ins, p_outs, p_sems)

    qtab, ktab = _pair_tables(nt)
    kv = lambda col: pl.BlockSpec((T, HEAD_DIM), lambda h, p, qt, kt: (kt[p], col + h))
    qo = pl.BlockSpec((T, HEAD_DIM), lambda h, p, qt, kt: (qt[p], h))
    outs = pl.pallas_call(
        body, name=name,
        grid_spec=pltpu.PrefetchScalarGridSpec(
            num_scalar_prefetch=2, grid=(heads, npairs),
            in_specs=[pl.BlockSpec((T, HEAD_DIM), lambda h, p, qt, kt: (qt[p], qc + h)), kv(kc), kv(vc),
                      pl.BlockSpec((sub, sub), lambda h, p, qt, kt: (0, 0))] + [HBM] * n_pi,
            out_specs=[qo, qo] + [HBM] * n_po,
            scratch_shapes=[pltpu.VMEM((T, HEAD_DIM), F32), pltpu.VMEM((T, 1), F32)]
            + (plan.scratch if plan is not None else [])),
        out_shape=[_sds((L, mix), BF16), _sds((L, mix), F32)] + (plan.out_shape if plan is not None else []),
        compiler_params=_cp(),
    )(qtab, ktab, proj, proj, proj, m_after, *(plan.operands if plan is not None else []))
    return outs[0], outs[1], list(outs[2:])


def _sb_bwd(proj, m_after, out, dout, mix, name, plan=None):
    L = proj.shape[0]
    heads = mix // HEAD_DIM
    T = _tile(L, SB_BLOCK)
    sub = m_after.shape[0]
    nt = L // T
    npairs = nt * (nt + 1) // 2
    qc, kc, vc = (3 * mix // HEAD_DIM, 4 * mix // HEAD_DIM, 5 * mix // HEAD_DIM)
    scale = HEAD_DIM ** -0.5
    n_pi, n_po = (len(plan.operands), len(plan.out_shape)) if plan is not None else (0, 0)

    def body(qt, kt, q_ref, k_ref, v_ref, m_ref, o_ref, do_ref, *rest):
        p_ins, (dq_ref, dk_ref, dv_ref), p_outs, (dq_acc, ra, rp, delta), p_sems = _split_plan_refs(
            rest, n_pi, 3, n_po, 4)
        h, p = pl.program_id(0), pl.program_id(1)
        qi, ki = qt[p], kt[p]

        if plan is not None:
            @pl.when((h == 0) & (p == 0))
            def _():
                plan.start(p_ins, p_outs, p_sems)

        @pl.when(p == 0)
        def _():
            dk_ref[...] = jnp.zeros_like(dk_ref)
            dv_ref[...] = jnp.zeros_like(dv_ref)

        def block(diagonal):
            q, k, v, do = q_ref[...], k_ref[...], v_ref[...], do_ref[...]
            z2, sp2, e, lg2, mask = _sb_block(q, k, diagonal)
            after, rs_a = _suffix_sum(lg2, m_ref[...])
            if diagonal:
                dl = jnp.sum(do.astype(F32) * o_ref[...], axis=1, keepdims=True)
                delta[...] = dl
            else:
                dl = delta[...]
                after = after + ra[...]
            w = jnp.exp2(z2 - sp2 + after)
            if diagonal:
                w = jnp.where(mask, w, 0.0)
            wb = w.astype(BF16)
            pm = wb.astype(F32) * _dot(do, v, NT)
            suffix, rs_p = _suffix_sum(pm, m_ref[...])
            if not diagonal:
                suffix = suffix + rp[...]
            rcp = 1.0 / (1.0 + e)
            beta = jnp.where(z2 >= 0, rcp, e * rcp)
            dz = pm - beta * (dl - suffix)
            if diagonal:
                dz = jnp.where(mask, dz, 0.0)
            dzb = dz.astype(BF16)
            rows = pl.ds(pl.multiple_of(ki * T, T), T)
            dk_ref[rows, :] += _dot(dzb, q, TN) * scale
            dv_ref[rows, :] += _dot(wb, do, TN)
            dq = _dot(dzb, k) * scale
            if diagonal:
                dq_acc[...] = dq
                ra[...] = rs_a
                rp[...] = rs_p
            else:
                dq_acc[...] += dq
                ra[...] += rs_a
                rp[...] += rs_p

        pl.when(ki == qi)(functools.partial(block, True))
        pl.when(ki != qi)(functools.partial(block, False))

        @pl.when(ki == 0)
        def _():
            dq_ref[...] = dq_acc[...]

        if plan is not None:
            @pl.when((h == heads - 1) & (p == npairs - 1))
            def _():
                plan.finish(p_ins, p_outs, p_sems)

    qtab, ktab = _pair_tables(nt)
    kv = lambda col: pl.BlockSpec((T, HEAD_DIM), lambda h, p, qt, kt: (kt[p], col + h))
    qo = pl.BlockSpec((T, HEAD_DIM), lambda h, p, qt, kt: (qt[p], h))
    whole = pl.BlockSpec((L, HEAD_DIM), lambda h, p, qt, kt: (0, h))
    outs = pl.pallas_call(
        body, name=name,
        grid_spec=pltpu.PrefetchScalarGridSpec(
            num_scalar_prefetch=2, grid=(heads, npairs),
            in_specs=[pl.BlockSpec((T, HEAD_DIM), lambda h, p, qt, kt: (qt[p], qc + h)), kv(kc), kv(vc),
                      pl.BlockSpec((sub, sub), lambda h, p, qt, kt: (0, 0)), qo, qo] + [HBM] * n_pi,
            out_specs=[qo, whole, whole] + [HBM] * n_po,
            scratch_shapes=[pltpu.VMEM((T, HEAD_DIM), F32), pltpu.VMEM((T, 1), F32), pltpu.VMEM((T, 1), F32),
                            pltpu.VMEM((T, 1), F32)] + (plan.scratch if plan is not None else [])),
        out_shape=[_sds((L, mix), F32)] * 3 + (plan.out_shape if plan is not None else []),
        compiler_params=_cp(),
    )(qtab, ktab, proj, proj, proj, m_after, out, dout, *(plan.operands if plan is not None else []))
    return outs[0], outs[1], outs[2], list(outs[3:])


def _merge_fwd(ys, w_branch, proj, b_gate, layer, b_layer, name):
    L, mix = ys[0].shape
    bw = w_branch.shape[-1]
    D = bw * NDEV
    tm = _tile(L, 512)
    gc = 6 * mix // bw

    def body(ya, yb, yc, w_ref, pa, pb, pc, b_ref, o_ref, ot_ref):
        acc = None
        for n, (y_ref, p_ref) in enumerate(((ya, pa), (yb, pb), (yc, pc))):
            gate = _sigmoid(p_ref[...].astype(F32) + b_ref[n:n + 1, :])
            term = gate * _dot(y_ref[...], w_ref[n])
            acc = term if acc is None else acc + term
        o_ref[...] = acc.astype(o_ref.dtype)
        ot_ref[...] = acc.T.astype(ot_ref.dtype)

    yspec = pl.BlockSpec((tm, mix), lambda i, j: (i, 0))
    pspec = [pl.BlockSpec((tm, bw), functools.partial(lambda i, j, n: (i, gc + n * NDEV + j), n=n)) for n in range(3)]
    return pl.pallas_call(
        body, name=name, grid=(L // tm, NDEV),
        in_specs=[yspec, yspec, yspec,
                  pl.BlockSpec((None, None, 3, mix, bw), lambda i, j: (layer, j, 0, 0, 0)),
                  *pspec, pl.BlockSpec((None, None, 3, bw), lambda i, j: (b_layer, j, 0, 0))],
        out_specs=[pl.BlockSpec((tm, bw), lambda i, j: (i, j)), pl.BlockSpec((bw, tm), lambda i, j: (j, i))],
        out_shape=[_sds((L, D), BF16), _sds((D, L), BF16)], compiler_params=_cp(),
    )(*ys, w_branch, proj, proj, proj, b_gate)


def _merge_bwd(ys, w_branch, proj, b_gate, dmerged, layer, b_layer, name):
    L, mix = ys[0].shape
    bw = w_branch.shape[-1]
    D = bw * NDEV
    tm = _tile(L, 512)
    gc = 6 * mix // bw

    def body(ya, yb, yc, w_ref, pa, pb, pc, b_ref, dm_ref,
             dpa, dpb, dpc, dba, dbb, dbc, dya, dyb, dyc, dbg_ref, acc):
        i, j = pl.program_id(0), pl.program_id(1)

        @pl.when((i == 0) & (j == 0))
        def _():
            dbg_ref[...] = jnp.zeros_like(dbg_ref)

        @pl.when(j == 0)
        def _():
            acc[...] = jnp.zeros_like(acc)

        dm = dm_ref[...].astype(F32)
        for n, (y_ref, p_ref, dp_ref, db_ref) in enumerate(((ya, pa, dpa, dba), (yb, pb, dpb, dbb), (yc, pc, dpc, dbc))):
            gate = _sigmoid(p_ref[...].astype(F32) + b_ref[n:n + 1, :])
            br = _dot(y_ref[...], w_ref[n])
            dp = dm * br * gate * (1.0 - gate)
            dp_ref[...] = dp.astype(dp_ref.dtype)
            dbg_ref[j, n:n + 1, :] += jnp.sum(dp, axis=0, keepdims=True)
            dbr = (dm * gate).astype(BF16)
            db_ref[...] = dbr
            acc[n] += _dot(dbr, w_ref[n], NT)

        @pl.when(j == NDEV - 1)
        def _():
            for n, dy_ref in enumerate((dya, dyb, dyc)):
                dy_ref[...] = acc[n].astype(dy_ref.dtype)

    yspec = pl.BlockSpec((tm, mix), lambda i, j: (i, 0))
    ospec = pl.BlockSpec((tm, bw), lambda i, j: (i, j))
    pspec = [pl.BlockSpec((tm, bw), functools.partial(lambda i, j, n: (i, gc + n * NDEV + j), n=n)) for n in range(3)]
    return pl.pallas_call(
        body, name=name, grid=(L // tm, NDEV),
        in_specs=[yspec, yspec, yspec,
                  pl.BlockSpec((None, None, 3, mix, bw), lambda i, j: (layer, j, 0, 0, 0)),
                  *pspec, pl.BlockSpec((None, None, 3, bw), lambda i, j: (b_layer, j, 0, 0)), ospec],
        out_specs=[ospec] * 6 + [yspec] * 3 + [pl.BlockSpec((NDEV, 3, bw), lambda i, j: (0, 0, 0))],
        out_shape=[_sds((L, D), BF16)] * 6 + [_sds((L, mix), BF16)] * 3 + [_sds((NDEV, 3, bw), F32)],
        scratch_shapes=[pltpu.VMEM((3, tm, mix), F32)],
        compiler_params=_cp(),
    )(*ys, w_branch, proj, proj, proj, b_gate, dmerged)


def _ktile(n):
    for t in (1024, 768, 512, 384, 256, 128, 64, 32, 16, 8):
        if n % t == 0:
            return t
    return n


MM_ROWS = 1024
MM_COLS = 1024
MM_DEPTH = 2048


def _mm_cols(name, a, wg, layer, out_dtype, epi=None, with_transpose=False, plan=None):
    M, K = a.shape
    nb = wg.shape[3]
    tm, tn = _tile(M, MM_ROWS), _tile(nb, MM_COLS, LANES)
    r = nb // tn
    t_out = dict(t_spec=pl.BlockSpec((tn, tm), lambda i, j, k: (j, i)),
                 t_shape=_sds((NDEV * nb, M), out_dtype)) if with_transpose else {}
    return _matmul(
        name, a, wg, grid=(M // tm, NDEV * r, 1),
        a_spec=pl.BlockSpec((tm, K), lambda i, j, k: (i, 0)),
        b_spec=pl.BlockSpec((None, None, K, tn), lambda i, j, k: (layer, j // r, 0, j % r)),
        o_spec=pl.BlockSpec((tm, tn), lambda i, j, k: (i, j)),
        out_shape=_sds((M, NDEV * nb), out_dtype), dims=NN, acc_shape=(tm, tn), epi=epi, plan=plan, **t_out)


def _mm_cols_t(name, a, wg, layer, out_dtype, plan=None):
    M = a.shape[0]
    K, nb = wg.shape[2], wg.shape[3]
    tm, tn, tk = _tile(M, MM_ROWS), _tile(K, MM_COLS, LANES), _ktile(nb)
    r = nb // tk
    return _matmul(
        name, a, wg, grid=(M // tm, K // tn, NDEV * r),
        a_spec=pl.BlockSpec((tm, tk), lambda i, j, k: (i, k)),
        b_spec=pl.BlockSpec((None, None, tn, tk), lambda i, j, k: (layer, k // r, j, k % r)),
        o_spec=pl.BlockSpec((tm, tn), lambda i, j, k: (i, j)),
        out_shape=_sds((M, K), out_dtype), dims=NT, acc_shape=(tm, tn), plan=plan)


def _mm_rows(name, a, wn, layer, out_dtype, res, a_pro=None):
    M, K = a.shape
    N = wn.shape[2]
    tm, tn, tk = _tile(M, MM_ROWS), _tile(N, MM_COLS // 2, LANES), _tile(K, MM_DEPTH, LANES)
    tile = pl.BlockSpec((tm, tn), lambda i, j, k: (i, j))
    return _matmul(
        name, a, wn, grid=(M // tm, N // tn, K // tk),
        a_spec=pl.BlockSpec((tm, tk), lambda i, j, k: (i, k)),
        b_spec=pl.BlockSpec((None, tk, tn), lambda i, j, k: (layer, k, j)),
        o_spec=tile, out_shape=_sds((M, N), out_dtype), dims=NN, acc_shape=(tm, tn),
        extra=(res,), extra_specs=(tile,), epi=lambda acc, rv: acc + rv, a_pro=a_pro)


def _mm_rows_t(name, a, wn, layer, out_dtype, extra=(), epi=None):
    M, N = a.shape
    K = wn.shape[1]
    tm, tn = _tile(M, MM_ROWS), _tile(K, MM_COLS, LANES)
    tile = pl.BlockSpec((tm, tn), lambda i, j, k: (i, j))
    return _matmul(
        name, a, wn, grid=(M // tm, K // tn, 1),
        a_spec=pl.BlockSpec((tm, N), lambda i, j, k: (i, 0)),
        b_spec=pl.BlockSpec((None, tn, N), lambda i, j, k: (layer, j, 0)),
        o_spec=tile, out_shape=_sds((M, K), out_dtype), dims=NT, acc_shape=(tm, tn),
        extra=extra, extra_specs=(tile,) * len(extra), epi=epi)


def _mm_grad(name, a, dy, *, tokens_last, o_block, o_map, tn, out_shape=None, into=None, a_pro=None):
    K, L = a.shape if tokens_last else a.shape[::-1]
    N = dy.shape[1]
    tm, tt = _tile(K, MM_ROWS), _tile(L, MM_ROWS, LANES)
    a_spec = (pl.BlockSpec((tm, tt), lambda i, j, k: (i, k)) if tokens_last
              else pl.BlockSpec((tt, tm), lambda i, j, k: (k, i)))
    return _matmul(
        name, a, dy, grid=(K // tm, N // tn, L // tt), a_spec=a_spec,
        b_spec=pl.BlockSpec((tt, tn), lambda i, j, k: (k, j)),
        o_spec=pl.BlockSpec(o_block(tm, tn), o_map), out_shape=out_shape,
        dims=NN if tokens_last else TN, acc_shape=(tm, tn), a_pro=a_pro, into=into)


def _grad_cols(name, a_t, dy):
    nb = dy.shape[1] // NDEV
    tn = _tile(nb, MM_COLS, LANES)
    r = nb // tn
    return _mm_grad(name, a_t, dy, tokens_last=True, o_block=lambda tm, t: (None, tm, t),
                    o_map=lambda i, j, k: (j // r, i, j % r), tn=tn,
                    out_shape=_sds((NDEV, a_t.shape[0], nb), BF16))


def _grad_rows(name, a_t, dy, a_pro=None):
    tn = _tile(dy.shape[1], MM_COLS, LANES)
    return _mm_grad(name, a_t, dy, tokens_last=True, o_block=lambda tm, t: (tm, t),
                    o_map=lambda i, j, k: (i, j), tn=tn,
                    out_shape=_sds((a_t.shape[0], dy.shape[1]), BF16), a_pro=a_pro)


def _grad_branch(name, y, dbr, into, n):
    bw = into.shape[3]
    return _mm_grad(name, y, dbr, tokens_last=False, o_block=lambda tm, t: (None, None, tm, t),
                    o_map=lambda i, j, k: (j, n, i, 0), tn=bw, into=into)


def _s5_discretize(lam_re, lam_im, log_dt, b_re, b_im):
    dt = jnp.exp(log_dt)[:, None]
    mag = jnp.exp(lam_re * dt)
    ab_re = mag * jnp.cos(lam_im * dt)
    ab_im = mag * jnp.sin(lam_im * dt)
    den = lam_re * lam_re + lam_im * lam_im
    n_re = ab_re - 1.0
    n_im = ab_im
    k_re = (n_re * lam_re + n_im * lam_im) / den
    k_im = (n_im * lam_re - n_re * lam_im) / den
    bb_re = k_re[..., None] * b_re - k_im[..., None] * b_im
    bb_im = k_re[..., None] * b_im + k_im[..., None] * b_re
    return ab_re, ab_im, bb_re, bb_im


def _s5_b_blocks(bb):
    g, p, h = bb.shape
    t = bb.reshape(g // 8, 8, p, h)
    return jnp.einsum('jiph,ik->jihkp', t, jnp.eye(8, dtype=bb.dtype)).reshape(g * h, 8 * p)


def _s5_b_unblock(m, g, p, h):
    t = m.reshape(g // 8, 8, h, 8, p)
    return jnp.einsum('jihkp,ik->jiph', t, jnp.eye(8, dtype=m.dtype)).reshape(g, p, h)


def _s5_c_blocks(c):
    g, h, p = c.shape
    t = c.reshape(g // 8, 8, h, p)
    return jnp.einsum('jihp,ik->jipkh', t, jnp.eye(8, dtype=c.dtype)).reshape(g * p, 8 * h)


def _s5_c_unblock(m, g, h, p):
    t = m.reshape(g // 8, 8, p, 8, h)
    return jnp.einsum('jipkh,ik->jihp', t, jnp.eye(8, dtype=m.dtype)).reshape(g, h, p)


BIG = ("w_in", "w_branch", "w_out", "w_mlp_in", "w_mlp_out", "s5_w_glu")
SMALL = ("norm1_g", "gm_norm_g", "gm_w_s", "gm_b_s", "s5_lambda_re", "s5_lambda_im", "s5_log_dt",
         "s5_b_re", "s5_b_im", "s5_c_re", "s5_c_im", "s5_d", "s5_b_glu", "norm2_g", "final_g", "b_gate")
WEIGHTS = ("norm1_g", "w_in", "b_gate", "gm_norm_g", "gm_w_s", "gm_b_s", "s5_lambda_re", "s5_lambda_im",
           "s5_log_dt", "s5_b_re", "s5_b_im", "s5_c_re", "s5_c_im", "s5_d", "s5_w_glu", "s5_b_glu",
           "w_branch", "w_out", "norm2_g", "w_mlp_in", "w_mlp_out", "final_g")
SMALL_LATE = ("norm1_g",)
SMALL_EARLY = tuple(n for n in SMALL if n not in SMALL_LATE)
FLAT_ROWS = 512


def _pack(arrays):
    flat = jnp.concatenate([a.reshape(-1) for a in arrays])
    unit = FLAT_ROWS * LANES
    pad = (-flat.shape[0]) % unit
    return jnp.pad(flat, (0, pad)).reshape(-1, LANES)


def _unpack(flat2d, shapes):
    flat = flat2d.reshape(-1)
    out, off = [], 0
    for s in shapes:
        n = math.prod(s)
        out.append(flat[off:off + n].reshape(s))
        off += n
    return out


def kernel(x, norm1_g, w_in, b_gate, gm_norm_g, gm_w_s, gm_b_s, s5_lambda_re, s5_lambda_im, s5_log_dt, s5_b_re, s5_b_im, s5_c_re, s5_c_im, s5_d, s5_w_glu, s5_b_glu, w_branch, w_out, norm2_g, w_mlp_in, w_mlp_out, final_g, loss_target, m_norm1_g, m_w_in, m_b_gate, m_gm_norm_g, m_gm_w_s, m_gm_b_s, m_s5_lambda_re, m_s5_lambda_im, m_s5_log_dt, m_s5_b_re, m_s5_b_im, m_s5_c_re, m_s5_c_im, m_s5_d, m_s5_w_glu, m_s5_b_glu, m_w_branch, m_w_out, m_norm2_g, m_w_mlp_in, m_w_mlp_out, m_final_g, v_norm1_g, v_w_in, v_b_gate, v_gm_norm_g, v_gm_w_s, v_gm_b_s, v_s5_lambda_re, v_s5_lambda_im, v_s5_log_dt, v_s5_b_re, v_s5_b_im, v_s5_c_re, v_s5_c_im, v_s5_d, v_s5_w_glu, v_s5_b_glu, v_w_branch, v_w_out, v_norm2_g, v_w_mlp_in, v_w_mlp_out, v_final_g):
    P = dict(locals())
    W = {n: P[n] for n in WEIGHTS}
    M1 = {n: P["m_" + n] for n in WEIGHTS}
    V2 = {n: P["v_" + n] for n in WEIGHTS}

    L, D = x.shape[1], x.shape[2]
    depth = norm1_g.shape[0]
    mix = D // 2
    nb_in, ffb, bw = w_in.shape[2], w_mlp_in.shape[2], w_branch.shape[3]
    ff = ffb * NDEV
    s5_groups = mix // S5_GROUP_CH
    assert mix % S5_UB == 0 and L % CHUNK == 0 and w_in.shape[2] * NDEV == 6 * mix + 3 * D

    xi, yi, ci = _mesh_pos()
    core = ci.astype(jnp.int32).reshape(1)
    chip = (2 * xi + yi).astype(jnp.int32).reshape(1)
    dev = 4 * xi + 2 * yi + ci

    local16 = {n: P[n].astype(BF16) for n in BIG}
    natural = {"w_in": lambda g: g[None], "w_branch": lambda g: g[None], "w_mlp_in": lambda g: g[None],
               "w_out": lambda g: g.reshape(1, D, D), "w_mlp_out": lambda g: g.reshape(1, ff, D),
               "s5_w_glu": lambda g: g.reshape(1, mix, mix)}
    behind_proj = ("w_branch", "w_out", "s5_w_glu")

    first = _run_plan(_gather_plan([(local16["w_in"], 0), (b_gate, None)]), "gather_first")
    w_in_next = natural["w_in"](first[0])
    bg = jnp.swapaxes(first[1], 0, 1)

    m_after = _after_matrix(min(SB_SUB, _tile(L, SB_BLOCK)))

    xcur = x[0]
    saved = []
    for l in range(depth):
        wl = {"w_in": w_in_next}
        h, h_t = _rms_fwd(xcur, norm1_g[l][None], f"rms1_l{l}")
        proj, landed = _mm_cols(f"proj_l{l}", h, wl["w_in"], 0, BF16,
                                plan=_gather_plan([(local16[n], l) for n in behind_proj]))
        wl.update({n: natural[n](g) for n, g in zip(behind_proj, landed)})
        ya = _gmlp_fwd(proj, gm_norm_g[l][None], gm_w_s[l], gm_b_s[l][..., None], mix, f"gmlp_l{l}")
        ab_re, ab_im, bb_re, bb_im = _s5_discretize(s5_lambda_re[l], s5_lambda_im[l], s5_log_dt[l],
                                                    s5_b_re[l], s5_b_im[l])
        s5p = (ab_re.reshape(1, -1), ab_im.reshape(1, -1),
               _s5_b_blocks(bb_re).astype(BF16), _s5_b_blocks(bb_im).astype(BF16),
               _s5_c_blocks(s5_c_re[l]).astype(BF16), _s5_c_blocks(s5_c_im[l]).astype(BF16))
        ypre, sb_re, sb_im, landed = _s5_scan_fwd(proj, *s5p, mix, f"s5scan_l{l}",
                                                  plan=_gather_plan([(local16["w_mlp_in"], l)]))
        wl["w_mlp_in"] = natural["w_mlp_in"](landed[0])
        yb = _s5_glu_fwd(ypre, proj, s5_d[l][None], wl["s5_w_glu"], s5_b_glu[l][None], 0, mix, f"s5glu_l{l}")
        ahead = _gather_plan([(local16["w_in"], l + 1)]) if l + 1 < depth else None
        yc, yc32, landed = _sb_fwd(proj, m_after, mix, f"sb_l{l}", plan=ahead)
        if ahead is not None:
            w_in_next = natural["w_in"](landed[0])
        merged, merged_t = _merge_fwd((ya, yb, yc), wl["w_branch"], proj, bg, 0, l, f"merge_l{l}")
        xmid = _mm_rows(f"wout_l{l}", merged, wl["w_out"], 0, F32, xcur)
        h2, h2_t = _rms_fwd(xmid, norm2_g[l][None], f"rms2_l{l}")
        (r, r_t), landed = _mm_cols(f"mlpin_l{l}", h2, wl["w_mlp_in"], 0, BF16,
                                    epi=lambda acc: jnp.maximum(acc, 0.0), with_transpose=True,
                                    plan=_gather_plan([(local16["w_mlp_out"], l)]))
        wl["w_mlp_out"] = natural["w_mlp_out"](landed[0])
        xout = _mm_rows(f"mlpout_l{l}", r, wl["w_mlp_out"], 0, F32, xmid, a_pro=lambda t: t * t)
        saved.append(dict(x=xcur, h_t=h_t, proj=proj, ys=(ya, yb, yc), s5p=s5p, ypre=ypre, sb=(sb_re, sb_im),
                          yc32=yc32, merged_t=merged_t, xmid=xmid, h2_t=h2_t, r=r, r_t=r_t, w=wl))
        xcur = xout

    loss_tile, dx, dxb, d_final_g = _loss_head(xcur, final_g[None], loss_target[0], "loss_head")
    loss = lax.psum(loss_tile[0, 0], ("x", "y", "c"))

    big_w = {"w_in": (D, nb_in), "w_branch": (3 * mix, bw), "w_out": (D // NDEV, D),
             "w_mlp_in": (D, ffb), "w_mlp_out": (ffb, D), "s5_w_glu": (mix // NDEV, mix)}
    small = {n: [None] * depth for n in SMALL if n != "final_g"}
    small_shapes = {n: (W[n].shape if n != "b_gate" else (depth, 3, D)) for n in SMALL}
    pair_sums = [None] * depth
    chip_parts = [None] * depth

    for l in reversed(range(depth)):
        sv = saved[l]
        wl = sv["w"]
        d_a = _mm_rows_t(f"d_act_l{l}", dxb, wl["w_mlp_out"], 0, BF16, extra=(sv["r"],),
                         epi=lambda acc, rv: acc * (2.0 * rv.astype(F32)))
        g_mo = _grad_rows(f"g_mlpout_l{l}", sv["r_t"], dxb, a_pro=lambda t: t * t)
        g_mi = _grad_cols(f"g_mlpin_l{l}", sv["h2_t"], d_a)
        dh2 = _mm_cols_t(f"d_h2_l{l}", d_a, wl["w_mlp_in"], 0, F32)
        dxm, dxmb, small["norm2_g"][l] = _rms_bwd(dh2, sv["xmid"], norm2_g[l][None], dx, f"rms2_bwd_l{l}")
        d_merged = _mm_rows_t(f"d_merged_l{l}", dxmb, wl["w_out"], 0, BF16)
        g_out = _grad_rows(f"g_wout_l{l}", sv["merged_t"], dxmb)
        (dpa, dpb, dpc, dba, dbb, dbc, dya, dyb, dyc, dbg) = _merge_bwd(
            sv["ys"], wl["w_branch"], sv["proj"], bg, d_merged, 0, l, f"merge_bwd_l{l}")
        small["b_gate"][l] = jnp.transpose(dbg, (1, 0, 2)).reshape(3, D)
        g_br = lax.empty((NDEV, 3, mix, bw), BF16)
        for n, dbr in enumerate((dba, dbb, dbc)):
            g_br = _grad_branch(f"g_branch{n}_l{l}", sv["ys"][n], dbr, g_br, n)
        d_uv, d_gn, d_ws, d_bs = _gmlp_bwd(sv["proj"], gm_norm_g[l][None], gm_w_s[l],
                                           jnp.swapaxes(gm_w_s[l], 1, 2), gm_b_s[l][..., None], dya, mix,
                                           f"gmlp_bwd_l{l}")
        small["gm_norm_g"][l], small["gm_w_s"][l], small["gm_b_s"][l] = d_gn, d_ws, d_bs[..., 0]
        d_ypre, d_xin, dw_glu, db_glu, dd = _s5_glu_bwd(sv["ypre"], sv["proj"], s5_d[l][None], wl["s5_w_glu"],
                                                        s5_b_glu[l][None], dyb, 0, mix, f"s5glu_bwd_l{l}")
        g_glu = dw_glu.astype(BF16)
        small["s5_b_glu"][l], small["s5_d"][l] = db_glu, dd
        d_s5in, dbm_re, dbm_im, dcm_re, dcm_im, da_re, da_im = _s5_scan_bwd(
            sv["proj"], *sv["s5p"], *sv["sb"], d_ypre, d_xin, mix, f"s5scan_bwd_l{l}")
        small["s5_c_re"][l] = _s5_c_unblock(dcm_re, s5_groups, S5_GROUP_CH, S5_STATE)
        small["s5_c_im"][l] = _s5_c_unblock(dcm_im, s5_groups, S5_GROUP_CH, S5_STATE)
        _, disc_vjp = jax.vjp(_s5_discretize, s5_lambda_re[l], s5_lambda_im[l], s5_log_dt[l],
                              s5_b_re[l], s5_b_im[l])
        (small["s5_lambda_re"][l], small["s5_lambda_im"][l], small["s5_log_dt"][l],
         small["s5_b_re"][l], small["s5_b_im"][l]) = disc_vjp(
            (da_re.reshape(s5_groups, S5_STATE), da_im.reshape(s5_groups, S5_STATE),
             _s5_b_unblock(dbm_re, s5_groups, S5_STATE, S5_GROUP_CH),
             _s5_b_unblock(dbm_im, s5_groups, S5_STATE, S5_GROUP_CH)))
        above = _chips_plan(pair_sums[l + 1]) if l + 1 < depth else None
        dq, dk, dv, landed = _sb_bwd(sv["proj"], m_after, sv["yc32"], dyc, mix, f"sb_bwd_l{l}", plan=above)
        if above is not None:
            chip_parts[l + 1] = landed
        dproj = jnp.concatenate([d_uv, d_s5in, dq.astype(BF16), dk.astype(BF16), dv.astype(BF16),
                                 dpa, dpb, dpc], axis=1)
        g_in = _grad_cols(f"g_win_l{l}", sv["h_t"], dproj)
        if l == 0:
            early = {n: jnp.stack(small[n]) for n in SMALL_EARLY if n != "final_g"}
            early["final_g"] = d_final_g[0]
            packed_early = _pack([early[n].reshape(small_shapes[n]) for n in SMALL_EARLY])
            dh, landed = _mm_cols_t(f"d_h_l{l}", dproj, wl["w_in"], 0, F32,
                                    plan=_gather_plan([(packed_early, None)]))
            gathered_early = landed[0]
        else:
            dh = _mm_cols_t(f"d_h_l{l}", dproj, wl["w_in"], 0, F32)
        dx, dxb, small["norm1_g"][l] = _rms_bwd(dh, sv["x"], norm1_g[l][None], dxm, f"rms1_bwd_l{l}")
        parts = {"w_in": g_in, "w_branch": g_br, "w_out": g_out, "w_mlp_in": g_mi, "w_mlp_out": g_mo,
                 "s5_w_glu": g_glu}
        gs = [parts[n].reshape(NDEV, *big_w[n]) for n in BIG]
        r1 = _run_plan(_pair_plan(gs), f"reduce_pair_l{l}")
        pair_sums[l] = [_pair_add(g, rr, core, f"pair_add_{n}_l{l}") for g, rr, n in zip(gs, r1, BIG)]

    grad_x = dx[None]
    chip_parts[0] = _run_plan(_chips_plan(pair_sums[0]), "reduce_chips_l0")

    grads, deltas, new_m, new_v = {}, {}, {}, {}
    for t, n in enumerate(BIG):
        shp = (depth, *big_w[n])
        outs = [lax.empty(shp, F32) for _ in range(4)]
        for l in range(depth):
            outs = _adamw_big(pair_sums[l][t], chip_parts[l][t], chip, W[n].reshape(shp), M1[n].reshape(shp),
                              V2[n].reshape(shp), l, outs, f"adamw_{n}_l{l}")
        grads[n], deltas[n], new_m[n], new_v[n] = (o.reshape(W[n].shape) for o in outs)

    packed_late = _pack([jnp.stack(small[n]).reshape(small_shapes[n]) for n in SMALL_LATE])
    gathered_late = _run_plan(_gather_plan([(packed_late, None)]), "gather_small")[0]
    summed = dict(zip(SMALL_EARLY, _unpack(_sum_devices(gathered_early, "sum_small_early"),
                                           [small_shapes[n] for n in SMALL_EARLY])))
    summed.update(zip(SMALL_LATE, _unpack(_sum_devices(gathered_late, "sum_small_late"),
                                          [small_shapes[n] for n in SMALL_LATE])))
    for n in SMALL:
        g = summed[n]
        grads[n] = g if n != "b_gate" else lax.dynamic_slice_in_dim(g, dev * bw, bw, axis=2)
    shapes = [W[n].shape for n in SMALL]
    d_s, m_s, v_s = _adamw_flat(_pack([grads[n] for n in SMALL]), _pack([W[n] for n in SMALL]),
                                _pack([M1[n] for n in SMALL]), _pack([V2[n] for n in SMALL]), "adamw_small")
    for n, d, mm, vv in zip(SMALL, _unpack(d_s, shapes), _unpack(m_s, shapes), _unpack(v_s, shapes)):
        deltas[n], new_m[n], new_v[n] = d, mm, vv

    return (loss, grad_x, *[grads[n] for n in WEIGHTS], *[deltas[n] for n in WEIGHTS],
            *[new_m[n] for n in WEIGHTS], *[new_v[n] for n in WEIGHTS])
```
